```python
import math
import jax, jax.numpy as jnp
from jax import lax
import numpy as np

D_MODEL = 2048
BATCH = 1
SEQ = 8192
DEPTH = 4

GRID_W = 64
CTX_LEN = 256
HEAD_DIM = 128
BRANCH_WIDTH = 1024
A_Q_HEADS = 8
A_KV_HEADS = 2
WINDOW = 128
A_BLOCK = 128
B_GROUPS = 8
B_GROUP_DIM = 128
B_CHUNK = 128
C_HEADS = 8
C_CHUNK = 128
N_GROUPS = 4
EXPERTS_PER_GROUP = 8
TOP_K = 2
D_EXPERT = 512
MOE_BLOCK = 128

N_BRANCH = 3
ROPE_BASE = 10000.0
EPS = 1e-6
NEG_INF = -1e30
F32 = jnp.float32

A_WIDTH = A_Q_HEADS * HEAD_DIM
A_KV_WIDTH = A_KV_HEADS * HEAD_DIM
B_WIDTH = B_GROUPS * B_GROUP_DIM
C_WIDTH = C_HEADS * HEAD_DIM
N_EXPERTS = N_GROUPS * EXPERTS_PER_GROUP
IN_SIZES = (A_KV_WIDTH, A_KV_WIDTH, C_WIDTH, C_WIDTH, A_WIDTH, C_WIDTH, C_WIDTH, B_WIDTH, B_WIDTH, N_BRANCH * D_MODEL)
IN_WIDTH = sum(IN_SIZES)
IN_SPLITS = tuple(int(s) for s in np.cumsum(IN_SIZES)[:-1])
CTX_STATE_WIDTH = 2 * A_KV_WIDTH + 2 * C_WIDTH

kernel_name = "hybrid_dit_window_gmlp_retention_hmoe"


def rmsnorm(x, g):
    xf = x.astype(F32)
    y = xf * lax.rsqrt(jnp.mean(xf * xf, axis=-1, keepdims=True) + EPS)
    return (y * g.astype(F32)).astype(x.dtype)


def groupnorm(x, g):
    xf = x.astype(F32)
    xc = xf - jnp.mean(xf, axis=-1, keepdims=True)
    y = xc * lax.rsqrt(jnp.mean(xc * xc, axis=-1, keepdims=True) + EPS)
    return (y * g.astype(F32)).astype(x.dtype)


def modulate(h, shift, scale):
    return h * (1.0 + scale) + shift


def axial_rope_tables(n):
    rows = n // GRID_W
    assert rows * GRID_W == n
    row = jnp.repeat(jnp.arange(rows, dtype=F32), GRID_W)
    col = jnp.tile(jnp.arange(GRID_W, dtype=F32), rows)
    nq = HEAD_DIM // 4
    inv = ROPE_BASE ** (-jnp.arange(nq, dtype=F32) / nq)
    ang = jnp.stack([row[:, None] * inv, col[:, None] * inv], axis=1)
    return jnp.cos(ang), jnp.sin(ang)


def apply_rope(x, cos, sin):
    xf = x.astype(F32)
    x4 = xf.reshape(*x.shape[:-1], 2, 2, HEAD_DIM // 4)
    x1, x2 = x4[..., 0, :], x4[..., 1, :]
    cs, sn = cos[:, None], sin[:, None]
    out = jnp.stack([x1 * cs - x2 * sn, x2 * cs + x1 * sn], axis=-2)
    return out.reshape(x.shape).astype(x.dtype)


def latent_window_attention(q, k, v, k_ctx, v_ctx, sink):
    B, N = q.shape[:2]
    G = A_Q_HEADS // A_KV_HEADS
    nb = N // A_BLOCK
    scale = HEAD_DIM ** -0.5
    qb = q.reshape(B, nb, A_BLOCK, A_KV_HEADS, G, HEAD_DIM)

    def band(t):
        pad = jnp.zeros((B, A_BLOCK) + t.shape[2:], t.dtype)
        tp = jnp.concatenate([pad, t, pad], axis=1).reshape(B, nb + 2, A_BLOCK, *t.shape[2:])
        return jnp.concatenate([tp[:, :-2], tp[:, 1:-1], tp[:, 2:]], axis=2)

    kb, vb = band(k), band(v)
    s_loc = jnp.einsum('bnqhgd,bnkhd->bhgnqk', qb, kb, preferred_element_type=F32) * scale
    q_pos = (jnp.arange(nb)[:, None] * A_BLOCK + jnp.arange(A_BLOCK)[None, :])[:, :, None]
    k_pos = ((jnp.arange(nb)[:, None] - 1) * A_BLOCK + jnp.arange(3 * A_BLOCK)[None, :])[:, None, :]
    valid = (jnp.abs(q_pos - k_pos) <= WINDOW) & (k_pos >= 0) & (k_pos < N)
    s_loc = jnp.where(valid, s_loc, NEG_INF)
    s_ctx = jnp.einsum('bnqhgd,blhd->bhgnql', qb, k_ctx, preferred_element_type=F32) * scale
    s_sink = jnp.broadcast_to(sink.astype(F32).reshape(1, A_KV_HEADS, G, 1, 1, 1), s_loc.shape[:-1] + (1,))
    p = jax.nn.softmax(jnp.concatenate([s_loc, s_ctx, s_sink], axis=-1), axis=-1).astype(v.dtype)
    nk = 3 * A_BLOCK
    o = (jnp.einsum('bhgnqk,bnkhd->bnqhgd', p[..., :nk], vb)
         + jnp.einsum('bhgnql,blhd->bnqhgd', p[..., nk:-1], v_ctx))
    return o.reshape(B, N, A_WIDTH)


def context_attention(q, k, v, sink):
    B, L = q.shape[:2]
    G = A_Q_HEADS // A_KV_HEADS
    qc = q.reshape(B, L, A_KV_HEADS, G, HEAD_DIM)
    s = jnp.einsum('blhgd,bmhd->bhglm', qc, k, preferred_element_type=F32) * (HEAD_DIM ** -0.5)
    s_sink = jnp.broadcast_to(sink.astype(F32).reshape(1, A_KV_HEADS, G, 1, 1), s.shape[:-1] + (1,))
    p = jax.nn.softmax(jnp.concatenate([s, s_sink], axis=-1), axis=-1).astype(v.dtype)
    o = jnp.einsum('bhglm,bmhd->blhgd', p[..., :-1], v)
    return o.reshape(B, L, A_WIDTH)


def chunk_gmlp(u, v, norm_g, w_s, b_s):
    B, n, _ = u.shape
    nc = n // B_CHUNK
    vh = groupnorm(v.reshape(B, nc, B_CHUNK, B_GROUPS, B_GROUP_DIM), norm_g)
    mixed = jnp.einsum('gpq,bcqgd->bcpgd', w_s, vh) + b_s.T[None, None, :, :, None]
    return u * mixed.reshape(B, n, B_WIDTH)


def retention_scan(k, v, log_gamma, state0, q=None):
    B, n, H, d = k.shape
    nc = n // C_CHUNK
    to_chunks = lambda t: t.astype(F32).reshape(B, nc, C_CHUNK, H, d).transpose(1, 0, 3, 2, 4)
    idx = jnp.arange(C_CHUNK, dtype=F32)
    lg = log_gamma.astype(F32)[:, None]
    diff = idx[:, None] - idx[None, :]
    intra = jnp.where(diff >= 0, jnp.exp(lg[:, :, None] * jnp.maximum(diff, 0.0)), 0.0)
    q_decay = jnp.exp(lg * (idx + 1.0))[:, :, None]
    k_decay = jnp.exp(lg * (C_CHUNK - 1.0 - idx))[:, :, None]
    chunk_decay = jnp.exp(lg[:, 0] * C_CHUNK)[:, None, None]

    def step(S, chunk):
        kc, vc = chunk[0], chunk[1]
        S_new = S * chunk_decay + jnp.einsum('bhjd,bhje->bhde', kc * k_decay, vc)
        if q is None:
            return S_new, None
        qc = chunk[2]
        s = jnp.einsum('bhid,bhjd->bhij', qc, kc) * intra
        out = jnp.einsum('bhij,bhje->bhie', s, vc) + jnp.einsum('bhid,bhde->bhie', qc * q_decay, S)
        return S_new, out

    if q is None:
        S_fin, _ = lax.scan(step, state0, (to_chunks(k), to_chunks(v)))
        return None, S_fin
    S_fin, out = lax.scan(step, state0, (to_chunks(k), to_chunks(v), to_chunks(q)))
    return out.transpose(1, 0, 3, 2, 4).reshape(B, n, H, d), S_fin


def retention_output(o, g, norm_g):
    y = groupnorm(o, norm_g).reshape(g.shape).astype(g.dtype)
    return jax.nn.silu(g) * y


def merge_branches(attn, gmlp, ret, gate_logits, w_branch, w_out):
    br = jnp.stack([attn, gmlp, ret], axis=2)
    proj = jnp.einsum('bnkw,kwd->bnkd', br, w_branch)
    g = jax.nn.sigmoid(gate_logits.reshape(*gate_logits.shape[:2], N_BRANCH, D_MODEL))
    return jnp.sum(g * proj, axis=2) @ w_out


def mixing_sublayer(hc, hx, with_ctx, cos, sin, w_in, a_q_norm, a_k_norm, a_sink, b_norm, b_spatial,
                    b_spatial_bias, c_decay_fwd, c_decay_bwd, c_norm, w_branch, w_out):
    B = hx.shape[0]
    heads = lambda t, h: t.reshape(t.shape[0], t.shape[1], h, HEAD_DIM)
    rope = lambda t: apply_rope(t, cos, sin)
    flip = lambda t: t[:, ::-1]
    lg_f = jax.nn.log_sigmoid(c_decay_fwd.astype(F32))
    lg_b = jax.nn.log_sigmoid(c_decay_bwd.astype(F32))

    (xa_k, xa_v, xr_k, xr_v, xa_q, xr_q, xr_g, xb_u, xb_v, x_gate) = jnp.split(hx @ w_in, IN_SPLITS, axis=-1)
    if with_ctx:
        (ca_k, ca_v, cr_k, cr_v, ca_q, cr_q, cr_g, cb_u, cb_v, c_gate) = jnp.split(hc @ w_in, IN_SPLITS, axis=-1)
    else:
        ca_k, ca_v, cr_k, cr_v = jnp.split(hc @ w_in[:, :CTX_STATE_WIDTH], IN_SPLITS[:3], axis=-1)

    ka_c = rmsnorm(heads(ca_k, A_KV_HEADS), a_k_norm)
    va_c = heads(ca_v, A_KV_HEADS)
    attn_x = latent_window_attention(rope(rmsnorm(heads(xa_q, A_Q_HEADS), a_q_norm)),
                                     rope(rmsnorm(heads(xa_k, A_KV_HEADS), a_k_norm)),
                                     heads(xa_v, A_KV_HEADS), ka_c, va_c, a_sink)
    gmlp_x = chunk_gmlp(jax.nn.gelu(xb_u), jax.nn.gelu(xb_v), b_norm, b_spatial, b_spatial_bias)
    k_scale = HEAD_DIM ** -0.5
    kr_c = heads(cr_k, C_HEADS) * k_scale
    vr_c = heads(cr_v, C_HEADS)
    s0 = jnp.zeros((B, C_HEADS, HEAD_DIM, HEAD_DIM), F32)
    if with_ctx:
        qr_c = heads(cr_q, C_HEADS)
        oc_f, s_f = retention_scan(kr_c, vr_c, lg_f, s0, qr_c)
        oc_b, s_b = retention_scan(flip(kr_c), flip(vr_c), lg_b, s0, flip(qr_c))
    else:
        _, s_f = retention_scan(kr_c, vr_c, lg_f, s0)
        _, s_b = retention_scan(flip(kr_c), flip(vr_c), lg_b, s0)
    qr_x = rope(heads(xr_q, C_HEADS))
    kr_x = rope(heads(xr_k, C_HEADS)) * k_scale
    vr_x = heads(xr_v, C_HEADS)
    ox_f, _ = retention_scan(kr_x, vr_x, lg_f, s_f, qr_x)
    ox_b, _ = retention_scan(flip(kr_x), flip(vr_x), lg_b, s_b, flip(qr_x))
    ret_x = retention_output(ox_f + flip(ox_b), xr_g, c_norm)

    out_x = merge_branches(attn_x, gmlp_x, ret_x, x_gate, w_branch, w_out)
    if not with_ctx:
        return None, out_x
    attn_c = context_attention(rmsnorm(heads(ca_q, A_Q_HEADS), a_q_norm), ka_c, va_c, a_sink)
    gmlp_c = chunk_gmlp(jax.nn.gelu(cb_u), jax.nn.gelu(cb_v), b_norm, b_spatial, b_spatial_bias)
    ret_c = retention_output(oc_f + flip(oc_b), cr_g, c_norm)
    out_c = merge_branches(attn_c, gmlp_c, ret_c, c_gate, w_branch, w_out)
    return out_c, out_x


def moe_ffn(h, w_rg, b_rg, w_re, b_re, w_e1, w_e2):
    T, D = h.shape
    hf = h.astype(F32)
    g_logits = hf @ w_rg.astype(F32) + b_rg.astype(F32)
    g_prob = jax.nn.softmax(g_logits, axis=-1)
    g_sel = jnp.argmax(g_logits, axis=-1).astype(jnp.int32)
    g_w = jnp.take_along_axis(g_prob, g_sel[:, None], axis=-1)
    e_logits = (hf @ w_re.astype(F32) + b_re.astype(F32)).reshape(T, N_GROUPS, EXPERTS_PER_GROUP)
    e_logits = jnp.take_along_axis(e_logits, g_sel[:, None, None], axis=1)[:, 0]
    top_p, top_i = lax.top_k(jax.nn.softmax(e_logits, axis=-1), TOP_K)
    top_p = top_p / jnp.sum(top_p, axis=-1, keepdims=True)
    weights = (g_w * top_p).reshape(-1)
    experts = (g_sel[:, None] * EXPERTS_PER_GROUP + top_i.astype(jnp.int32)).reshape(-1)
    tokens = jnp.repeat(jnp.arange(T, dtype=jnp.int32), TOP_K)

    A = T * TOP_K
    n_blocks = -(-(A + N_EXPERTS * (MOE_BLOCK - 1)) // MOE_BLOCK)
    P = n_blocks * MOE_BLOCK
    counts = jnp.zeros((N_EXPERTS,), jnp.int32).at[experts].add(1)
    padded = (counts + MOE_BLOCK - 1) // MOE_BLOCK * MOE_BLOCK
    pad_end = jnp.cumsum(padded)
    pad_start = pad_end - padded
    start = jnp.cumsum(counts) - counts
    order = jnp.argsort(experts)
    e_sorted = experts[order]
    dest = pad_start[e_sorted] + jnp.arange(A, dtype=jnp.int32) - start[e_sorted]
    slot_tok = jnp.zeros((P,), jnp.int32).at[dest].set(tokens[order])
    slot_w = jnp.zeros((P,), F32).at[dest].set(weights[order]).astype(h.dtype)
    block_e = jnp.minimum(jnp.searchsorted(pad_end, jnp.arange(n_blocks, dtype=jnp.int32) * MOE_BLOCK, side='right'),
                          N_EXPERTS - 1)
    xs = h[slot_tok].reshape(n_blocks, MOE_BLOCK, D)

    def expert_block(args):
        xb, e = args
        gate, up = jnp.split(xb @ w_e1[e], 2, axis=-1)
        return (jax.nn.silu(gate) * up) @ w_e2[e]

    ys = lax.map(expert_block, (xs, block_e)).reshape(P, D)
    return jnp.zeros_like(h).at[slot_tok].add(ys * slot_w[:, None])


def setup_inputs(seed: int = 0) -> dict:
    key = jax.random.key(seed)
    ks = jax.random.split(key, 26)
    nrm = lambda k, shape, s: jax.random.normal(k, shape, F32) * s
    decay_logit = jnp.asarray(np.log(2.0 ** (5.0 + np.arange(C_HEADS)) - 1.0), F32)
    return {
        "x": nrm(ks[0], (BATCH, SEQ, D_MODEL), 1.0),
        "c": nrm(ks[1], (BATCH, D_MODEL), 1.0),
        "ctx": nrm(ks[2], (BATCH, CTX_LEN, D_MODEL), 1.0),
        "c_ctx": nrm(ks[3], (D_MODEL,), 1.0),
        "norm_mix": 1.0 + nrm(ks[4], (DEPTH, D_MODEL), 0.02),
        "norm_ffn": 1.0 + nrm(ks[5], (DEPTH, D_MODEL), 0.02),
        "w_ada": nrm(ks[6], (DEPTH, D_MODEL, 6 * D_MODEL), 0.5 * D_MODEL ** -0.5),
        "b_ada": nrm(ks[7], (DEPTH, 6 * D_MODEL), 0.02),
        "w_in": nrm(ks[8], (DEPTH, D_MODEL, IN_WIDTH), D_MODEL ** -0.5),
        "a_q_norm": 1.0 + nrm(ks[9], (DEPTH, HEAD_DIM), 0.02),
        "a_k_norm": 1.0 + nrm(ks[10], (DEPTH, HEAD_DIM), 0.02),
        "a_sink": nrm(ks[11], (DEPTH, A_Q_HEADS), 1.0),
        "b_norm": 1.0 + nrm(ks[12], (DEPTH, B_GROUPS, B_GROUP_DIM), 0.02),
        "b_spatial": nrm(ks[13], (DEPTH, B_GROUPS, B_CHUNK, B_CHUNK), B_CHUNK ** -0.5),
        "b_spatial_bias": 1.0 + nrm(ks[14], (DEPTH, B_GROUPS, B_CHUNK), 0.02),
        "c_decay_fwd": decay_logit + nrm(ks[15], (DEPTH, C_HEADS), 0.1),
        "c_decay_bwd": decay_logit + nrm(ks[16], (DEPTH, C_HEADS), 0.1),
        "c_norm": 1.0 + nrm(ks[17], (DEPTH, C_HEADS, HEAD_DIM), 0.02),
        "w_branch": nrm(ks[18], (DEPTH, N_BRANCH, BRANCH_WIDTH, D_MODEL), BRANCH_WIDTH ** -0.5),
        "w_out": nrm(ks[19], (DEPTH, D_MODEL, D_MODEL), D_MODEL ** -0.5),
        "w_router_group": nrm(ks[20], (DEPTH, D_MODEL, N_GROUPS), D_MODEL ** -0.5),
        "b_router_group": nrm(ks[21], (DEPTH, N_GROUPS), 0.01),
        "w_router_expert": nrm(ks[22], (DEPTH, D_MODEL, N_EXPERTS), D_MODEL ** -0.5),
        "b_router_expert": nrm(ks[23], (DEPTH, N_EXPERTS), 0.01),
        "w_expert_in": nrm(ks[24], (DEPTH, N_EXPERTS, D_MODEL, 2 * D_EXPERT), D_MODEL ** -0.5),
        "w_expert_out": nrm(ks[25], (DEPTH, N_EXPERTS, D_EXPERT, D_MODEL), D_EXPERT ** -0.5),
    }


def reference(x, c, ctx, c_ctx, norm_mix, norm_ffn, w_ada, b_ada, w_in, a_q_norm, a_k_norm, a_sink,
              b_norm, b_spatial, b_spatial_bias, c_decay_fwd, c_decay_bwd, c_norm, w_branch, w_out,
              w_router_group, b_router_group, w_router_expert, b_router_expert, w_expert_in, w_expert_out):
    B, N, D = x.shape
    L = ctx.shape[1]
    cos, sin = axial_rope_tables(N)
    zx, zc = x, ctx
    for l in range(DEPTH):
        last = l == DEPTH - 1
        mod_x = (jax.nn.silu(c) @ w_ada[l] + b_ada[l]).reshape(B, 6, 1, D)
        mod_c = (jax.nn.silu(c_ctx) @ w_ada[l] + b_ada[l]).reshape(6, D)
        hx = modulate(rmsnorm(zx, norm_mix[l]), mod_x[:, 0], mod_x[:, 1])
        hc = modulate(rmsnorm(zc, norm_mix[l]), mod_c[0], mod_c[1])
        mix_c, mix_x = mixing_sublayer(hc, hx, not last, cos, sin, w_in[l], a_q_norm[l], a_k_norm[l], a_sink[l],
                                       b_norm[l], b_spatial[l], b_spatial_bias[l], c_decay_fwd[l],
                                       c_decay_bwd[l], c_norm[l], w_branch[l], w_out[l])
        zx = zx + mod_x[:, 2] * mix_x
        moe_args = (w_router_group[l], b_router_group[l], w_router_expert[l], b_router_expert[l],
                    w_expert_in[l], w_expert_out[l])
        hx = modulate(rmsnorm(zx, norm_ffn[l]), mod_x[:, 3], mod_x[:, 4])
        if last:
            zx = zx + mod_x[:, 5] * moe_ffn(hx.reshape(B * N, D), *moe_args).reshape(B, N, D)
        else:
            zc = zc + mod_c[2] * mix_c
            hc = modulate(rmsnorm(zc, norm_ffn[l]), mod_c[3], mod_c[4])
            h = jnp.concatenate([hc, hx], axis=1).reshape(B * (L + N), D)
            y = moe_ffn(h, *moe_args).reshape(B, L + N, D)
            zc = zc + mod_c[5] * y[:, :L]
            zx = zx + mod_x[:, 5] * y[:, L:]
    return zx
```

```python
import functools
import math

import jax
import jax.numpy as jnp
from jax import lax
from jax.experimental import pallas as pl
from jax.experimental.pallas import tpu as pltpu

F32 = jnp.float32
MXU_DTYPE = jnp.bfloat16
ACT_DTYPE = jnp.bfloat16

LANES = 128
HEAD_DIM = 128
GRID_W = 64
ROPE_BASE = 10000.0
EPS = 1e-6
NEG_INF = -1e30
A_Q_HEADS = 8
A_KV_HEADS = 2
A_GROUP = A_Q_HEADS // A_KV_HEADS
A_BLOCK = 128
B_GROUPS = 8
C_HEADS = 8
CHUNK = 128
N_GROUPS = 4
EXPERTS_PER_GROUP = 8
N_EXPERTS = N_GROUPS * EXPERTS_PER_GROUP
D_EXPERT = 512
BRANCH_WIDTH = 1024
N_BRANCH = 3
MOE_BLOCK = 128
N_MOD = 6
MOD_ROWS = 8
MIB = 1024 * 1024


def _params(vmem_mib, n_grid):
    return pltpu.CompilerParams(dimension_semantics=("arbitrary",) * n_grid,
                                vmem_limit_bytes=vmem_mib * MIB)


def _silu(x):
    return x * (1.0 / (1.0 + jnp.exp(-x)))


def _sigmoid(x):
    return 1.0 / (1.0 + jnp.exp(-x))


def _gelu_tanh(x):
    return 0.5 * x * (1.0 + jnp.tanh(math.sqrt(2.0 / math.pi) * (x + 0.044715 * (x * x * x))))


def _ada_kernel(c_ref, w_ref, b_ref, o_ref):
    tn = w_ref.shape[1]
    s0 = _silu(c_ref[0])
    s1 = _silu(c_ref[1])
    o_ref[...] = jnp.zeros(o_ref.shape, o_ref.dtype)
    for j in range(tn // LANES):
        sl = slice(j * LANES, (j + 1) * LANES)
        wj = w_ref[:, sl]
        o_ref[0:1, sl] = jnp.sum(wj * s0, axis=0, keepdims=True) + b_ref[:, sl]
        o_ref[1:2, sl] = jnp.sum(wj * s1, axis=0, keepdims=True) + b_ref[:, sl]


def _ada_call(cond_b, w_ada, b_ada):
    depth, k, n = w_ada.shape
    tn = 1024
    return pl.pallas_call(
        _ada_kernel,
        grid=(depth, n // tn),
        in_specs=[pl.BlockSpec((2, k, LANES), lambda l, j: (0, 0, 0)),
                  pl.BlockSpec((None, k, tn), lambda l, j: (l, 0, j)),
                  pl.BlockSpec((None, 1, tn), lambda l, j: (l, 0, j))],
        out_specs=pl.BlockSpec((None, 8, tn), lambda l, j: (l, 0, j)),
        out_shape=jax.ShapeDtypeStruct((depth, 8, n), F32),
        compiler_params=_params(40, 2),
        name="adaln",
    )(cond_b, w_ada, b_ada.reshape(depth, 1, n))


def _norm_mod(z, g, mod, shift_row, scale_row):
    r = lax.rsqrt(jnp.mean(z * z, axis=-1, keepdims=True) + EPS)
    return (z * r * g) * (1.0 + mod[scale_row:scale_row + 1, :]) + mod[shift_row:shift_row + 1, :]


def _norm_mod_kernel(z_ref, g_ref, mod_ref, o_ref):
    o_ref[...] = _norm_mod(z_ref[...], g_ref[...], mod_ref[...], 0, 1).astype(o_ref.dtype)


def _norm_mod_call(z, norm_w, modsel, layer, ctx_rows):
    rows, d = z.shape
    tm = 256
    ctx_tiles = ctx_rows // tm
    return pl.pallas_call(
        _norm_mod_kernel,
        grid=(rows // tm,),
        in_specs=[pl.BlockSpec((tm, d), lambda i: (i, 0)),
                  pl.BlockSpec((None, 1, d), lambda i: (layer, 0, 0)),
                  pl.BlockSpec((None, None, MOD_ROWS, d),
                               lambda i: (layer, jnp.where(i >= ctx_tiles, 1, 0), 0, 0))],
        out_specs=pl.BlockSpec((tm, d), lambda i: (i, 0)),
        out_shape=jax.ShapeDtypeStruct((rows, d), ACT_DTYPE),
        compiler_params=_params(32, 1),
        name="norm_mod",
    )(z, norm_w, modsel)


def _proj_kernel(h_ref, w_ref, o_ref, wbf_ref):
    @pl.when(pl.program_id(1) == 0)
    def _():
        wbf_ref[...] = w_ref[...].astype(wbf_ref.dtype)

    o_ref[...] = jnp.dot(h_ref[...], wbf_ref[...], preferred_element_type=F32).astype(o_ref.dtype)


def _row_tile(rows, pref):
    for t in pref:
        if rows % t == 0:
            return t
    raise ValueError(f"no row tile for {rows}")


def _proj_call(h, w, layer, col_off, ncols):
    rows, k = h.shape
    tn = 512
    tm = _row_tile(rows, (768, 512, 256))
    off = col_off // tn
    assert col_off % tn == 0 and ncols % tn == 0
    return pl.pallas_call(
        _proj_kernel,
        grid=(ncols // tn, rows // tm),
        in_specs=[pl.BlockSpec((tm, k), lambda j, i: (i, 0)),
                  pl.BlockSpec((None, k, tn), lambda j, i: (layer, 0, off + j))],
        out_specs=pl.BlockSpec((tm, tn), lambda j, i: (i, j)),
        out_shape=jax.ShapeDtypeStruct((rows, ncols), ACT_DTYPE),
        scratch_shapes=[pltpu.VMEM((k, tn), MXU_DTYPE)],
        compiler_params=_params(40, 2),
        name="proj_in",
    )(h, w)


def _rope(y, cos, sin, even):
    sw = jnp.where(even, pltpu.roll(y, 96, 1), pltpu.roll(y, 32, 1))
    return y * cos + sw * sin


def _prep_kernel(ak_ref, ck_ref, aq_ref, cq_ref, cos_ref, sin_ref, qn_ref, kn_ref,
                 okn_ref, oqn_ref, okr_ref, oqr_ref):
    cos = cos_ref[...]
    sin = sin_ref[...]
    lane = lax.broadcasted_iota(jnp.int32, cos.shape, 1)
    even = ((lane // 32) % 2) == 0
    scale = HEAD_DIM ** -0.5

    def normed(x, g):
        return x * lax.rsqrt(jnp.mean(x * x, axis=-1, keepdims=True) + EPS) * g

    for h in range(A_KV_HEADS):
        sl = slice(h * HEAD_DIM, (h + 1) * HEAD_DIM)
        y = normed(ak_ref[:, sl].astype(F32), kn_ref[...])
        okn_ref[:, sl] = _rope(y, cos, sin, even).astype(okn_ref.dtype)
    for h in range(A_Q_HEADS):
        sl = slice(h * HEAD_DIM, (h + 1) * HEAD_DIM)
        y = normed(aq_ref[:, sl].astype(F32), qn_ref[...])
        oqn_ref[:, sl] = (_rope(y, cos, sin, even) * scale).astype(oqn_ref.dtype)
    for h in range(C_HEADS):
        sl = slice(h * HEAD_DIM, (h + 1) * HEAD_DIM)
        okr_ref[:, sl] = (_rope(ck_ref[:, sl].astype(F32), cos, sin, even) * scale).astype(okr_ref.dtype)
        oqr_ref[:, sl] = _rope(cq_ref[:, sl].astype(F32), cos, sin, even).astype(oqr_ref.dtype)


def _prep_call(g1, g2, g3, cos_t, sin_t, a_q_norm, a_k_norm, layer):
    rows = g1.shape[0]
    tm = 256
    w = BRANCH_WIDTH
    kvw = A_KV_HEADS * HEAD_DIM
    return pl.pallas_call(
        _prep_kernel,
        grid=(rows // tm,),
        in_specs=[pl.BlockSpec((tm, kvw), lambda i: (i, 0)),
                  pl.BlockSpec((tm, w), lambda i: (i, 0)),
                  pl.BlockSpec((tm, w), lambda i: (i, 0)),
                  pl.BlockSpec((tm, w), lambda i: (i, 1)),
                  pl.BlockSpec((tm, LANES), lambda i: (i, 0)),
                  pl.BlockSpec((tm, LANES), lambda i: (i, 0)),
                  pl.BlockSpec((None, 1, HEAD_DIM), lambda i: (layer, 0, 0)),
                  pl.BlockSpec((None, 1, HEAD_DIM), lambda i: (layer, 0, 0))],
        out_specs=[pl.BlockSpec((tm, kvw), lambda i: (i, 0)),
                   pl.BlockSpec((tm, w), lambda i: (i, 0)),
                   pl.BlockSpec((tm, w), lambda i: (i, 0)),
                   pl.BlockSpec((tm, w), lambda i: (i, 0))],
        out_shape=[jax.ShapeDtypeStruct((rows, kvw), ACT_DTYPE),
                   jax.ShapeDtypeStruct((rows, w), ACT_DTYPE),
                   jax.ShapeDtypeStruct((rows, w), ACT_DTYPE),
                   jax.ShapeDtypeStruct((rows, w), ACT_DTYPE)],
        compiler_params=_params(32, 1),
        name="qk_prep",
    )(g1, g2, g3, g3, cos_t, sin_t, a_q_norm, a_k_norm)


def _attn_kernel(q_ref, kl_ref, km_ref, kr_ref, kc_ref, vl_ref, vm_ref, vr_ref, vc_ref, sink_ref,
                 o_ref, *, ctx_blocks, n_blocks):
    rb = pl.program_id(0)
    blk = A_BLOCK
    q = jnp.concatenate([q_ref[:, g * HEAD_DIM:(g + 1) * HEAD_DIM] for g in range(A_GROUP)], axis=0)
    k = jnp.concatenate([kl_ref[...], km_ref[...], kr_ref[...], kc_ref[...]], axis=0)
    v = jnp.concatenate([vl_ref[...], vm_ref[...], vr_ref[...], vc_ref[...]], axis=0)
    s = lax.dot_general(q, k, (((1,), (1,)), ((), ())), preferred_element_type=F32)
    row = lax.broadcasted_iota(jnp.int32, s.shape, 0) % blk
    col = lax.broadcasted_iota(jnp.int32, s.shape, 1)
    is_lat = rb >= ctx_blocks
    c_lo = jnp.where(rb >= ctx_blocks + 1, 0, blk)
    c_hi = jnp.where(rb <= n_blocks - 2, 3 * blk, 2 * blk)
    c_lo = jnp.where(is_lat, c_lo, 3 * blk)
    c_hi = jnp.where(is_lat, c_hi, 0)
    row_lo = jnp.maximum(col - 2 * blk, 0)
    row_hi = jnp.where(col < blk, col, blk - 1)
    valid = ((col >= c_lo) & (col < c_hi) & (row >= row_lo) & (row <= row_hi)) | (col >= 3 * blk)
    s = jnp.where(valid, s, NEG_INF)
    sink = sink_ref[...]
    m = jnp.maximum(jnp.max(s, axis=-1, keepdims=True), sink)
    p = jnp.exp(s - m)
    denom = jnp.sum(p, axis=-1, keepdims=True) + jnp.exp(sink - m)
    o = jnp.dot(p.astype(v.dtype), v, preferred_element_type=F32) / denom
    for g in range(A_GROUP):
        o_ref[:, g * HEAD_DIM:(g + 1) * HEAD_DIM] = o[g * blk:(g + 1) * blk].astype(o_ref.dtype)


def _attn_call(qn, kn, g1, sink_col, layer, ctx_rows):
    rows = qn.shape[0]
    blk = A_BLOCK
    nb = rows // blk
    cb = ctx_rows // blk
    gw = A_GROUP * HEAD_DIM
    lo = lambda r: jnp.maximum(r - 1, 0)
    hi = lambda r: jnp.minimum(r + 1, nb - 1)
    kspec = lambda f: pl.BlockSpec((blk, HEAD_DIM), lambda r, h: (f(r), h))
    vspec = lambda f: pl.BlockSpec((blk, HEAD_DIM), lambda r, h: (f(r), A_KV_HEADS + h))
    ident = lambda r: r
    return pl.pallas_call(
        functools.partial(_attn_kernel, ctx_blocks=cb, n_blocks=nb),
        grid=(nb, A_KV_HEADS),
        in_specs=[pl.BlockSpec((blk, gw), lambda r, h: (r, h)),
                  kspec(lo), kspec(ident), kspec(hi),
                  pl.BlockSpec((ctx_rows, HEAD_DIM), lambda r, h: (0, h)),
                  vspec(lo), vspec(ident), vspec(hi),
                  pl.BlockSpec((ctx_rows, HEAD_DIM), lambda r, h: (0, A_KV_HEADS + h)),
                  pl.BlockSpec((None, None, A_GROUP * blk, 1), lambda r, h: (layer, h, 0, 0))],
        out_specs=pl.BlockSpec((blk, gw), lambda r, h: (r, h)),
        out_shape=jax.ShapeDtypeStruct((rows, A_Q_HEADS * HEAD_DIM), ACT_DTYPE),
        compiler_params=_params(32, 2),
        name="window_attn",
    )(qn, kn, kn, kn, kn, g1, g1, g1, g1, sink_col)


def _gmlp_kernel(u_ref, v_ref, bn_ref, ws_ref, bias_ref, o_ref):
    for g in range(B_GROUPS):
        sl = slice(g * LANES, (g + 1) * LANES)
        v = _gelu_tanh(v_ref[:, sl].astype(F32))
        vc = v - jnp.mean(v, axis=-1, keepdims=True)
        vh = vc * lax.rsqrt(jnp.mean(vc * vc, axis=-1, keepdims=True) + EPS) * bn_ref[:, sl]
        mixed = jnp.dot(ws_ref[g], vh.astype(ws_ref.dtype), preferred_element_type=F32) + bias_ref[g]
        o_ref[:, sl] = (_gelu_tanh(u_ref[:, sl].astype(F32)) * mixed).astype(o_ref.dtype)


def _gmlp_call(g3, b_norm_flat, ws, bias_b, layer):
    rows = g3.shape[0]
    w = BRANCH_WIDTH
    return pl.pallas_call(
        _gmlp_kernel,
        grid=(rows // CHUNK,),
        in_specs=[pl.BlockSpec((CHUNK, w), lambda i: (i, 3)),
                  pl.BlockSpec((CHUNK, w), lambda i: (i, 4)),
                  pl.BlockSpec((None, 1, w), lambda i: (layer, 0, 0)),
                  pl.BlockSpec((None, B_GROUPS, CHUNK, CHUNK), lambda i: (layer, 0, 0, 0)),
                  pl.BlockSpec((None, B_GROUPS, CHUNK, LANES), lambda i: (layer, 0, 0, 0))],
        out_specs=pl.BlockSpec((CHUNK, w), lambda i: (i, 0)),
        out_shape=jax.ShapeDtypeStruct((rows, w), ACT_DTYPE),
        compiler_params=_params(32, 1),
        name="chunk_gmlp",
    )(g3, g3, b_norm_flat, ws, bias_b)


def _ret_kernel(qf_ref, kf_ref, vf_ref, qb_ref, kb_ref, vb_ref, intra_ref, qdec_ref, kdec_ref, cdec_ref,
                of_ref, ob_ref, s_ref):
    @pl.when(pl.program_id(0) == 0)
    def _():
        s_ref[...] = jnp.zeros(s_ref.shape, s_ref.dtype)

    def one_dir(d, q_ref, k_ref, v_ref, o_ref):
        for h in range(C_HEADS):
            sl = slice(h * HEAD_DIM, (h + 1) * HEAD_DIM)
            q = q_ref[:, sl]
            k = k_ref[:, sl]
            v = v_ref[:, sl]
            a = lax.dot_general(q, k, (((1,), (1,)), ((), ())), preferred_element_type=F32) * intra_ref[d, h]
            state = s_ref[d, h]
            qd = (q.astype(F32) * qdec_ref[d, h]).astype(q.dtype)
            lhs = jnp.concatenate([a.astype(q.dtype), qd], axis=1)
            rhs = jnp.concatenate([v, state.astype(v.dtype)], axis=0)
            o_ref[:, sl] = jnp.dot(lhs, rhs, preferred_element_type=F32).astype(o_ref.dtype)
            kd = (k.astype(F32) * kdec_ref[d, h]).astype(k.dtype)
            upd = lax.dot_general(kd, v, (((0,), (0,)), ((), ())), preferred_element_type=F32)
            s_ref[d, h] = state * cdec_ref[d, h] + upd

    one_dir(0, qf_ref, kf_ref, vf_ref, of_ref)
    one_dir(1, qb_ref, kb_ref, vb_ref, ob_ref)


def _ret_call(qr, kr, g2, tables, layer, ctx_rows):
    rows = qr.shape[0]
    w = BRANCH_WIDTH
    nc = rows // CHUNK
    cc = ctx_rows // CHUNK

    def bwd(s):
        return jnp.where(s < cc, cc - 1 - s, nc - 1 + cc - s)

    fq = pl.BlockSpec((CHUNK, w), lambda s: (s, 0))
    fv = pl.BlockSpec((CHUNK, w), lambda s: (s, 1))
    bq = pl.BlockSpec((CHUNK, w), lambda s: (bwd(s), 0))
    bv = pl.BlockSpec((CHUNK, w), lambda s: (bwd(s), 1))
    tab = pl.BlockSpec((None, 2, C_HEADS, CHUNK, LANES), lambda s: (layer, 0, 0, 0, 0))
    return pl.pallas_call(
        _ret_kernel,
        grid=(nc,),
        in_specs=[fq, fq, fv, bq, bq, bv, tab, tab, tab, tab],
        out_specs=[pl.BlockSpec((CHUNK, w), lambda s: (s, 0)),
                   pl.BlockSpec((CHUNK, w), lambda s: (bwd(s), 0))],
        out_shape=[jax.ShapeDtypeStruct((rows, w), F32), jax.ShapeDtypeStruct((rows, w), F32)],
        scratch_shapes=[pltpu.VMEM((2, C_HEADS, HEAD_DIM, HEAD_DIM), F32)],
        compiler_params=_params(32, 1),
        name="retention",
    )(qr, kr, g2, qr, kr, g2, *tables)


def _merge_kernel(attn_ref, gm_ref, of_ref, ob_ref, rg_ref, gate_ref, cn_ref, wb_ref, o_ref):
    d = o_ref.shape[1]
    ret_parts = []
    for h in range(C_HEADS):
        sl = slice(h * HEAD_DIM, (h + 1) * HEAD_DIM)
        o = of_ref[:, sl] + ob_ref[:, sl]
        oc = o - jnp.mean(o, axis=-1, keepdims=True)
        y = oc * lax.rsqrt(jnp.mean(oc * oc, axis=-1, keepdims=True) + EPS) * cn_ref[:, sl]
        ret_parts.append((_silu(rg_ref[:, sl].astype(F32)) * y).astype(wb_ref.dtype))
    ret = jnp.concatenate(ret_parts, axis=1)
    branches = (attn_ref[...].astype(wb_ref.dtype), gm_ref[...].astype(wb_ref.dtype), ret)
    acc = None
    for b in range(N_BRANCH):
        proj = jnp.dot(branches[b], wb_ref[b], preferred_element_type=F32)
        term = _sigmoid(gate_ref[:, b * d:(b + 1) * d].astype(F32)) * proj
        acc = term if acc is None else acc + term
    o_ref[...] = acc.astype(o_ref.dtype)


def _merge_call(attn, gm, o_f, o_b, g3, g4, c_norm_flat, wb, layer):
    rows = attn.shape[0]
    w = BRANCH_WIDTH
    d = wb.shape[-1]
    tm = 256
    row = lambda c: pl.BlockSpec((tm, w), lambda i: (i, c))
    return pl.pallas_call(
        _merge_kernel,
        grid=(rows // tm,),
        in_specs=[row(0), row(0), row(0), row(0), row(2),
                  pl.BlockSpec((tm, N_BRANCH * d), lambda i: (i, 0)),
                  pl.BlockSpec((None, 1, w), lambda i: (layer, 0, 0)),
                  pl.BlockSpec((None, N_BRANCH, w, d), lambda i: (layer, 0, 0, 0))],
        out_specs=pl.BlockSpec((tm, d), lambda i: (i, 0)),
        out_shape=jax.ShapeDtypeStruct((rows, d), ACT_DTYPE),
        compiler_params=_params(52, 1),
        name="branch_merge",
    )(attn, gm, o_f, o_b, g3, g4, c_norm_flat, wb)


def _outproj_kernel(m_ref, w_ref, z_ref, mod_ref, o_ref):
    y = jnp.dot(m_ref[...], w_ref[...], preferred_element_type=F32)
    o_ref[...] = z_ref[...] + mod_ref[2:3, :] * y


def _outproj_call(merged, wo, z, modsel, layer, ctx_rows):
    rows, d = z.shape
    tm = 256
    ctx_tiles = ctx_rows // tm
    return pl.pallas_call(
        _outproj_kernel,
        grid=(rows // tm,),
        in_specs=[pl.BlockSpec((tm, d), lambda i: (i, 0)),
                  pl.BlockSpec((None, d, d), lambda i: (layer, 0, 0)),
                  pl.BlockSpec((tm, d), lambda i: (i, 0)),
                  pl.BlockSpec((None, None, MOD_ROWS, d),
                               lambda i: (layer, jnp.where(i >= ctx_tiles, 1, 0), 0, 0))],
        out_specs=pl.BlockSpec((tm, d), lambda i: (i, 0)),
        out_shape=jax.ShapeDtypeStruct((rows, d), F32),
        compiler_params=_params(40, 1),
        name="out_proj",
    )(merged, wo, z, modsel)


def _route_kernel(z_ref, g_ref, mod_ref, whi_ref, wlo_ref, rb_ref, h_ref, info_ref, cnt_ref, carry_ref):
    i = pl.program_id(0)

    @pl.when(i == 0)
    def _():
        carry_ref[...] = jnp.zeros(carry_ref.shape, carry_ref.dtype)

    h = _norm_mod(z_ref[...], g_ref[...], mod_ref[...], 3, 4)
    h_ref[...] = h
    h_hi = h.astype(whi_ref.dtype)
    h_lo = (h - h_hi.astype(F32)).astype(whi_ref.dtype)
    logits = (jnp.dot(h_hi, whi_ref[...], preferred_element_type=F32)
              + jnp.dot(h_hi, wlo_ref[...], preferred_element_type=F32)
              + jnp.dot(h_lo, whi_ref[...], preferred_element_type=F32)) + rb_ref[...]
    tm = logits.shape[0]
    lane = lax.broadcasted_iota(jnp.int32, logits.shape, 1).astype(F32)
    first = lambda hit: jnp.min(jnp.where(hit, lane, 4.0 * LANES), axis=-1, keepdims=True)

    is_g = lane < N_GROUPS
    gl = jnp.where(is_g, logits, NEG_INF)
    gmax = jnp.max(gl, axis=-1, keepdims=True)
    g_sel = first(gl == gmax)
    g_w = 1.0 / jnp.sum(jnp.where(is_g, jnp.exp(gl - gmax), 0.0), axis=-1, keepdims=True)

    e_id = lane - N_GROUPS
    in_group = (e_id >= g_sel * EXPERTS_PER_GROUP) & (e_id < (g_sel + 1.0) * EXPERTS_PER_GROUP)
    el = jnp.where(in_group, logits, NEG_INF)
    m1 = jnp.max(el, axis=-1, keepdims=True)
    i1 = first(el == m1)
    el2 = jnp.where(lane == i1, NEG_INF, el)
    m2 = jnp.max(el2, axis=-1, keepdims=True)
    i2 = first(el2 == m2)
    r = jnp.exp(m2 - m1)
    w1 = g_w / (1.0 + r)
    w2 = g_w * r / (1.0 + r)
    e1 = i1 - N_GROUPS
    e2 = i2 - N_GROUPS

    hot1 = lane == e1
    hot2 = lane == e2
    hot = jnp.where(hot1 | hot2, 1.0, 0.0)
    rr = lax.broadcasted_iota(jnp.int32, (tm, tm), 0)
    cc = lax.broadcasted_iota(jnp.int32, (tm, tm), 1)
    tri = jnp.where(cc < rr, 1.0, 0.0).astype(MXU_DTYPE)
    before = jnp.dot(tri, hot.astype(MXU_DTYPE), preferred_element_type=F32) + carry_ref[0:1, :]
    rank1 = jnp.sum(jnp.where(hot1, before, 0.0), axis=-1, keepdims=True)
    rank2 = jnp.sum(jnp.where(hot2, before, 0.0), axis=-1, keepdims=True)
    carry_ref[0:1, :] = carry_ref[0:1, :] + jnp.sum(hot, axis=0, keepdims=True)
    cnt_ref[...] = carry_ref[...]

    info = jnp.where(lane == 0, e1, 0.0)
    info = jnp.where(lane == 1, e2, info)
    info = jnp.where(lane == 2, w1, info)
    info = jnp.where(lane == 3, w2, info)
    info = jnp.where(lane == 4, rank1, info)
    info = jnp.where(lane == 5, rank2, info)
    info_ref[...] = info


def _route_call(z, norm_w, modsel, w_hi, w_lo, rbias, layer, ctx_rows):
    rows, d = z.shape
    tm = 256
    ctx_tiles = ctx_rows // tm
    return pl.pallas_call(
        _route_kernel,
        grid=(rows // tm,),
        in_specs=[pl.BlockSpec((tm, d), lambda i: (i, 0)),
                  pl.BlockSpec((None, 1, d), lambda i: (layer, 0, 0)),
                  pl.BlockSpec((None, None, MOD_ROWS, d),
                               lambda i: (layer, jnp.where(i >= ctx_tiles, 1, 0), 0, 0)),
                  pl.BlockSpec((None, d, LANES), lambda i: (layer, 0, 0)),
                  pl.BlockSpec((None, d, LANES), lambda i: (layer, 0, 0)),
                  pl.BlockSpec((None, 1, LANES), lambda i: (layer, 0, 0))],
        out_specs=[pl.BlockSpec((tm, d), lambda i: (i, 0)),
                   pl.BlockSpec((tm, LANES), lambda i: (i, 0)),
                   pl.BlockSpec((8, LANES), lambda i: (0, 0))],
        out_shape=[jax.ShapeDtypeStruct((rows, d), F32),
                   jax.ShapeDtypeStruct((rows, LANES), F32),
                   jax.ShapeDtypeStruct((8, LANES), F32)],
        scratch_shapes=[pltpu.VMEM((8, LANES), F32)],
        compiler_params=_params(32, 1),
        name="route",
    )(z, norm_w, modsel, w_hi, w_lo, rbias)


def _dispatch_kernel(dest_ref, h_ref, xs_in_ref, xs_ref, sem):
    del xs_in_ref
    tm = h_ref.shape[0]

    def row_copy(t, slot):
        return pltpu.make_async_copy(h_ref.at[pl.ds(t, 1)], xs_ref.at[pl.ds(slot, 1)], sem)

    def start(t, carry):
        row_copy(t, dest_ref[0, 0, 2 * t]).start()
        row_copy(t, dest_ref[0, 0, 2 * t + 1]).start()
        return carry

    def wait(t, carry):
        row_copy(t, dest_ref[0, 0, 2 * t]).wait()
        row_copy(t, dest_ref[0, 0, 2 * t + 1]).wait()
        return carry

    lax.fori_loop(0, tm, start, 0)
    lax.fori_loop(0, tm, wait, 0)


def _dispatch_call(dest3, h, xs_init):
    rows, d = h.shape
    tm = dest3.shape[2] // 2
    return pl.pallas_call(
        _dispatch_kernel,
        grid=(rows // tm,),
        in_specs=[pl.BlockSpec((1, 1, 2 * tm), lambda i: (i, 0, 0), memory_space=pltpu.SMEM),
                  pl.BlockSpec((tm, d), lambda i: (i, 0)),
                  pl.BlockSpec(memory_space=pl.ANY)],
        out_specs=pl.BlockSpec(memory_space=pl.ANY),
        out_shape=jax.ShapeDtypeStruct(xs_init.shape, xs_init.dtype),
        scratch_shapes=[pltpu.SemaphoreType.DMA(())],
        input_output_aliases={2: 0},
        compiler_params=_params(32, 1),
        name="moe_dispatch",
    )(dest3, h, xs_init)


def _expert_kernel(be_ref, nu_ref, xs_ref, w1_ref, w2_ref, ys_ref, w1bf_ref, w2bf_ref):
    i = pl.program_id(0)
    changed = jnp.logical_or(i == 0, be_ref[i] != be_ref[jnp.maximum(i - 1, 0)])
    used = i < nu_ref[0]

    @pl.when(jnp.logical_and(changed, used))
    def _():
        w1bf_ref[...] = w1_ref[...].astype(w1bf_ref.dtype)
        w2bf_ref[...] = w2_ref[...].astype(w2bf_ref.dtype)

    @pl.when(used)
    def _():
        de = w2bf_ref.shape[0]
        hcat = jnp.dot(xs_ref[...].astype(w1bf_ref.dtype), w1bf_ref[...], preferred_element_type=F32)
        act = _silu(hcat[:, :de]) * hcat[:, de:]
        ys_ref[...] = jnp.dot(act.astype(w2bf_ref.dtype), w2bf_ref[...], preferred_element_type=F32)

    @pl.when(jnp.logical_not(used))
    def _():
        ys_ref[...] = jnp.zeros(ys_ref.shape, ys_ref.dtype)


def _expert_call(block_e, n_used, xs, w_e1, w_e2, layer):
    slots, d = xs.shape
    de = w_e2.shape[2]
    nb = slots // MOE_BLOCK
    grid_spec = pltpu.PrefetchScalarGridSpec(
        num_scalar_prefetch=2,
        grid=(nb,),
        in_specs=[pl.BlockSpec((MOE_BLOCK, d), lambda i, be, nu: (i, 0)),
                  pl.BlockSpec((None, None, d, 2 * de), lambda i, be, nu: (layer, be[i], 0, 0)),
                  pl.BlockSpec((None, None, de, d), lambda i, be, nu: (layer, be[i], 0, 0))],
        out_specs=pl.BlockSpec((MOE_BLOCK, d), lambda i, be, nu: (i, 0)),
        scratch_shapes=[pltpu.VMEM((d, 2 * de), MXU_DTYPE), pltpu.VMEM((de, d), MXU_DTYPE)])
    return pl.pallas_call(
        _expert_kernel,
        grid_spec=grid_spec,
        out_shape=jax.ShapeDtypeStruct((slots, d), F32),
        compiler_params=_params(48, 1),
        name="moe_experts",
    )(block_e, n_used, xs, w_e1, w_e2)


def _combine_kernel(dest_ref, ys_ref, info_ref, z_ref, mod_ref, o_ref, buf_ref, sem):
    tm = z_ref.shape[0]

    def row_copy(t, k, slot):
        return pltpu.make_async_copy(ys_ref.at[pl.ds(slot, 1)], buf_ref.at[k, pl.ds(t, 1)], sem)

    def start(t, carry):
        row_copy(t, 0, dest_ref[0, 0, 2 * t]).start()
        row_copy(t, 1, dest_ref[0, 0, 2 * t + 1]).start()
        return carry

    def wait(t, carry):
        row_copy(t, 0, dest_ref[0, 0, 2 * t]).wait()
        row_copy(t, 1, dest_ref[0, 0, 2 * t + 1]).wait()
        return carry

    lax.fori_loop(0, tm, start, 0)
    lax.fori_loop(0, tm, wait, 0)
    info = info_ref[...]
    lane = lax.broadcasted_iota(jnp.int32, info.shape, 1)
    w1 = jnp.sum(jnp.where(lane == 2, info, 0.0), axis=-1, keepdims=True)
    w2 = jnp.sum(jnp.where(lane == 3, info, 0.0), axis=-1, keepdims=True)
    y = buf_ref[0] * w1 + buf_ref[1] * w2
    o_ref[...] = z_ref[...] + mod_ref[5:6, :] * y


def _combine_call(dest3, ys, info, z, modsel, layer, ctx_rows):
    rows, d = z.shape
    tm = dest3.shape[2] // 2
    ctx_tiles = ctx_rows // tm
    return pl.pallas_call(
        _combine_kernel,
        grid=(rows // tm,),
        in_specs=[pl.BlockSpec((1, 1, 2 * tm), lambda i: (i, 0, 0), memory_space=pltpu.SMEM),
                  pl.BlockSpec(memory_space=pl.ANY),
                  pl.BlockSpec((tm, LANES), lambda i: (i, 0)),
                  pl.BlockSpec((tm, d), lambda i: (i, 0)),
                  pl.BlockSpec((None, None, MOD_ROWS, d),
                               lambda i: (layer, jnp.where(i >= ctx_tiles, 1, 0), 0, 0))],
        out_specs=pl.BlockSpec((tm, d), lambda i: (i, 0)),
        out_shape=jax.ShapeDtypeStruct((rows, d), F32),
        scratch_shapes=[pltpu.VMEM((2, tm, d), F32), pltpu.SemaphoreType.DMA(())],
        compiler_params=_params(32, 1),
        name="moe_combine",
    )(dest3, ys, info, z, modsel)


def _rope_tables(n, ctx_rows):
    rows = n // GRID_W
    row = jnp.repeat(jnp.arange(rows, dtype=F32), GRID_W)
    col = jnp.tile(jnp.arange(GRID_W, dtype=F32), rows)
    nq = HEAD_DIM // 4
    inv = ROPE_BASE ** (-jnp.arange(nq, dtype=F32) / nq)
    ar, ac = row[:, None] * inv, col[:, None] * inv
    cos = jnp.concatenate([jnp.cos(ar), jnp.cos(ar), jnp.cos(ac), jnp.cos(ac)], axis=1)
    sin = jnp.concatenate([-jnp.sin(ar), jnp.sin(ar), -jnp.sin(ac), jnp.sin(ac)], axis=1)
    cos = jnp.concatenate([jnp.ones((ctx_rows, HEAD_DIM), F32), cos], axis=0)
    sin = jnp.concatenate([jnp.zeros((ctx_rows, HEAD_DIM), F32), sin], axis=0)
    return cos, sin


def _retention_tables(c_decay_fwd, c_decay_bwd):
    lg_f = jax.nn.log_sigmoid(c_decay_fwd.astype(F32))[:, :, None, None]
    lg_b = jax.nn.log_sigmoid(c_decay_bwd.astype(F32))[:, :, None, None]
    idx = jnp.arange(CHUNK, dtype=F32)
    diff = idx[:, None] - idx[None, :]
    ones = jnp.ones((CHUNK, CHUNK), F32)
    t_col = idx[:, None] * ones
    intra_f = jnp.where(diff >= 0, jnp.exp(lg_f * jnp.maximum(diff, 0.0)), 0.0)
    intra_b = jnp.where(diff <= 0, jnp.exp(lg_b * jnp.maximum(-diff, 0.0)), 0.0)
    qdec_f = jnp.exp(lg_f * (t_col + 1.0))
    qdec_b = jnp.exp(lg_b * (CHUNK - t_col))
    kdec_f = jnp.exp(lg_f * (CHUNK - 1.0 - t_col))
    kdec_b = jnp.exp(lg_b * t_col)
    cdec_f = jnp.exp(lg_f * CHUNK) * ones
    cdec_b = jnp.exp(lg_b * CHUNK) * ones
    pair = lambda a, b: jnp.stack([a, b], axis=1)
    return (pair(intra_f, intra_b), pair(qdec_f, qdec_b), pair(kdec_f, kdec_b), pair(cdec_f, cdec_b))


def _moe_plan(info, counts_row, n_slots_blocks):
    e = info[:, 0:2].astype(jnp.int32)
    rank = info[:, 4:6].astype(jnp.int32)
    counts = counts_row[:N_EXPERTS].astype(jnp.int32)
    padded = (counts + MOE_BLOCK - 1) // MOE_BLOCK * MOE_BLOCK
    pad_end = jnp.cumsum(padded)
    pad_start = pad_end - padded
    dest = pad_start[e] + rank
    n_used = pad_end[-1] // MOE_BLOCK
    blocks = jnp.arange(n_slots_blocks, dtype=jnp.int32)
    block_e = jnp.minimum(jnp.searchsorted(pad_end, blocks * MOE_BLOCK, side="right"), N_EXPERTS - 1)
    last_e = block_e[jnp.maximum(n_used - 1, 0)]
    block_e = jnp.where(blocks < n_used, block_e, last_e).astype(jnp.int32)
    return dest, block_e, n_used.reshape(1).astype(jnp.int32)


def kernel(x, c, ctx, c_ctx, norm_mix, norm_ffn, w_ada, b_ada, w_in, a_q_norm, a_k_norm, a_sink,
           b_norm, b_spatial, b_spatial_bias, c_decay_fwd, c_decay_bwd, c_norm, w_branch, w_out,
           w_router_group, b_router_group, w_router_expert, b_router_expert, w_expert_in, w_expert_out):
    batch, n, d = x.shape
    ctx_rows = ctx.shape[1]
    depth = w_in.shape[0]
    assert batch == 1 and ctx_rows % 256 == 0 and n % 256 == 0
    rows = ctx_rows + n

    cond = jnp.stack([c[0], c_ctx], axis=0)
    cond_b = jnp.broadcast_to(cond[:, :, None], (2, d, LANES))
    mod = _ada_call(cond_b, w_ada, b_ada)[:, :2].reshape(depth, 2, N_MOD, d)
    modsel = jnp.pad(mod[:, ::-1], ((0, 0), (0, 0), (0, MOD_ROWS - N_MOD), (0, 0)))

    cos_t, sin_t = _rope_tables(n, ctx_rows)
    ret_tables = _retention_tables(c_decay_fwd, c_decay_bwd)
    sink_col = jnp.broadcast_to(a_sink.astype(F32).reshape(depth, A_KV_HEADS, A_GROUP, 1, 1),
                                (depth, A_KV_HEADS, A_GROUP, A_BLOCK, 1)).reshape(depth, A_KV_HEADS, A_GROUP * A_BLOCK, 1)
    ws = b_spatial.astype(MXU_DTYPE)
    bias_b = jnp.broadcast_to(b_spatial_bias.astype(F32)[:, :, :, None], (depth, B_GROUPS, CHUNK, LANES))
    wb = w_branch.astype(MXU_DTYPE)
    wo = w_out.astype(MXU_DTYPE)
    w_r = jnp.concatenate([w_router_group, w_router_expert], axis=-1).astype(F32)
    w_r = jnp.pad(w_r, ((0, 0), (0, 0), (0, LANES - w_r.shape[-1])))
    w_r_hi = w_r.astype(MXU_DTYPE)
    w_r_lo = (w_r - w_r_hi.astype(F32)).astype(MXU_DTYPE)
    b_r = jnp.concatenate([b_router_group, b_router_expert], axis=-1).astype(F32)
    b_r = jnp.pad(b_r, ((0, 0), (0, LANES - b_r.shape[-1]))).reshape(depth, 1, LANES)
    norm_mix3 = norm_mix.reshape(depth, 1, d)
    norm_ffn3 = norm_ffn.reshape(depth, 1, d)
    a_q_norm3 = a_q_norm.reshape(depth, 1, HEAD_DIM)
    a_k_norm3 = a_k_norm.reshape(depth, 1, HEAD_DIM)
    b_norm3 = b_norm.reshape(depth, 1, BRANCH_WIDTH)
    c_norm3 = c_norm.reshape(depth, 1, BRANCH_WIDTH)

    kvw = A_KV_HEADS * HEAD_DIM
    w = BRANCH_WIDTH
    n_assign = rows * 2
    n_slot_blocks = -(-(n_assign + N_EXPERTS * (MOE_BLOCK - 1)) // MOE_BLOCK)
    tok_tile = 128

    z = jnp.concatenate([ctx[0], x[0]], axis=0)
    for l in range(depth):
        h = _norm_mod_call(z, norm_mix3, modsel, l, ctx_rows)
        g1 = _proj_call(h, w_in, l, 0, 2 * kvw)
        g2 = _proj_call(h, w_in, l, 2 * kvw, 2 * w)
        g3 = _proj_call(h, w_in, l, 2 * kvw + 2 * w, 5 * w)
        g4 = _proj_call(h, w_in, l, 2 * kvw + 7 * w, N_BRANCH * d)
        kn, qn, kr, qr = _prep_call(g1, g2, g3, cos_t, sin_t, a_q_norm3, a_k_norm3, l)
        attn = _attn_call(qn, kn, g1, sink_col, l, ctx_rows)
        gm = _gmlp_call(g3, b_norm3, ws, bias_b, l)
        o_f, o_b = _ret_call(qr, kr, g2, ret_tables, l, ctx_rows)
        merged = _merge_call(attn, gm, o_f, o_b, g3, g4, c_norm3, wb, l)
        z = _outproj_call(merged, wo, z, modsel, l, ctx_rows)

        h2, info, counts = _route_call(z, norm_ffn3, modsel, w_r_hi, w_r_lo, b_r, l, ctx_rows)
        dest, block_e, n_used = _moe_plan(info, counts[0], n_slot_blocks)
        dest3 = dest.reshape(rows // tok_tile, 1, 2 * tok_tile)
        xs = _dispatch_call(dest3, h2, jnp.zeros((n_slot_blocks * MOE_BLOCK, d), F32))
        ys = _expert_call(block_e, n_used, xs, w_expert_in, w_expert_out, l)
        z = _combine_call(dest3, ys, info, z, modsel, l, ctx_rows)
    return z[ctx_rows:][None]
```

```python
import functools
import math

import jax
import jax.numpy as jnp
from jax import lax
from jax.experimental import pallas as pl
from jax.experimental.pallas import tpu as pltpu

F32 = jnp.float32
MXU_DTYPE = jnp.bfloat16
ACT_DTYPE = jnp.bfloat16

LANES = 128
HEAD_DIM = 128
GRID_W = 64
ROPE_BASE = 10000.0
EPS = 1e-6
NEG_INF = -1e30
A_Q_HEADS = 8
A_KV_HEADS = 2
A_GROUP = A_Q_HEADS // A_KV_HEADS
A_BLOCK = 128
B_GROUPS = 8
C_HEADS = 8
CHUNK = 128
N_GROUPS = 4
EXPERTS_PER_GROUP = 8
N_EXPERTS = N_GROUPS * EXPERTS_PER_GROUP
D_EXPERT = 512
BRANCH_WIDTH = 1024
N_BRANCH = 3
MOE_BLOCK = 256
N_MOD = 6
MOD_ROWS = 8
MIB = 1024 * 1024


def _params(vmem_mib, n_grid, **kw):
    return pltpu.CompilerParams(dimension_semantics=("arbitrary",) * n_grid,
                                vmem_limit_bytes=vmem_mib * MIB, **kw)


def _silu(x):
    return x * (1.0 / (1.0 + jnp.exp(-x)))


def _sigmoid(x):
    return 1.0 / (1.0 + jnp.exp(-x))


def _gelu_tanh(x):
    return 0.5 * x * (1.0 + jnp.tanh(math.sqrt(2.0 / math.pi) * (x + 0.044715 * (x * x * x))))


def _ada_kernel(c_ref, w_ref, b_ref, o_ref):
    tn = w_ref.shape[1]
    s0 = _silu(c_ref[0])
    s1 = _silu(c_ref[1])
    o_ref[...] = jnp.zeros(o_ref.shape, o_ref.dtype)
    for j in range(tn // LANES):
        sl = slice(j * LANES, (j + 1) * LANES)
        wj = w_ref[:, sl]
        o_ref[0:1, sl] = jnp.sum(wj * s0, axis=0, keepdims=True) + b_ref[:, sl]
        o_ref[1:2, sl] = jnp.sum(wj * s1, axis=0, keepdims=True) + b_ref[:, sl]


def _ada_call(cond_b, w_ada, b_ada):
    depth, k, n = w_ada.shape
    tn = 1024
    return pl.pallas_call(
        _ada_kernel,
        grid=(depth, n // tn),
        in_specs=[pl.BlockSpec((2, k, LANES), lambda l, j: (0, 0, 0)),
                  pl.BlockSpec((None, k, tn), lambda l, j: (l, 0, j)),
                  pl.BlockSpec((None, 1, tn), lambda l, j: (l, 0, j))],
        out_specs=pl.BlockSpec((None, 8, tn), lambda l, j: (l, 0, j)),
        out_shape=jax.ShapeDtypeStruct((depth, 8, n), F32),
        compiler_params=_params(40, 2),
        name="adaln",
    )(cond_b, w_ada, b_ada.reshape(depth, 1, n))


def _norm_mod(z, g, mod, shift_row, scale_row):
    r = lax.rsqrt(jnp.mean(z * z, axis=-1, keepdims=True) + EPS)
    return (z * r * g) * (1.0 + mod[scale_row:scale_row + 1, :]) + mod[shift_row:shift_row + 1, :]


def _norm_mod_kernel(z_ref, g_ref, mod_ref, o_ref):
    o_ref[...] = _norm_mod(z_ref[...], g_ref[...], mod_ref[...], 0, 1).astype(o_ref.dtype)


def _norm_mod_call(z, norm_w, modsel, layer, ctx_rows):
    rows, d = z.shape
    tm = 256
    ctx_tiles = ctx_rows // tm
    return pl.pallas_call(
        _norm_mod_kernel,
        grid=(rows // tm,),
        in_specs=[pl.BlockSpec((tm, d), lambda i: (i, 0)),
                  pl.BlockSpec((None, 1, d), lambda i: (layer, 0, 0)),
                  pl.BlockSpec((None, None, MOD_ROWS, d),
                               lambda i: (layer, jnp.where(i >= ctx_tiles, 1, 0), 0, 0))],
        out_specs=pl.BlockSpec((tm, d), lambda i: (i, 0)),
        out_shape=jax.ShapeDtypeStruct((rows, d), ACT_DTYPE),
        compiler_params=_params(32, 1),
        name="norm_mod",
    )(z, norm_w, modsel)


def _proj_kernel(h_ref, w_ref, o_ref, wbf_ref):
    @pl.when(pl.program_id(1) == 0)
    def _():
        wbf_ref[...] = w_ref[...].astype(wbf_ref.dtype)

    o_ref[...] = jnp.dot(h_ref[...], wbf_ref[...], preferred_element_type=F32).astype(o_ref.dtype)


def _row_tile(rows, pref):
    for t in pref:
        if rows % t == 0:
            return t
    raise ValueError(f"no row tile for {rows}")


def _proj_call(h, w, layer, col_off, ncols):
    rows, k = h.shape
    tn = next(t for t in (1536, 1280, 1024, 512) if col_off % t == 0 and ncols % t == 0)
    tm = _row_tile(rows, (768, 512, 256))
    off = col_off // tn
    return pl.pallas_call(
        _proj_kernel,
        grid=(ncols // tn, rows // tm),
        in_specs=[pl.BlockSpec((tm, k), lambda j, i: (i, 0)),
                  pl.BlockSpec((None, k, tn), lambda j, i: (layer, 0, off + j))],
        out_specs=pl.BlockSpec((tm, tn), lambda j, i: (i, j)),
        out_shape=jax.ShapeDtypeStruct((rows, ncols), ACT_DTYPE),
        scratch_shapes=[pltpu.VMEM((k, tn), MXU_DTYPE)],
        compiler_params=_params(52, 2),
        name="proj_in",
    )(h, w)


def _rope(y, cos, sin, even):
    sw = jnp.where(even, pltpu.roll(y, 96, 1), pltpu.roll(y, 32, 1))
    return y * cos + sw * sin


def _prep_kernel(ak_ref, ck_ref, aq_ref, cq_ref, cos_ref, sin_ref, qn_ref, kn_ref,
                 okn_ref, oqn_ref, okr_ref, oqr_ref):
    cos = cos_ref[...]
    sin = sin_ref[...]
    lane = lax.broadcasted_iota(jnp.int32, cos.shape, 1)
    even = ((lane // 32) % 2) == 0
    scale = HEAD_DIM ** -0.5

    def normed(x, g):
        return x * lax.rsqrt(jnp.mean(x * x, axis=-1, keepdims=True) + EPS) * g

    for h in range(A_KV_HEADS):
        sl = slice(h * HEAD_DIM, (h + 1) * HEAD_DIM)
        y = normed(ak_ref[:, sl].astype(F32), kn_ref[...])
        okn_ref[:, sl] = _rope(y, cos, sin, even).astype(okn_ref.dtype)
    for h in range(A_Q_HEADS):
        sl = slice(h * HEAD_DIM, (h + 1) * HEAD_DIM)
        y = normed(aq_ref[:, sl].astype(F32), qn_ref[...])
        oqn_ref[:, sl] = (_rope(y, cos, sin, even) * scale).astype(oqn_ref.dtype)
    for h in range(C_HEADS):
        sl = slice(h * HEAD_DIM, (h + 1) * HEAD_DIM)
        okr_ref[:, sl] = (_rope(ck_ref[:, sl].astype(F32), cos, sin, even) * scale).astype(okr_ref.dtype)
        oqr_ref[:, sl] = _rope(cq_ref[:, sl].astype(F32), cos, sin, even).astype(oqr_ref.dtype)


def _prep_call(g1, g2, g3, cos_t, sin_t, a_q_norm, a_k_norm, layer):
    rows = g1.shape[0]
    tm = 256
    w = BRANCH_WIDTH
    kvw = A_KV_HEADS * HEAD_DIM
    return pl.pallas_call(
        _prep_kernel,
        grid=(rows // tm,),
        in_specs=[pl.BlockSpec((tm, kvw), lambda i: (i, 0)),
                  pl.BlockSpec((tm, w), lambda i: (i, 0)),
                  pl.BlockSpec((tm, w), lambda i: (i, 0)),
                  pl.BlockSpec((tm, w), lambda i: (i, 1)),
                  pl.BlockSpec((tm, LANES), lambda i: (i, 0)),
                  pl.BlockSpec((tm, LANES), lambda i: (i, 0)),
                  pl.BlockSpec((None, 1, HEAD_DIM), lambda i: (layer, 0, 0)),
                  pl.BlockSpec((None, 1, HEAD_DIM), lambda i: (layer, 0, 0))],
        out_specs=[pl.BlockSpec((tm, kvw), lambda i: (i, 0)),
                   pl.BlockSpec((tm, w), lambda i: (i, 0)),
                   pl.BlockSpec((tm, w), lambda i: (i, 0)),
                   pl.BlockSpec((tm, w), lambda i: (i, 0))],
        out_shape=[jax.ShapeDtypeStruct((rows, kvw), ACT_DTYPE),
                   jax.ShapeDtypeStruct((rows, w), ACT_DTYPE),
                   jax.ShapeDtypeStruct((rows, w), ACT_DTYPE),
                   jax.ShapeDtypeStruct((rows, w), ACT_DTYPE)],
        compiler_params=_params(32, 1),
        name="qk_prep",
    )(g1, g2, g3, g3, cos_t, sin_t, a_q_norm, a_k_norm)


def _attn_kernel(q_ref, kl_ref, km_ref, kr_ref, kc_ref, vl_ref, vm_ref, vr_ref, vc_ref, sink_ref,
                 o_ref, *, ctx_blocks, n_blocks):
    rb = pl.program_id(0)
    blk = A_BLOCK
    n_keys = 3 * blk + kc_ref.shape[0]
    row = lax.broadcasted_iota(jnp.int32, (A_GROUP * blk, n_keys), 0) % blk
    col = lax.broadcasted_iota(jnp.int32, (A_GROUP * blk, n_keys), 1)
    is_lat = rb >= ctx_blocks
    c_lo = jnp.where(rb >= ctx_blocks + 1, 0, blk)
    c_hi = jnp.where(rb <= n_blocks - 2, 3 * blk, 2 * blk)
    c_lo = jnp.where(is_lat, c_lo, 3 * blk)
    c_hi = jnp.where(is_lat, c_hi, 0)
    row_lo = jnp.maximum(col - 2 * blk, 0)
    row_hi = jnp.where(col < blk, col, blk - 1)
    valid = ((col >= c_lo) & (col < c_hi) & (row >= row_lo) & (row <= row_hi)) | (col >= 3 * blk)
    heads = range(A_KV_HEADS)
    hs = lambda hk: slice(hk * HEAD_DIM, (hk + 1) * HEAD_DIM)
    qs = lambda hk, g: slice((hk * A_GROUP + g) * HEAD_DIM, (hk * A_GROUP + g + 1) * HEAD_DIM)
    scores = []
    for hk in heads:
        q = jnp.concatenate([q_ref[:, qs(hk, g)] for g in range(A_GROUP)], axis=0)
        k = jnp.concatenate([kl_ref[:, hs(hk)], km_ref[:, hs(hk)], kr_ref[:, hs(hk)], kc_ref[:, hs(hk)]], axis=0)
        scores.append(lax.dot_general(q, k, (((1,), (1,)), ((), ())), preferred_element_type=F32))
    probs, denoms = [], []
    for hk in heads:
        s = jnp.where(valid, scores[hk], NEG_INF)
        sink = sink_ref[hk]
        m = jnp.maximum(jnp.max(s, axis=-1, keepdims=True), sink)
        p = jnp.exp(s - m)
        denoms.append(jnp.sum(p, axis=-1, keepdims=True) + jnp.exp(sink - m))
        probs.append(p.astype(vm_ref.dtype))
    for hk in heads:
        v = jnp.concatenate([vl_ref[:, hs(hk)], vm_ref[:, hs(hk)], vr_ref[:, hs(hk)], vc_ref[:, hs(hk)]], axis=0)
        o = jnp.dot(probs[hk], v, preferred_element_type=F32) / denoms[hk]
        for g in range(A_GROUP):
            o_ref[:, qs(hk, g)] = o[g * blk:(g + 1) * blk].astype(o_ref.dtype)


def _attn_call(qn, kn, g1, sink_col, layer, ctx_rows):
    rows = qn.shape[0]
    blk = A_BLOCK
    nb = rows // blk
    cb = ctx_rows // blk
    qw = A_Q_HEADS * HEAD_DIM
    kvw = A_KV_HEADS * HEAD_DIM
    lo = lambda r: jnp.maximum(r - 1, 0)
    hi = lambda r: jnp.minimum(r + 1, nb - 1)
    ident = lambda r: r
    kspec = lambda f: pl.BlockSpec((blk, kvw), lambda r: (f(r), 0))
    vspec = lambda f: pl.BlockSpec((blk, kvw), lambda r: (f(r), 1))
    return pl.pallas_call(
        functools.partial(_attn_kernel, ctx_blocks=cb, n_blocks=nb),
        grid=(nb,),
        in_specs=[pl.BlockSpec((blk, qw), lambda r: (r, 0)),
                  kspec(lo), kspec(ident), kspec(hi),
                  pl.BlockSpec((ctx_rows, kvw), lambda r: (0, 0)),
                  vspec(lo), vspec(ident), vspec(hi),
                  pl.BlockSpec((ctx_rows, kvw), lambda r: (0, 1)),
                  pl.BlockSpec((None, A_KV_HEADS, A_GROUP * blk, 1), lambda r: (layer, 0, 0, 0))],
        out_specs=pl.BlockSpec((blk, qw), lambda r: (r, 0)),
        out_shape=jax.ShapeDtypeStruct((rows, qw), ACT_DTYPE),
        compiler_params=_params(32, 1),
        name="window_attn",
    )(qn, kn, kn, kn, kn, g1, g1, g1, g1, sink_col)


def _gmlp_kernel(u_ref, v_ref, bn_ref, ws_ref, bias_ref, o_ref):
    for g in range(B_GROUPS):
        sl = slice(g * LANES, (g + 1) * LANES)
        v = _gelu_tanh(v_ref[:, sl].astype(F32))
        vc = v - jnp.mean(v, axis=-1, keepdims=True)
        vh = vc * lax.rsqrt(jnp.mean(vc * vc, axis=-1, keepdims=True) + EPS) * bn_ref[:, sl]
        mixed = jnp.dot(ws_ref[g], vh.astype(ws_ref.dtype), preferred_element_type=F32) + bias_ref[g]
        o_ref[:, sl] = (_gelu_tanh(u_ref[:, sl].astype(F32)) * mixed).astype(o_ref.dtype)


def _gmlp_call(g3, b_norm_flat, ws, bias_b, layer):
    rows = g3.shape[0]
    w = BRANCH_WIDTH
    return pl.pallas_call(
        _gmlp_kernel,
        grid=(rows // CHUNK,),
        in_specs=[pl.BlockSpec((CHUNK, w), lambda i: (i, 3)),
                  pl.BlockSpec((CHUNK, w), lambda i: (i, 4)),
                  pl.BlockSpec((None, 1, w), lambda i: (layer, 0, 0)),
                  pl.BlockSpec((None, B_GROUPS, CHUNK, CHUNK), lambda i: (layer, 0, 0, 0)),
                  pl.BlockSpec((None, B_GROUPS, CHUNK, LANES), lambda i: (layer, 0, 0, 0))],
        out_specs=pl.BlockSpec((CHUNK, w), lambda i: (i, 0)),
        out_shape=jax.ShapeDtypeStruct((rows, w), ACT_DTYPE),
        compiler_params=_params(32, 1),
        name="chunk_gmlp",
    )(g3, g3, b_norm_flat, ws, bias_b)


def _ret_kernel(qf_ref, kf_ref, vf_ref, qb_ref, kb_ref, vb_ref, intra_ref, qdec_ref, kdec_ref, cdec_ref,
                of_ref, ob_ref, *state_refs):
    @pl.when(pl.program_id(0) == 0)
    def _():
        for s_ref in state_refs:
            s_ref[...] = jnp.zeros(s_ref.shape, s_ref.dtype)

    dirs = ((qf_ref, kf_ref, vf_ref, of_ref), (qb_ref, kb_ref, vb_ref, ob_ref))
    chains = [(d, h) for d in range(2) for h in range(C_HEADS)]
    head = lambda h: slice(h * HEAD_DIM, (h + 1) * HEAD_DIM)

    scores = []
    for d, h in chains:
        q_ref, k_ref, _, _ = dirs[d]
        q = q_ref[:, head(h)]
        a = lax.dot_general(q, k_ref[:, head(h)], (((1,), (1,)), ((), ())), preferred_element_type=F32)
        qd = (q.astype(F32) * qdec_ref[d, h]).astype(q.dtype)
        scores.append(jnp.concatenate([(a * intra_ref[d, h]).astype(q.dtype), qd], axis=1))
    for (d, h), lhs in zip(chains, scores):
        _, _, v_ref, o_ref = dirs[d]
        v = v_ref[:, head(h)]
        rhs = jnp.concatenate([v, state_refs[d * C_HEADS + h][...].astype(v.dtype)], axis=0)
        o_ref[:, head(h)] = jnp.dot(lhs, rhs, preferred_element_type=F32).astype(o_ref.dtype)
    for d, h in chains:
        _, k_ref, v_ref, _ = dirs[d]
        s_ref = state_refs[d * C_HEADS + h]
        k = k_ref[:, head(h)]
        kd = (k.astype(F32) * kdec_ref[d, h]).astype(k.dtype)
        upd = lax.dot_general(kd, v_ref[:, head(h)], (((0,), (0,)), ((), ())), preferred_element_type=F32)
        s_ref[...] = s_ref[...] * cdec_ref[d, h] + upd


def _ret_call(qr, kr, g2, tables, layer, ctx_rows):
    rows = qr.shape[0]
    w = BRANCH_WIDTH
    nc = rows // CHUNK
    cc = ctx_rows // CHUNK

    def bwd(s):
        return jnp.where(s < cc, cc - 1 - s, nc - 1 + cc - s)

    fq = pl.BlockSpec((CHUNK, w), lambda s: (s, 0))
    fv = pl.BlockSpec((CHUNK, w), lambda s: (s, 1))
    bq = pl.BlockSpec((CHUNK, w), lambda s: (bwd(s), 0))
    bv = pl.BlockSpec((CHUNK, w), lambda s: (bwd(s), 1))
    tab = pl.BlockSpec((None, 2, C_HEADS, CHUNK, LANES), lambda s: (layer, 0, 0, 0, 0))
    return pl.pallas_call(
        _ret_kernel,
        grid=(nc,),
        in_specs=[fq, fq, fv, bq, bq, bv, tab, tab, tab, tab],
        out_specs=[pl.BlockSpec((CHUNK, w), lambda s: (s, 0)),
                   pl.BlockSpec((CHUNK, w), lambda s: (bwd(s), 0))],
        out_shape=[jax.ShapeDtypeStruct((rows, w), F32), jax.ShapeDtypeStruct((rows, w), F32)],
        scratch_shapes=[pltpu.VMEM((HEAD_DIM, HEAD_DIM), F32) for _ in range(2 * C_HEADS)],
        compiler_params=_params(32, 1),
        name="retention",
    )(qr, kr, g2, qr, kr, g2, *tables)


def _merge_kernel(attn_ref, gm_ref, of_ref, ob_ref, rg_ref, gate_ref, cn_ref, wb_ref, o_ref):
    d = o_ref.shape[1]
    ret_parts = []
    for h in range(C_HEADS):
        sl = slice(h * HEAD_DIM, (h + 1) * HEAD_DIM)
        o = of_ref[:, sl] + ob_ref[:, sl]
        oc = o - jnp.mean(o, axis=-1, keepdims=True)
        y = oc * lax.rsqrt(jnp.mean(oc * oc, axis=-1, keepdims=True) + EPS) * cn_ref[:, sl]
        ret_parts.append((_silu(rg_ref[:, sl].astype(F32)) * y).astype(wb_ref.dtype))
    ret = jnp.concatenate(ret_parts, axis=1)
    branches = (attn_ref[...].astype(wb_ref.dtype), gm_ref[...].astype(wb_ref.dtype), ret)
    acc = None
    for b in range(N_BRANCH):
        proj = jnp.dot(branches[b], wb_ref[b], preferred_element_type=F32)
        term = _sigmoid(gate_ref[:, b * d:(b + 1) * d].astype(F32)) * proj
        acc = term if acc is None else acc + term
    o_ref[...] = acc.astype(o_ref.dtype)


def _merge_call(attn, gm, o_f, o_b, g3, g4, c_norm_flat, wb, layer):
    rows = attn.shape[0]
    w = BRANCH_WIDTH
    d = wb.shape[-1]
    tm = 256
    row = lambda c: pl.BlockSpec((tm, w), lambda i: (i, c))
    return pl.pallas_call(
        _merge_kernel,
        grid=(rows // tm,),
        in_specs=[row(0), row(0), row(0), row(0), row(2),
                  pl.BlockSpec((tm, N_BRANCH * d), lambda i: (i, 0)),
                  pl.BlockSpec((None, 1, w), lambda i: (layer, 0, 0)),
                  pl.BlockSpec((None, N_BRANCH, w, d), lambda i: (layer, 0, 0, 0))],
        out_specs=pl.BlockSpec((tm, d), lambda i: (i, 0)),
        out_shape=jax.ShapeDtypeStruct((rows, d), ACT_DTYPE),
        compiler_params=_params(52, 1),
        name="branch_merge",
    )(attn, gm, o_f, o_b, g3, g4, c_norm_flat, wb)


def _outproj_kernel(m_ref, w_ref, z_ref, mod_ref, o_ref):
    y = jnp.dot(m_ref[...], w_ref[...], preferred_element_type=F32)
    o_ref[...] = z_ref[...] + mod_ref[2:3, :] * y


def _outproj_call(merged, wo, z, modsel, layer, ctx_rows):
    rows, d = z.shape
    tm = 256
    ctx_tiles = ctx_rows // tm
    return pl.pallas_call(
        _outproj_kernel,
        grid=(rows // tm,),
        in_specs=[pl.BlockSpec((tm, d), lambda i: (i, 0)),
                  pl.BlockSpec((None, d, d), lambda i: (layer, 0, 0)),
                  pl.BlockSpec((tm, d), lambda i: (i, 0)),
                  pl.BlockSpec((None, None, MOD_ROWS, d),
                               lambda i: (layer, jnp.where(i >= ctx_tiles, 1, 0), 0, 0))],
        out_specs=pl.BlockSpec((tm, d), lambda i: (i, 0)),
        out_shape=jax.ShapeDtypeStruct((rows, d), F32),
        compiler_params=_params(40, 1),
        name="out_proj",
    )(merged, wo, z, modsel)


def _route_kernel(z_ref, g_ref, mod_ref, whi_ref, wlo_ref, rb_ref, h_ref, info_ref, cnt_ref, carry_ref):
    i = pl.program_id(0)

    @pl.when(i == 0)
    def _():
        carry_ref[...] = jnp.zeros(carry_ref.shape, carry_ref.dtype)

    h = _norm_mod(z_ref[...], g_ref[...], mod_ref[...], 3, 4)
    h_ref[...] = h
    h_hi = h.astype(whi_ref.dtype)
    h_lo = (h - h_hi.astype(F32)).astype(whi_ref.dtype)
    logits = (jnp.dot(h_hi, whi_ref[...], preferred_element_type=F32)
              + jnp.dot(h_hi, wlo_ref[...], preferred_element_type=F32)
              + jnp.dot(h_lo, whi_ref[...], preferred_element_type=F32)) + rb_ref[...]
    tm = logits.shape[0]
    lane = lax.broadcasted_iota(jnp.int32, logits.shape, 1).astype(F32)
    first = lambda hit: jnp.min(jnp.where(hit, lane, 4.0 * LANES), axis=-1, keepdims=True)

    is_g = lane < N_GROUPS
    gl = jnp.where(is_g, logits, NEG_INF)
    gmax = jnp.max(gl, axis=-1, keepdims=True)
    g_sel = first(gl == gmax)
    g_w = 1.0 / jnp.sum(jnp.where(is_g, jnp.exp(gl - gmax), 0.0), axis=-1, keepdims=True)

    e_id = lane - N_GROUPS
    in_group = (e_id >= g_sel * EXPERTS_PER_GROUP) & (e_id < (g_sel + 1.0) * EXPERTS_PER_GROUP)
    el = jnp.where(in_group, logits, NEG_INF)
    m1 = jnp.max(el, axis=-1, keepdims=True)
    i1 = first(el == m1)
    el2 = jnp.where(lane == i1, NEG_INF, el)
    m2 = jnp.max(el2, axis=-1, keepdims=True)
    i2 = first(el2 == m2)
    r = jnp.exp(m2 - m1)
    w1 = g_w / (1.0 + r)
    w2 = g_w * r / (1.0 + r)
    e1 = i1 - N_GROUPS
    e2 = i2 - N_GROUPS

    hot1 = lane == e1
    hot2 = lane == e2
    hot = jnp.where(hot1 | hot2, 1.0, 0.0)
    rr = lax.broadcasted_iota(jnp.int32, (tm, tm), 0)
    cc = lax.broadcasted_iota(jnp.int32, (tm, tm), 1)
    tri = jnp.where(cc < rr, 1.0, 0.0).astype(MXU_DTYPE)
    before = jnp.dot(tri, hot.astype(MXU_DTYPE), preferred_element_type=F32) + carry_ref[0:1, :]
    rank1 = jnp.sum(jnp.where(hot1, before, 0.0), axis=-1, keepdims=True)
    rank2 = jnp.sum(jnp.where(hot2, before, 0.0), axis=-1, keepdims=True)
    carry_ref[0:1, :] = carry_ref[0:1, :] + jnp.sum(hot, axis=0, keepdims=True)
    cnt_ref[...] = carry_ref[...]

    info = jnp.where(lane == 0, e1, 0.0)
    info = jnp.where(lane == 1, e2, info)
    info = jnp.where(lane == 2, w1, info)
    info = jnp.where(lane == 3, w2, info)
    info = jnp.where(lane == 4, rank1, info)
    info = jnp.where(lane == 5, rank2, info)
    info_ref[...] = info


def _route_call(z, norm_w, modsel, w_hi, w_lo, rbias, layer, ctx_rows):
    rows, d = z.shape
    tm = 256
    ctx_tiles = ctx_rows // tm
    return pl.pallas_call(
        _route_kernel,
        grid=(rows // tm,),
        in_specs=[pl.BlockSpec((tm, d), lambda i: (i, 0)),
                  pl.BlockSpec((None, 1, d), lambda i: (layer, 0, 0)),
                  pl.BlockSpec((None, None, MOD_ROWS, d),
                               lambda i: (layer, jnp.where(i >= ctx_tiles, 1, 0), 0, 0)),
                  pl.BlockSpec((None, d, LANES), lambda i: (layer, 0, 0)),
                  pl.BlockSpec((None, d, LANES), lambda i: (layer, 0, 0)),
                  pl.BlockSpec((None, 1, LANES), lambda i: (layer, 0, 0))],
        out_specs=[pl.BlockSpec((tm, d), lambda i: (i, 0)),
                   pl.BlockSpec((tm, LANES), lambda i: (i, 0)),
                   pl.BlockSpec((8, LANES), lambda i: (0, 0))],
        out_shape=[jax.ShapeDtypeStruct((rows, d), F32),
                   jax.ShapeDtypeStruct((rows, LANES), F32),
                   jax.ShapeDtypeStruct((8, LANES), F32)],
        scratch_shapes=[pltpu.VMEM((8, LANES), F32)],
        compiler_params=_params(32, 1),
        name="route",
    )(z, norm_w, modsel, w_hi, w_lo, rbias)


def _dispatch_kernel(pad_end_ref, padded_ref, n_used_ref, dest_ref, h_ref, xs_ref, zero_ref, sem, zero_sem):
    tm = h_ref.shape[0]
    n_blocks = xs_ref.shape[0] // MOE_BLOCK

    @pl.when(pl.program_id(0) == 0)
    def _():
        zero_ref[...] = jnp.zeros(zero_ref.shape, zero_ref.dtype)

        def block_copy(first):
            first = pl.multiple_of(first, MOE_BLOCK)
            return pltpu.make_async_copy(zero_ref, xs_ref.at[pl.ds(first, MOE_BLOCK)], zero_sem)

        def start_unused(b, carry):
            block_copy(b * MOE_BLOCK).start()
            return carry

        def wait_unused(b, carry):
            block_copy(b * MOE_BLOCK).wait()
            return carry

        for e in range(N_EXPERTS):
            @pl.when(padded_ref[e] > 0)
            def _(e=e):
                block_copy(pad_end_ref[e] - MOE_BLOCK).start()
        lax.fori_loop(n_used_ref[0], n_blocks, start_unused, 0)
        for e in range(N_EXPERTS):
            @pl.when(padded_ref[e] > 0)
            def _(e=e):
                block_copy(pad_end_ref[e] - MOE_BLOCK).wait()
        lax.fori_loop(n_used_ref[0], n_blocks, wait_unused, 0)

    def row_copy(t, slot):
        return pltpu.make_async_copy(h_ref.at[pl.ds(t, 1)], xs_ref.at[pl.ds(slot, 1)], sem)

    def start(t, carry):
        row_copy(t, dest_ref[0, 0, 2 * t]).start()
        row_copy(t, dest_ref[0, 0, 2 * t + 1]).start()
        return carry

    lax.fori_loop(0, tm, start, 0, unroll=4)
    all_rows = pltpu.make_async_copy(h_ref, xs_ref.at[pl.ds(0, tm)], sem)
    all_rows.wait()
    all_rows.wait()


def _dispatch_call(pad_end, padded, n_used, dest3, h, n_slots):
    rows, d = h.shape
    tm = dest3.shape[2] // 2
    grid_spec = pltpu.PrefetchScalarGridSpec(
        num_scalar_prefetch=3,
        grid=(rows // tm,),
        in_specs=[pl.BlockSpec((1, 1, 2 * tm), lambda i, pe, pd, nu: (i, 0, 0), memory_space=pltpu.SMEM),
                  pl.BlockSpec((tm, d), lambda i, pe, pd, nu: (i, 0))],
        out_specs=pl.BlockSpec(memory_space=pl.ANY),
        scratch_shapes=[pltpu.VMEM((MOE_BLOCK, d), h.dtype),
                        pltpu.SemaphoreType.DMA(()), pltpu.SemaphoreType.DMA(())])
    return pl.pallas_call(
        _dispatch_kernel,
        grid_spec=grid_spec,
        out_shape=jax.ShapeDtypeStruct((n_slots, d), h.dtype),
        compiler_params=_params(32, 1),
        name="moe_dispatch",
    )(pad_end, padded, n_used, dest3, h)


def _expert_kernel(be_ref, nu_ref, xs_ref, w1_ref, w2_ref, ys_ref, w1bf_ref, w2bf_ref):
    i = pl.program_id(0)
    changed = jnp.logical_or(i == 0, be_ref[i] != be_ref[jnp.maximum(i - 1, 0)])
    used = i < nu_ref[0]

    @pl.when(jnp.logical_and(changed, used))
    def _():
        w1bf_ref[...] = w1_ref[...].astype(w1bf_ref.dtype)
        w2bf_ref[...] = w2_ref[...].astype(w2bf_ref.dtype)

    @pl.when(used)
    def _():
        de = w2bf_ref.shape[0]
        hcat = jnp.dot(xs_ref[...].astype(w1bf_ref.dtype), w1bf_ref[...], preferred_element_type=F32)
        act = _silu(hcat[:, :de]) * hcat[:, de:]
        ys_ref[...] = jnp.dot(act.astype(w2bf_ref.dtype), w2bf_ref[...], preferred_element_type=F32)

    @pl.when(jnp.logical_not(used))
    def _():
        ys_ref[...] = jnp.zeros(ys_ref.shape, ys_ref.dtype)


def _expert_call(block_e, n_used, xs, w_e1, w_e2, layer):
    slots, d = xs.shape
    de = w_e2.shape[2]
    nb = slots // MOE_BLOCK
    blk = lambda i, be, nu: (jnp.minimum(i, nu[0] - 1), 0)
    grid_spec = pltpu.PrefetchScalarGridSpec(
        num_scalar_prefetch=2,
        grid=(nb,),
        in_specs=[pl.BlockSpec((MOE_BLOCK, d), blk),
                  pl.BlockSpec((None, None, d, 2 * de), lambda i, be, nu: (layer, be[i], 0, 0)),
                  pl.BlockSpec((None, None, de, d), lambda i, be, nu: (layer, be[i], 0, 0))],
        out_specs=pl.BlockSpec((MOE_BLOCK, d), lambda i, be, nu: (i, 0)),
        scratch_shapes=[pltpu.VMEM((d, 2 * de), MXU_DTYPE), pltpu.VMEM((de, d), MXU_DTYPE)])
    return pl.pallas_call(
        _expert_kernel,
        grid_spec=grid_spec,
        out_shape=jax.ShapeDtypeStruct((slots, d), F32),
        compiler_params=_params(48, 1),
        name="moe_experts",
    )(block_e, n_used, xs, w_e1, w_e2)


def _combine_kernel(dest_ref, ys_ref, info_ref, z_ref, mod_ref, o_ref, buf_ref, sem):
    tm = z_ref.shape[0]

    def row_copy(t, k, slot):
        return pltpu.make_async_copy(ys_ref.at[pl.ds(slot, 1)], buf_ref.at[k, pl.ds(t, 1)], sem)

    def start(t, carry):
        row_copy(t, 0, dest_ref[0, 0, 2 * t]).start()
        row_copy(t, 1, dest_ref[0, 0, 2 * t + 1]).start()
        return carry

    lax.fori_loop(0, tm, start, 0, unroll=4)
    for k in range(2):
        pltpu.make_async_copy(ys_ref.at[pl.ds(0, tm)], buf_ref.at[k], sem).wait()
    info = info_ref[...]
    lane = lax.broadcasted_iota(jnp.int32, info.shape, 1)
    w1 = jnp.sum(jnp.where(lane == 2, info, 0.0), axis=-1, keepdims=True)
    w2 = jnp.sum(jnp.where(lane == 3, info, 0.0), axis=-1, keepdims=True)
    y = buf_ref[0] * w1 + buf_ref[1] * w2
    o_ref[...] = z_ref[...] + mod_ref[5:6, :] * y


def _combine_call(dest3, ys, info, z, modsel, layer, ctx_rows):
    rows, d = z.shape
    tm = dest3.shape[2] // 2
    ctx_tiles = ctx_rows // tm
    return pl.pallas_call(
        _combine_kernel,
        grid=(rows // tm,),
        in_specs=[pl.BlockSpec((1, 1, 2 * tm), lambda i: (i, 0, 0), memory_space=pltpu.SMEM),
                  pl.BlockSpec(memory_space=pl.ANY),
                  pl.BlockSpec((tm, LANES), lambda i: (i, 0)),
                  pl.BlockSpec((tm, d), lambda i: (i, 0)),
                  pl.BlockSpec((None, None, MOD_ROWS, d),
                               lambda i: (layer, jnp.where(i >= ctx_tiles, 1, 0), 0, 0))],
        out_specs=pl.BlockSpec((tm, d), lambda i: (i, 0)),
        out_shape=jax.ShapeDtypeStruct((rows, d), F32),
        scratch_shapes=[pltpu.VMEM((2, tm, d), F32), pltpu.SemaphoreType.DMA(())],
        compiler_params=_params(32, 1),
        name="moe_combine",
    )(dest3, ys, info, z, modsel)


def _rope_tables(n, ctx_rows):
    rows = n // GRID_W
    row = jnp.repeat(jnp.arange(rows, dtype=F32), GRID_W)
    col = jnp.tile(jnp.arange(GRID_W, dtype=F32), rows)
    nq = HEAD_DIM // 4
    inv = ROPE_BASE ** (-jnp.arange(nq, dtype=F32) / nq)
    ar, ac = row[:, None] * inv, col[:, None] * inv
    cos = jnp.concatenate([jnp.cos(ar), jnp.cos(ar), jnp.cos(ac), jnp.cos(ac)], axis=1)
    sin = jnp.concatenate([-jnp.sin(ar), jnp.sin(ar), -jnp.sin(ac), jnp.sin(ac)], axis=1)
    cos = jnp.concatenate([jnp.ones((ctx_rows, HEAD_DIM), F32), cos], axis=0)
    sin = jnp.concatenate([jnp.zeros((ctx_rows, HEAD_DIM), F32), sin], axis=0)
    return cos, sin


def _retention_tables(c_decay_fwd, c_decay_bwd):
    lg_f = jax.nn.log_sigmoid(c_decay_fwd.astype(F32))[:, :, None, None]
    lg_b = jax.nn.log_sigmoid(c_decay_bwd.astype(F32))[:, :, None, None]
    idx = jnp.arange(CHUNK, dtype=F32)
    diff = idx[:, None] - idx[None, :]
    ones = jnp.ones((CHUNK, CHUNK), F32)
    t_col = idx[:, None] * ones
    intra_f = jnp.where(diff >= 0, jnp.exp(lg_f * jnp.maximum(diff, 0.0)), 0.0)
    intra_b = jnp.where(diff <= 0, jnp.exp(lg_b * jnp.maximum(-diff, 0.0)), 0.0)
    qdec_f = jnp.exp(lg_f * (t_col + 1.0))
    qdec_b = jnp.exp(lg_b * (CHUNK - t_col))
    kdec_f = jnp.exp(lg_f * (CHUNK - 1.0 - t_col))
    kdec_b = jnp.exp(lg_b * t_col)
    cdec_f = jnp.exp(lg_f * CHUNK) * ones
    cdec_b = jnp.exp(lg_b * CHUNK) * ones
    pair = lambda a, b: jnp.stack([a, b], axis=1)
    return (pair(intra_f, intra_b), pair(qdec_f, qdec_b), pair(kdec_f, kdec_b), pair(cdec_f, cdec_b))


def _moe_plan(info, counts_row, n_slots_blocks):
    e = info[:, 0:2].astype(jnp.int32)
    rank = info[:, 4:6].astype(jnp.int32)
    counts = counts_row[:N_EXPERTS].astype(jnp.int32)
    padded = (counts + MOE_BLOCK - 1) // MOE_BLOCK * MOE_BLOCK
    pad_end = jnp.cumsum(padded)
    pad_start = pad_end - padded
    dest = pad_start[e] + rank
    n_used = pad_end[-1] // MOE_BLOCK
    blocks = jnp.arange(n_slots_blocks, dtype=jnp.int32)
    first_slot = jnp.minimum(blocks, n_used - 1) * MOE_BLOCK
    block_e = jnp.sum((pad_end[None, :] <= first_slot[:, None]).astype(jnp.int32), axis=1)
    return (dest, block_e.astype(jnp.int32), n_used.reshape(1).astype(jnp.int32),
            pad_end.astype(jnp.int32), padded.astype(jnp.int32))


def kernel(x, c, ctx, c_ctx, norm_mix, norm_ffn, w_ada, b_ada, w_in, a_q_norm, a_k_norm, a_sink,
           b_norm, b_spatial, b_spatial_bias, c_decay_fwd, c_decay_bwd, c_norm, w_branch, w_out,
           w_router_group, b_router_group, w_router_expert, b_router_expert, w_expert_in, w_expert_out):
    batch, n, d = x.shape
    ctx_rows = ctx.shape[1]
    depth = w_in.shape[0]
    assert batch == 1 and ctx_rows % 256 == 0 and n % 256 == 0
    rows = ctx_rows + n

    cond = jnp.stack([c[0], c_ctx], axis=0)
    cond_b = jnp.broadcast_to(cond[:, :, None], (2, d, LANES))
    mod = _ada_call(cond_b, w_ada, b_ada)[:, :2].reshape(depth, 2, N_MOD, d)
    modsel = jnp.pad(mod[:, ::-1], ((0, 0), (0, 0), (0, MOD_ROWS - N_MOD), (0, 0)))

    cos_t, sin_t = _rope_tables(n, ctx_rows)
    ret_tables = _retention_tables(c_decay_fwd, c_decay_bwd)
    sink_col = jnp.broadcast_to(a_sink.astype(F32).reshape(depth, A_KV_HEADS, A_GROUP, 1, 1),
                                (depth, A_KV_HEADS, A_GROUP, A_BLOCK, 1)).reshape(depth, A_KV_HEADS, A_GROUP * A_BLOCK, 1)
    ws = b_spatial.astype(MXU_DTYPE)
    bias_b = jnp.broadcast_to(b_spatial_bias.astype(F32)[:, :, :, None], (depth, B_GROUPS, CHUNK, LANES))
    wb = w_branch.astype(MXU_DTYPE)
    wo = w_out.astype(MXU_DTYPE)
    w_r = jnp.concatenate([w_router_group, w_router_expert], axis=-1).astype(F32)
    w_r = jnp.pad(w_r, ((0, 0), (0, 0), (0, LANES - w_r.shape[-1])))
    w_r_hi = w_r.astype(MXU_DTYPE)
    w_r_lo = (w_r - w_r_hi.astype(F32)).astype(MXU_DTYPE)
    b_r = jnp.concatenate([b_router_group, b_router_expert], axis=-1).astype(F32)
    b_r = jnp.pad(b_r, ((0, 0), (0, LANES - b_r.shape[-1]))).reshape(depth, 1, LANES)
    norm_mix3 = norm_mix.reshape(depth, 1, d)
    norm_ffn3 = norm_ffn.reshape(depth, 1, d)
    a_q_norm3 = a_q_norm.reshape(depth, 1, HEAD_DIM)
    a_k_norm3 = a_k_norm.reshape(depth, 1, HEAD_DIM)
    b_norm3 = b_norm.reshape(depth, 1, BRANCH_WIDTH)
    c_norm3 = c_norm.reshape(depth, 1, BRANCH_WIDTH)

    kvw = A_KV_HEADS * HEAD_DIM
    w = BRANCH_WIDTH
    n_assign = rows * 2
    n_slot_blocks = -(-(n_assign + N_EXPERTS * (MOE_BLOCK - 1)) // MOE_BLOCK)
    tok_tile = 128

    z = jnp.concatenate([ctx[0], x[0]], axis=0)
    for l in range(depth):
        h = _norm_mod_call(z, norm_mix3, modsel, l, ctx_rows)
        g1 = _proj_call(h, w_in, l, 0, 2 * kvw)
        g2 = _proj_call(h, w_in, l, 2 * kvw, 2 * w)
        g3 = _proj_call(h, w_in, l, 2 * kvw + 2 * w, 5 * w)
        g4 = _proj_call(h, w_in, l, 2 * kvw + 7 * w, N_BRANCH * d)
        kn, qn, kr, qr = _prep_call(g1, g2, g3, cos_t, sin_t, a_q_norm3, a_k_norm3, l)
        attn = _attn_call(qn, kn, g1, sink_col, l, ctx_rows)
        gm = _gmlp_call(g3, b_norm3, ws, bias_b, l)
        o_f, o_b = _ret_call(qr, kr, g2, ret_tables, l, ctx_rows)
        merged = _merge_call(attn, gm, o_f, o_b, g3, g4, c_norm3, wb, l)
        z = _outproj_call(merged, wo, z, modsel, l, ctx_rows)

        h2, info, counts = _route_call(z, norm_ffn3, modsel, w_r_hi, w_r_lo, b_r, l, ctx_rows)
        dest, block_e, n_used, pad_end, padded = _moe_plan(info, counts[0], n_slot_blocks)
        dest3 = dest.reshape(rows // tok_tile, 1, 2 * tok_tile)
        xs = _dispatch_call(pad_end, padded, n_used, dest3, h2, n_slot_blocks * MOE_BLOCK)
        ys = _expert_call(block_e, n_used, xs, w_expert_in, w_expert_out, l)
        z = _combine_call(dest3, ys, info, z, modsel, l, ctx_rows)
    return z[ctx_rows:][None]
```

```python
import functools
import math

import jax
import jax.numpy as jnp
from jax import lax
from jax.experimental import pallas as pl
from jax.experimental.pallas import tpu as pltpu

F32 = jnp.float32
MXU_DTYPE = jnp.bfloat16
ACT_DTYPE = jnp.bfloat16

LANES = 128
HEAD_DIM = 128
GRID_W = 64
ROPE_BASE = 10000.0
EPS = 1e-6
NEG_INF = -1e30
A_Q_HEADS = 8
A_KV_HEADS = 2
A_GROUP = A_Q_HEADS // A_KV_HEADS
A_BLOCK = 128
B_GROUPS = 8
C_HEADS = 8
CHUNK = 128
N_GROUPS = 4
EXPERTS_PER_GROUP = 8
N_EXPERTS = N_GROUPS * EXPERTS_PER_GROUP
D_EXPERT = 512
BRANCH_WIDTH = 1024
N_BRANCH = 3
MOE_BLOCK = 256
N_MOD = 6
MOD_ROWS = 8
MIB = 1024 * 1024


def _params(vmem_mib, n_grid, **kw):
    return pltpu.CompilerParams(dimension_semantics=("arbitrary",) * n_grid,
                                vmem_limit_bytes=vmem_mib * MIB, **kw)


def _silu(x):
    return x * (1.0 / (1.0 + jnp.exp(-x)))


def _sigmoid(x):
    return 1.0 / (1.0 + jnp.exp(-x))


def _gelu_tanh(x):
    return 0.5 * x * (1.0 + jnp.tanh(math.sqrt(2.0 / math.pi) * (x + 0.044715 * (x * x * x))))


def _ada_kernel(c_ref, w_ref, b_ref, o_ref):
    tn = w_ref.shape[1]
    s0 = _silu(c_ref[0])
    s1 = _silu(c_ref[1])
    o_ref[...] = jnp.zeros(o_ref.shape, o_ref.dtype)
    for j in range(tn // LANES):
        sl = slice(j * LANES, (j + 1) * LANES)
        wj = w_ref[:, sl]
        o_ref[0:1, sl] = jnp.sum(wj * s0, axis=0, keepdims=True) + b_ref[:, sl]
        o_ref[1:2, sl] = jnp.sum(wj * s1, axis=0, keepdims=True) + b_ref[:, sl]


def _ada_call(cond_b, w_ada, b_ada):
    depth, k, n = w_ada.shape
    tn = 1024
    return pl.pallas_call(
        _ada_kernel,
        grid=(depth, n // tn),
        in_specs=[pl.BlockSpec((2, k, LANES), lambda l, j: (0, 0, 0)),
                  pl.BlockSpec((None, k, tn), lambda l, j: (l, 0, j)),
                  pl.BlockSpec((None, 1, tn), lambda l, j: (l, 0, j))],
        out_specs=pl.BlockSpec((None, 8, tn), lambda l, j: (l, 0, j)),
        out_shape=jax.ShapeDtypeStruct((depth, 8, n), F32),
        compiler_params=_params(40, 2),
        name="adaln",
    )(cond_b, w_ada, b_ada.reshape(depth, 1, n))


def _norm_mod(z, g, mod, shift_row, scale_row):
    r = lax.rsqrt(jnp.mean(z * z, axis=-1, keepdims=True) + EPS)
    return (z * r * g) * (1.0 + mod[scale_row:scale_row + 1, :]) + mod[shift_row:shift_row + 1, :]


def _norm_mod_kernel(z_ref, g_ref, mod_ref, o_ref):
    o_ref[...] = _norm_mod(z_ref[...], g_ref[...], mod_ref[...], 0, 1).astype(o_ref.dtype)


def _norm_mod_call(z, norm_w, modsel, layer, ctx_rows):
    rows, d = z.shape
    tm = 256
    ctx_tiles = ctx_rows // tm
    return pl.pallas_call(
        _norm_mod_kernel,
        grid=(rows // tm,),
        in_specs=[pl.BlockSpec((tm, d), lambda i: (i, 0)),
                  pl.BlockSpec((None, 1, d), lambda i: (layer, 0, 0)),
                  pl.BlockSpec((None, None, MOD_ROWS, d),
                               lambda i: (layer, jnp.where(i >= ctx_tiles, 1, 0), 0, 0))],
        out_specs=pl.BlockSpec((tm, d), lambda i: (i, 0)),
        out_shape=jax.ShapeDtypeStruct((rows, d), ACT_DTYPE),
        compiler_params=_params(32, 1),
        name="norm_mod",
    )(z, norm_w, modsel)


def _proj_kernel(h_ref, w_ref, o_ref, wbf_ref):
    @pl.when(pl.program_id(1) == 0)
    def _():
        wbf_ref[...] = w_ref[...].astype(wbf_ref.dtype)

    o_ref[...] = jnp.dot(h_ref[...], wbf_ref[...], preferred_element_type=F32).astype(o_ref.dtype)


def _row_tile(rows, pref):
    for t in pref:
        if rows % t == 0:
            return t
    raise ValueError(f"no row tile for {rows}")


def _proj_call(h, w, layer, col_off, ncols):
    rows, k = h.shape
    tn = next(t for t in (1536, 1280, 1024, 512) if col_off % t == 0 and ncols % t == 0)
    tm = _row_tile(rows, (768, 512, 256))
    off = col_off // tn
    return pl.pallas_call(
        _proj_kernel,
        grid=(ncols // tn, rows // tm),
        in_specs=[pl.BlockSpec((tm, k), lambda j, i: (i, 0)),
                  pl.BlockSpec((None, k, tn), lambda j, i: (layer, 0, off + j))],
        out_specs=pl.BlockSpec((tm, tn), lambda j, i: (i, j)),
        out_shape=jax.ShapeDtypeStruct((rows, ncols), ACT_DTYPE),
        scratch_shapes=[pltpu.VMEM((k, tn), MXU_DTYPE)],
        compiler_params=_params(52, 2),
        name="proj_in",
    )(h, w)


def _prep_kernel(ak_ref, ck_ref, aq_ref, cq_ref, cos_ref, sin_ref, qn_ref, kn_ref, perm_ref,
                 okn_ref, oqn_ref, okr_ref, oqr_ref):
    cos = cos_ref[...]
    sin = sin_ref[...]
    perm = perm_ref[...]
    scale = HEAD_DIM ** -0.5

    def swap(x):
        return jnp.dot(x, perm, preferred_element_type=F32)

    def norm_rope(x, g_ref, out_scale):
        xf = x.astype(F32)
        r = lax.rsqrt(jnp.mean(xf * xf, axis=-1, keepdims=True) + EPS) * out_scale
        return (xf * (g_ref[0:1, :] * cos) + swap(x) * (g_ref[1:2, :] * sin)) * r

    for h in range(A_KV_HEADS):
        sl = slice(h * HEAD_DIM, (h + 1) * HEAD_DIM)
        okn_ref[:, sl] = norm_rope(ak_ref[:, sl], kn_ref, 1.0).astype(okn_ref.dtype)
    for h in range(A_Q_HEADS):
        sl = slice(h * HEAD_DIM, (h + 1) * HEAD_DIM)
        oqn_ref[:, sl] = norm_rope(aq_ref[:, sl], qn_ref, scale).astype(oqn_ref.dtype)
    for h in range(C_HEADS):
        sl = slice(h * HEAD_DIM, (h + 1) * HEAD_DIM)
        k = ck_ref[:, sl]
        q = cq_ref[:, sl]
        okr_ref[:, sl] = ((k.astype(F32) * cos + swap(k) * sin) * scale).astype(okr_ref.dtype)
        oqr_ref[:, sl] = (q.astype(F32) * cos + swap(q) * sin).astype(oqr_ref.dtype)


def _prep_call(g1, g2, g3, cos_t, sin_t, a_q_norm, a_k_norm, perm, layer):
    rows = g1.shape[0]
    tm = 256
    w = BRANCH_WIDTH
    kvw = A_KV_HEADS * HEAD_DIM
    return pl.pallas_call(
        _prep_kernel,
        grid=(rows // tm,),
        in_specs=[pl.BlockSpec((tm, kvw), lambda i: (i, 0)),
                  pl.BlockSpec((tm, w), lambda i: (i, 0)),
                  pl.BlockSpec((tm, w), lambda i: (i, 0)),
                  pl.BlockSpec((tm, w), lambda i: (i, 1)),
                  pl.BlockSpec((tm, LANES), lambda i: (i, 0)),
                  pl.BlockSpec((tm, LANES), lambda i: (i, 0)),
                  pl.BlockSpec((None, 2, HEAD_DIM), lambda i: (layer, 0, 0)),
                  pl.BlockSpec((None, 2, HEAD_DIM), lambda i: (layer, 0, 0)),
                  pl.BlockSpec((HEAD_DIM, HEAD_DIM), lambda i: (0, 0))],
        out_specs=[pl.BlockSpec((tm, kvw), lambda i: (i, 0)),
                   pl.BlockSpec((tm, w), lambda i: (i, 0)),
                   pl.BlockSpec((tm, w), lambda i: (i, 0)),
                   pl.BlockSpec((tm, w), lambda i: (i, 0))],
        out_shape=[jax.ShapeDtypeStruct((rows, kvw), ACT_DTYPE),
                   jax.ShapeDtypeStruct((rows, w), ACT_DTYPE),
                   jax.ShapeDtypeStruct((rows, w), ACT_DTYPE),
                   jax.ShapeDtypeStruct((rows, w), ACT_DTYPE)],
        compiler_params=_params(32, 1),
        name="qk_prep",
    )(g1, g2, g3, g3, cos_t, sin_t, a_q_norm, a_k_norm, perm)


def _attn_kernel(q_ref, kl_ref, km_ref, kr_ref, kc_ref, vl_ref, vm_ref, vr_ref, vc_ref, sink_ref,
                 o_ref, *, ctx_blocks, n_blocks):
    rb = pl.program_id(0)
    blk = A_BLOCK
    n_keys = 3 * blk + kc_ref.shape[0]
    row = lax.broadcasted_iota(jnp.int32, (A_GROUP * blk, n_keys), 0) % blk
    col = lax.broadcasted_iota(jnp.int32, (A_GROUP * blk, n_keys), 1)
    is_lat = rb >= ctx_blocks
    c_lo = jnp.where(rb >= ctx_blocks + 1, 0, blk)
    c_hi = jnp.where(rb <= n_blocks - 2, 3 * blk, 2 * blk)
    c_lo = jnp.where(is_lat, c_lo, 3 * blk)
    c_hi = jnp.where(is_lat, c_hi, 0)
    row_lo = jnp.maximum(col - 2 * blk, 0)
    row_hi = jnp.where(col < blk, col, blk - 1)
    valid = ((col >= c_lo) & (col < c_hi) & (row >= row_lo) & (row <= row_hi)) | (col >= 3 * blk)
    heads = range(A_KV_HEADS)
    hs = lambda hk: slice(hk * HEAD_DIM, (hk + 1) * HEAD_DIM)
    qs = lambda hk, g: slice((hk * A_GROUP + g) * HEAD_DIM, (hk * A_GROUP + g + 1) * HEAD_DIM)
    scores = []
    for hk in heads:
        q = jnp.concatenate([q_ref[:, qs(hk, g)] for g in range(A_GROUP)], axis=0)
        k = jnp.concatenate([kl_ref[:, hs(hk)], km_ref[:, hs(hk)], kr_ref[:, hs(hk)], kc_ref[:, hs(hk)]], axis=0)
        scores.append(lax.dot_general(q, k, (((1,), (1,)), ((), ())), preferred_element_type=F32))
    probs, denoms = [], []
    for hk in heads:
        s = jnp.where(valid, scores[hk], NEG_INF)
        sink = sink_ref[hk]
        m = jnp.maximum(jnp.max(s, axis=-1, keepdims=True), sink)
        p = jnp.exp(s - m)
        denoms.append(jnp.sum(p, axis=-1, keepdims=True) + jnp.exp(sink - m))
        probs.append(p.astype(vm_ref.dtype))
    for hk in heads:
        v = jnp.concatenate([vl_ref[:, hs(hk)], vm_ref[:, hs(hk)], vr_ref[:, hs(hk)], vc_ref[:, hs(hk)]], axis=0)
        o = jnp.dot(probs[hk], v, preferred_element_type=F32) / denoms[hk]
        for g in range(A_GROUP):
            o_ref[:, qs(hk, g)] = o[g * blk:(g + 1) * blk].astype(o_ref.dtype)


def _attn_call(qn, kn, g1, sink_col, layer, ctx_rows):
    rows = qn.shape[0]
    blk = A_BLOCK
    nb = rows // blk
    cb = ctx_rows // blk
    qw = A_Q_HEADS * HEAD_DIM
    kvw = A_KV_HEADS * HEAD_DIM
    lo = lambda r: jnp.maximum(r - 1, 0)
    hi = lambda r: jnp.minimum(r + 1, nb - 1)
    ident = lambda r: r
    kspec = lambda f: pl.BlockSpec((blk, kvw), lambda r: (f(r), 0))
    vspec = lambda f: pl.BlockSpec((blk, kvw), lambda r: (f(r), 1))
    return pl.pallas_call(
        functools.partial(_attn_kernel, ctx_blocks=cb, n_blocks=nb),
        grid=(nb,),
        in_specs=[pl.BlockSpec((blk, qw), lambda r: (r, 0)),
                  kspec(lo), kspec(ident), kspec(hi),
                  pl.BlockSpec((ctx_rows, kvw), lambda r: (0, 0)),
                  vspec(lo), vspec(ident), vspec(hi),
                  pl.BlockSpec((ctx_rows, kvw), lambda r: (0, 1)),
                  pl.BlockSpec((None, A_KV_HEADS, A_GROUP * blk, 1), lambda r: (layer, 0, 0, 0))],
        out_specs=pl.BlockSpec((blk, qw), lambda r: (r, 0)),
        out_shape=jax.ShapeDtypeStruct((rows, qw), ACT_DTYPE),
        compiler_params=_params(32, 1),
        name="window_attn",
    )(qn, kn, kn, kn, kn, g1, g1, g1, g1, sink_col)


def _gmlp_kernel(u_ref, v_ref, bn_ref, ws_ref, bias_ref, o_ref):
    for g in range(B_GROUPS):
        sl = slice(g * LANES, (g + 1) * LANES)
        v = _gelu_tanh(v_ref[:, sl].astype(F32))
        vc = v - jnp.mean(v, axis=-1, keepdims=True)
        vh = vc * lax.rsqrt(jnp.mean(vc * vc, axis=-1, keepdims=True) + EPS) * bn_ref[:, sl]
        mixed = jnp.dot(ws_ref[g], vh.astype(ws_ref.dtype), preferred_element_type=F32) + bias_ref[g]
        o_ref[:, sl] = (_gelu_tanh(u_ref[:, sl].astype(F32)) * mixed).astype(o_ref.dtype)


def _gmlp_call(g3, b_norm_flat, ws, bias_b, layer):
    rows = g3.shape[0]
    w = BRANCH_WIDTH
    return pl.pallas_call(
        _gmlp_kernel,
        grid=(rows // CHUNK,),
        in_specs=[pl.BlockSpec((CHUNK, w), lambda i: (i, 3)),
                  pl.BlockSpec((CHUNK, w), lambda i: (i, 4)),
                  pl.BlockSpec((None, 1, w), lambda i: (layer, 0, 0)),
                  pl.BlockSpec((None, B_GROUPS, CHUNK, CHUNK), lambda i: (layer, 0, 0, 0)),
                  pl.BlockSpec((None, B_GROUPS, CHUNK, LANES), lambda i: (layer, 0, 0, 0))],
        out_specs=pl.BlockSpec((CHUNK, w), lambda i: (i, 0)),
        out_shape=jax.ShapeDtypeStruct((rows, w), ACT_DTYPE),
        compiler_params=_params(32, 1),
        name="chunk_gmlp",
    )(g3, g3, b_norm_flat, ws, bias_b)


def _ret_kernel(qf_ref, kf_ref, vf_ref, qb_ref, kb_ref, vb_ref, intra_ref, qdec_ref, kdec_ref, cdec_ref,
                of_ref, ob_ref, *state_refs):
    @pl.when(pl.program_id(0) == 0)
    def _():
        for s_ref in state_refs:
            s_ref[...] = jnp.zeros(s_ref.shape, s_ref.dtype)

    dirs = ((qf_ref, kf_ref, vf_ref, of_ref), (qb_ref, kb_ref, vb_ref, ob_ref))
    chains = [(d, h) for d in range(2) for h in range(C_HEADS)]
    head = lambda h: slice(h * HEAD_DIM, (h + 1) * HEAD_DIM)

    scores = []
    for d, h in chains:
        q_ref, k_ref, _, _ = dirs[d]
        q = q_ref[:, head(h)]
        a = lax.dot_general(q, k_ref[:, head(h)], (((1,), (1,)), ((), ())), preferred_element_type=F32)
        qd = (q.astype(F32) * qdec_ref[d, h]).astype(q.dtype)
        scores.append(jnp.concatenate([(a * intra_ref[d, h]).astype(q.dtype), qd], axis=1))
    for (d, h), lhs in zip(chains, scores):
        _, _, v_ref, o_ref = dirs[d]
        v = v_ref[:, head(h)]
        rhs = jnp.concatenate([v, state_refs[d * C_HEADS + h][...].astype(v.dtype)], axis=0)
        o_ref[:, head(h)] = jnp.dot(lhs, rhs, preferred_element_type=F32).astype(o_ref.dtype)
    for d, h in chains:
        _, k_ref, v_ref, _ = dirs[d]
        s_ref = state_refs[d * C_HEADS + h]
        k = k_ref[:, head(h)]
        kd = (k.astype(F32) * kdec_ref[d, h]).astype(k.dtype)
        upd = lax.dot_general(kd, v_ref[:, head(h)], (((0,), (0,)), ((), ())), preferred_element_type=F32)
        s_ref[...] = s_ref[...] * cdec_ref[d, h] + upd


def _ret_call(qr, kr, g2, tables, layer, ctx_rows):
    rows = qr.shape[0]
    w = BRANCH_WIDTH
    nc = rows // CHUNK
    cc = ctx_rows // CHUNK

    def bwd(s):
        return jnp.where(s < cc, cc - 1 - s, nc - 1 + cc - s)

    fq = pl.BlockSpec((CHUNK, w), lambda s: (s, 0))
    fv = pl.BlockSpec((CHUNK, w), lambda s: (s, 1))
    bq = pl.BlockSpec((CHUNK, w), lambda s: (bwd(s), 0))
    bv = pl.BlockSpec((CHUNK, w), lambda s: (bwd(s), 1))
    tab = pl.BlockSpec((None, 2, C_HEADS, CHUNK, LANES), lambda s: (layer, 0, 0, 0, 0))
    return pl.pallas_call(
        _ret_kernel,
        grid=(nc,),
        in_specs=[fq, fq, fv, bq, bq, bv, tab, tab, tab, tab],
        out_specs=[pl.BlockSpec((CHUNK, w), lambda s: (s, 0)),
                   pl.BlockSpec((CHUNK, w), lambda s: (bwd(s), 0))],
        out_shape=[jax.ShapeDtypeStruct((rows, w), F32), jax.ShapeDtypeStruct((rows, w), F32)],
        scratch_shapes=[pltpu.VMEM((HEAD_DIM, HEAD_DIM), F32) for _ in range(2 * C_HEADS)],
        compiler_params=_params(32, 1),
        name="retention",
    )(qr, kr, g2, qr, kr, g2, *tables)


def _merge_kernel(attn_ref, gm_ref, of_ref, ob_ref, rg_ref, gate_ref, cn_ref, wb_ref, o_ref):
    d = o_ref.shape[1]
    ret_parts = []
    for h in range(C_HEADS):
        sl = slice(h * HEAD_DIM, (h + 1) * HEAD_DIM)
        o = of_ref[:, sl] + ob_ref[:, sl]
        oc = o - jnp.mean(o, axis=-1, keepdims=True)
        y = oc * lax.rsqrt(jnp.mean(oc * oc, axis=-1, keepdims=True) + EPS) * cn_ref[:, sl]
        ret_parts.append((_silu(rg_ref[:, sl].astype(F32)) * y).astype(wb_ref.dtype))
    ret = jnp.concatenate(ret_parts, axis=1)
    branches = (attn_ref[...].astype(wb_ref.dtype), gm_ref[...].astype(wb_ref.dtype), ret)
    acc = None
    for b in range(N_BRANCH):
        proj = jnp.dot(branches[b], wb_ref[b], preferred_element_type=F32)
        term = _sigmoid(gate_ref[:, b * d:(b + 1) * d].astype(F32)) * proj
        acc = term if acc is None else acc + term
    o_ref[...] = acc.astype(o_ref.dtype)


def _merge_call(attn, gm, o_f, o_b, g3, g4, c_norm_flat, wb, layer):
    rows = attn.shape[0]
    w = BRANCH_WIDTH
    d = wb.shape[-1]
    tm = 256
    row = lambda c: pl.BlockSpec((tm, w), lambda i: (i, c))
    return pl.pallas_call(
        _merge_kernel,
        grid=(rows // tm,),
        in_specs=[row(0), row(0), row(0), row(0), row(2),
                  pl.BlockSpec((tm, N_BRANCH * d), lambda i: (i, 0)),
                  pl.BlockSpec((None, 1, w), lambda i: (layer, 0, 0)),
                  pl.BlockSpec((None, N_BRANCH, w, d), lambda i: (layer, 0, 0, 0))],
        out_specs=pl.BlockSpec((tm, d), lambda i: (i, 0)),
        out_shape=jax.ShapeDtypeStruct((rows, d), ACT_DTYPE),
        compiler_params=_params(52, 1),
        name="branch_merge",
    )(attn, gm, o_f, o_b, g3, g4, c_norm_flat, wb)


def _route_kernel(m_ref, wo_ref, z_ref, g_ref, mod_ref, whi_ref, wlo_ref, rb_ref,
                  znew_ref, h_ref, info_ref, cnt_ref, carry_ref):
    i = pl.program_id(0)

    @pl.when(i == 0)
    def _():
        carry_ref[...] = jnp.zeros(carry_ref.shape, carry_ref.dtype)

    z = z_ref[...] + mod_ref[2:3, :] * jnp.dot(m_ref[...], wo_ref[...], preferred_element_type=F32)
    znew_ref[...] = z
    h = _norm_mod(z, g_ref[...], mod_ref[...], 3, 4)
    h_ref[...] = h
    h_hi = h.astype(whi_ref.dtype)
    h_lo = (h - h_hi.astype(F32)).astype(whi_ref.dtype)
    logits = (jnp.dot(h_hi, whi_ref[...], preferred_element_type=F32)
              + jnp.dot(h_hi, wlo_ref[...], preferred_element_type=F32)
              + jnp.dot(h_lo, whi_ref[...], preferred_element_type=F32)) + rb_ref[...]
    tm = logits.shape[0]
    lane = lax.broadcasted_iota(jnp.int32, logits.shape, 1).astype(F32)
    first = lambda hit: jnp.min(jnp.where(hit, lane, 4.0 * LANES), axis=-1, keepdims=True)

    is_g = lane < N_GROUPS
    gl = jnp.where(is_g, logits, NEG_INF)
    gmax = jnp.max(gl, axis=-1, keepdims=True)
    g_sel = first(gl == gmax)
    g_w = 1.0 / jnp.sum(jnp.where(is_g, jnp.exp(gl - gmax), 0.0), axis=-1, keepdims=True)

    e_id = lane - N_GROUPS
    in_group = (e_id >= g_sel * EXPERTS_PER_GROUP) & (e_id < (g_sel + 1.0) * EXPERTS_PER_GROUP)
    el = jnp.where(in_group, logits, NEG_INF)
    m1 = jnp.max(el, axis=-1, keepdims=True)
    i1 = first(el == m1)
    el2 = jnp.where(lane == i1, NEG_INF, el)
    m2 = jnp.max(el2, axis=-1, keepdims=True)
    i2 = first(el2 == m2)
    r = jnp.exp(m2 - m1)
    w1 = g_w / (1.0 + r)
    w2 = g_w * r / (1.0 + r)
    e1 = i1 - N_GROUPS
    e2 = i2 - N_GROUPS

    hot1 = lane == e1
    hot2 = lane == e2
    hot = jnp.where(hot1 | hot2, 1.0, 0.0)
    rr = lax.broadcasted_iota(jnp.int32, (tm, tm), 0)
    cc = lax.broadcasted_iota(jnp.int32, (tm, tm), 1)
    tri = jnp.where(cc < rr, 1.0, 0.0).astype(MXU_DTYPE)
    before = jnp.dot(tri, hot.astype(MXU_DTYPE), preferred_element_type=F32) + carry_ref[0:1, :]
    rank1 = jnp.sum(jnp.where(hot1, before, 0.0), axis=-1, keepdims=True)
    rank2 = jnp.sum(jnp.where(hot2, before, 0.0), axis=-1, keepdims=True)
    carry_ref[0:1, :] = carry_ref[0:1, :] + jnp.sum(hot, axis=0, keepdims=True)
    cnt_ref[...] = carry_ref[...]

    info = jnp.where(lane == 0, e1, 0.0)
    info = jnp.where(lane == 1, e2, info)
    info = jnp.where(lane == 2, w1, info)
    info = jnp.where(lane == 3, w2, info)
    info = jnp.where(lane == 4, rank1, info)
    info = jnp.where(lane == 5, rank2, info)
    info_ref[...] = info


def _route_call(merged, wo, z, norm_w, modsel, w_hi, w_lo, rbias, layer, ctx_rows):
    rows, d = z.shape
    tm = 256
    ctx_tiles = ctx_rows // tm
    return pl.pallas_call(
        _route_kernel,
        grid=(rows // tm,),
        in_specs=[pl.BlockSpec((tm, d), lambda i: (i, 0)),
                  pl.BlockSpec((None, d, d), lambda i: (layer, 0, 0)),
                  pl.BlockSpec((tm, d), lambda i: (i, 0)),
                  pl.BlockSpec((None, 1, d), lambda i: (layer, 0, 0)),
                  pl.BlockSpec((None, None, MOD_ROWS, d),
                               lambda i: (layer, jnp.where(i >= ctx_tiles, 1, 0), 0, 0)),
                  pl.BlockSpec((None, d, LANES), lambda i: (layer, 0, 0)),
                  pl.BlockSpec((None, d, LANES), lambda i: (layer, 0, 0)),
                  pl.BlockSpec((None, 1, LANES), lambda i: (layer, 0, 0))],
        out_specs=[pl.BlockSpec((tm, d), lambda i: (i, 0)),
                   pl.BlockSpec((tm, d), lambda i: (i, 0)),
                   pl.BlockSpec((tm, LANES), lambda i: (i, 0)),
                   pl.BlockSpec((8, LANES), lambda i: (0, 0))],
        out_shape=[jax.ShapeDtypeStruct((rows, d), F32),
                   jax.ShapeDtypeStruct((rows, d), F32),
                   jax.ShapeDtypeStruct((rows, LANES), F32),
                   jax.ShapeDtypeStruct((8, LANES), F32)],
        scratch_shapes=[pltpu.VMEM((8, LANES), F32)],
        compiler_params=_params(48, 1),
        name="outproj_route",
    )(merged, wo, z, norm_w, modsel, w_hi, w_lo, rbias)


def _dispatch_kernel(pad_end_ref, padded_ref, n_used_ref, dest_ref, h_ref, xs_ref, zero_ref, sem, zero_sem):
    tm = h_ref.shape[0]
    n_blocks = xs_ref.shape[0] // MOE_BLOCK

    @pl.when(pl.program_id(0) == 0)
    def _():
        zero_ref[...] = jnp.zeros(zero_ref.shape, zero_ref.dtype)

        def block_copy(first):
            first = pl.multiple_of(first, MOE_BLOCK)
            return pltpu.make_async_copy(zero_ref, xs_ref.at[pl.ds(first, MOE_BLOCK)], zero_sem)

        def start_unused(b, carry):
            block_copy(b * MOE_BLOCK).start()
            return carry

        def wait_unused(b, carry):
            block_copy(b * MOE_BLOCK).wait()
            return carry

        for e in range(N_EXPERTS):
            @pl.when(padded_ref[e] > 0)
            def _(e=e):
                block_copy(pad_end_ref[e] - MOE_BLOCK).start()
        lax.fori_loop(n_used_ref[0], n_blocks, start_unused, 0)
        for e in range(N_EXPERTS):
            @pl.when(padded_ref[e] > 0)
            def _(e=e):
                block_copy(pad_end_ref[e] - MOE_BLOCK).wait()
        lax.fori_loop(n_used_ref[0], n_blocks, wait_unused, 0)

    def row_copy(t, slot):
        return pltpu.make_async_copy(h_ref.at[pl.ds(t, 1)], xs_ref.at[pl.ds(slot, 1)], sem)

    def start(t, carry):
        row_copy(t, dest_ref[0, 0, 2 * t]).start()
        row_copy(t, dest_ref[0, 0, 2 * t + 1]).start()
        return carry

    lax.fori_loop(0, tm, start, 0, unroll=4)
    all_rows = pltpu.make_async_copy(h_ref, xs_ref.at[pl.ds(0, tm)], sem)
    all_rows.wait()
    all_rows.wait()


def _dispatch_call(pad_end, padded, n_used, dest3, h, n_slots):
    rows, d = h.shape
    tm = dest3.shape[2] // 2
    grid_spec = pltpu.PrefetchScalarGridSpec(
        num_scalar_prefetch=3,
        grid=(rows // tm,),
        in_specs=[pl.BlockSpec((1, 1, 2 * tm), lambda i, pe, pd, nu: (i, 0, 0), memory_space=pltpu.SMEM),
                  pl.BlockSpec((tm, d), lambda i, pe, pd, nu: (i, 0))],
        out_specs=pl.BlockSpec(memory_space=pl.ANY),
        scratch_shapes=[pltpu.VMEM((MOE_BLOCK, d), h.dtype),
                        pltpu.SemaphoreType.DMA(()), pltpu.SemaphoreType.DMA(())])
    return pl.pallas_call(
        _dispatch_kernel,
        grid_spec=grid_spec,
        out_shape=jax.ShapeDtypeStruct((n_slots, d), h.dtype),
        compiler_params=_params(32, 1),
        name="moe_dispatch",
    )(pad_end, padded, n_used, dest3, h)


def _expert_kernel(be_ref, nu_ref, first_ref, slot_ref, next_ref, xs_ref, w1_hbm, w2_hbm, ys_ref,
                   w1f_ref, w2f_ref, w1bf_ref, w2bf_ref, sem, *, layer):
    i = pl.program_id(0)
    used = i < nu_ref[0]

    def fetch(e, s):
        return (pltpu.make_async_copy(w1_hbm.at[layer, e], w1f_ref.at[s], sem.at[s]),
                pltpu.make_async_copy(w2_hbm.at[layer, e], w2f_ref.at[s], sem.at[s]))

    @pl.when(i == 0)
    def _():
        for cp in fetch(be_ref[0], slot_ref[0]):
            cp.start()

    @pl.when(jnp.logical_and(used, first_ref[i] == 1))
    def _():
        s = slot_ref[i]
        for cp in fetch(be_ref[i], s):
            cp.wait()

        @pl.when(next_ref[i] >= 0)
        def _():
            for cp in fetch(next_ref[i], 1 - s):
                cp.start()

        w1bf_ref[...] = w1f_ref[s].astype(w1bf_ref.dtype)
        w2bf_ref[...] = w2f_ref[s].astype(w2bf_ref.dtype)

    @pl.when(used)
    def _():
        de = w2bf_ref.shape[0]
        hcat = jnp.dot(xs_ref[...].astype(w1bf_ref.dtype), w1bf_ref[...], preferred_element_type=F32)
        act = _silu(hcat[:, :de]) * hcat[:, de:]
        ys_ref[...] = jnp.dot(act.astype(w2bf_ref.dtype), w2bf_ref[...], preferred_element_type=F32)

    @pl.when(jnp.logical_not(used))
    def _():
        ys_ref[...] = jnp.zeros(ys_ref.shape, ys_ref.dtype)


def _expert_call(plan, xs, w_e1, w_e2, layer):
    slots, d = xs.shape
    de = w_e2.shape[2]
    nb = slots // MOE_BLOCK
    blk = lambda i, be, nu, *_: (jnp.minimum(i, nu[0] - 1), 0)
    grid_spec = pltpu.PrefetchScalarGridSpec(
        num_scalar_prefetch=5,
        grid=(nb,),
        in_specs=[pl.BlockSpec((MOE_BLOCK, d), blk),
                  pl.BlockSpec(memory_space=pl.ANY),
                  pl.BlockSpec(memory_space=pl.ANY)],
        out_specs=pl.BlockSpec((MOE_BLOCK, d), lambda i, *_: (i, 0)),
        scratch_shapes=[pltpu.VMEM((2, d, 2 * de), w_e1.dtype), pltpu.VMEM((2, de, d), w_e2.dtype),
                        pltpu.VMEM((d, 2 * de), MXU_DTYPE), pltpu.VMEM((de, d), MXU_DTYPE),
                        pltpu.SemaphoreType.DMA((2,))])
    return pl.pallas_call(
        functools.partial(_expert_kernel, layer=layer),
        grid_spec=grid_spec,
        out_shape=jax.ShapeDtypeStruct((slots, d), F32),
        compiler_params=_params(48, 1),
        name="moe_experts",
    )(plan["block_e"], plan["n_used"], plan["first"], plan["slot"], plan["next_e"], xs, w_e1, w_e2)


def _combine_kernel(dest_ref, dest_next_ref, ys_ref, info_ref, z_ref, mod_ref, *rest, emit_next):
    if emit_next:
        gn_ref, modn_ref, o_ref, hn_ref, buf_ref, sem = rest
    else:
        o_ref, buf_ref, sem = rest
    i = pl.program_id(0)
    tm = z_ref.shape[0]
    slot = i % 2

    def issue(d_ref, s):
        def start(t, carry):
            for k in range(2):
                pltpu.make_async_copy(ys_ref.at[pl.ds(d_ref[0, 0, 2 * t + k], 1)],
                                      buf_ref.at[s, k, pl.ds(t, 1)], sem.at[s]).start()
            return carry

        lax.fori_loop(0, tm, start, 0, unroll=4)

    @pl.when(i == 0)
    def _():
        issue(dest_ref, 0)

    @pl.when(i + 1 < pl.num_programs(0))
    def _():
        issue(dest_next_ref, 1 - slot)

    for k in range(2):
        pltpu.make_async_copy(ys_ref.at[pl.ds(0, tm)], buf_ref.at[slot, k], sem.at[slot]).wait()
    info = info_ref[...]
    lane = lax.broadcasted_iota(jnp.int32, info.shape, 1)
    w1 = jnp.sum(jnp.where(lane == 2, info, 0.0), axis=-1, keepdims=True)
    w2 = jnp.sum(jnp.where(lane == 3, info, 0.0), axis=-1, keepdims=True)
    y = buf_ref[slot, 0] * w1 + buf_ref[slot, 1] * w2
    z = z_ref[...] + mod_ref[5:6, :] * y
    o_ref[...] = z
    if emit_next:
        hn_ref[...] = _norm_mod(z, gn_ref[...], modn_ref[...], 0, 1).astype(hn_ref.dtype)


def _combine_call(dest3, ys, info, z, modsel, layer, ctx_rows, next_norm_w=None):
    rows, d = z.shape
    tm = dest3.shape[2] // 2
    ctx_tiles = ctx_rows // tm
    n_tiles = rows // tm
    emit_next = next_norm_w is not None
    mod_spec = lambda l: pl.BlockSpec((None, None, MOD_ROWS, d),
                                      lambda i: (l, jnp.where(i >= ctx_tiles, 1, 0), 0, 0))
    row_spec = pl.BlockSpec((tm, d), lambda i: (i, 0))
    in_specs = [pl.BlockSpec((1, 1, 2 * tm), lambda i: (i, 0, 0), memory_space=pltpu.SMEM),
                pl.BlockSpec((1, 1, 2 * tm), lambda i: (jnp.minimum(i + 1, n_tiles - 1), 0, 0),
                             memory_space=pltpu.SMEM),
                pl.BlockSpec(memory_space=pl.ANY),
                pl.BlockSpec((tm, LANES), lambda i: (i, 0)),
                row_spec,
                mod_spec(layer)]
    args = [dest3, dest3, ys, info, z, modsel]
    out_specs = [row_spec]
    out_shape = [jax.ShapeDtypeStruct((rows, d), F32)]
    if emit_next:
        in_specs += [pl.BlockSpec((None, 1, d), lambda i: (layer + 1, 0, 0)), mod_spec(layer + 1)]
        args += [next_norm_w, modsel]
        out_specs.append(row_spec)
        out_shape.append(jax.ShapeDtypeStruct((rows, d), ACT_DTYPE))
    return pl.pallas_call(
        functools.partial(_combine_kernel, emit_next=emit_next),
        grid=(n_tiles,),
        in_specs=in_specs,
        out_specs=out_specs,
        out_shape=out_shape,
        scratch_shapes=[pltpu.VMEM((2, 2, tm, d), F32), pltpu.SemaphoreType.DMA((2,))],
        compiler_params=_params(40, 1),
        name="moe_combine",
    )(*args)


def _rope_tables(n, ctx_rows):
    rows = n // GRID_W
    row = jnp.repeat(jnp.arange(rows, dtype=F32), GRID_W)
    col = jnp.tile(jnp.arange(GRID_W, dtype=F32), rows)
    nq = HEAD_DIM // 4
    inv = ROPE_BASE ** (-jnp.arange(nq, dtype=F32) / nq)
    ar, ac = row[:, None] * inv, col[:, None] * inv
    cos = jnp.concatenate([jnp.cos(ar), jnp.cos(ar), jnp.cos(ac), jnp.cos(ac)], axis=1)
    sin = jnp.concatenate([-jnp.sin(ar), jnp.sin(ar), -jnp.sin(ac), jnp.sin(ac)], axis=1)
    cos = jnp.concatenate([jnp.ones((ctx_rows, HEAD_DIM), F32), cos], axis=0)
    sin = jnp.concatenate([jnp.zeros((ctx_rows, HEAD_DIM), F32), sin], axis=0)
    return cos, sin


def _retention_tables(c_decay_fwd, c_decay_bwd):
    lg_f = jax.nn.log_sigmoid(c_decay_fwd.astype(F32))[:, :, None, None]
    lg_b = jax.nn.log_sigmoid(c_decay_bwd.astype(F32))[:, :, None, None]
    idx = jnp.arange(CHUNK, dtype=F32)
    diff = idx[:, None] - idx[None, :]
    ones = jnp.ones((CHUNK, CHUNK), F32)
    t_col = idx[:, None] * ones
    intra_f = jnp.where(diff >= 0, jnp.exp(lg_f * jnp.maximum(diff, 0.0)), 0.0)
    intra_b = jnp.where(diff <= 0, jnp.exp(lg_b * jnp.maximum(-diff, 0.0)), 0.0)
    qdec_f = jnp.exp(lg_f * (t_col + 1.0))
    qdec_b = jnp.exp(lg_b * (CHUNK - t_col))
    kdec_f = jnp.exp(lg_f * (CHUNK - 1.0 - t_col))
    kdec_b = jnp.exp(lg_b * t_col)
    cdec_f = jnp.exp(lg_f * CHUNK) * ones
    cdec_b = jnp.exp(lg_b * CHUNK) * ones
    pair = lambda a, b: jnp.stack([a, b], axis=1)
    return (pair(intra_f, intra_b), pair(qdec_f, qdec_b), pair(kdec_f, kdec_b), pair(cdec_f, cdec_b))


def _moe_plan(info, counts_row, n_slots_blocks):
    e = info[:, 0:2].astype(jnp.int32)
    rank = info[:, 4:6].astype(jnp.int32)
    counts = counts_row[:N_EXPERTS].astype(jnp.int32)
    padded = (counts + MOE_BLOCK - 1) // MOE_BLOCK * MOE_BLOCK
    pad_end = jnp.cumsum(padded)
    pad_start = pad_end - padded
    hit = e[:, :, None] == jnp.arange(N_EXPERTS, dtype=jnp.int32)
    dest = jnp.sum(jnp.where(hit, pad_start, 0), axis=-1) + rank
    n_used = pad_end[-1] // MOE_BLOCK
    blocks = jnp.arange(n_slots_blocks, dtype=jnp.int32)
    first_slot = jnp.minimum(blocks, n_used - 1) * MOE_BLOCK
    block_e = jnp.sum((pad_end[None, :] <= first_slot[:, None]).astype(jnp.int32), axis=1)
    ids = jnp.arange(N_EXPERTS, dtype=jnp.int32)
    nonempty = counts > 0
    slot_of = (jnp.cumsum(nonempty.astype(jnp.int32)) - 1) % 2
    later = nonempty[None, :] & (ids[None, :] > ids[:, None])
    next_of = jnp.min(jnp.where(later, ids[None, :], N_EXPERTS), axis=1)
    next_of = jnp.where(next_of == N_EXPERTS, -1, next_of)
    pick = lambda table: jnp.sum(jnp.where(block_e[:, None] == ids[None, :], table[None, :], 0), axis=1)
    prev_e = jnp.concatenate([jnp.full((1,), -1, jnp.int32), block_e[:-1]])
    first = ((block_e != prev_e) & (blocks < n_used)).astype(jnp.int32)
    i32 = lambda a: a.astype(jnp.int32)
    return dict(dest=dest, block_e=i32(block_e), n_used=i32(n_used.reshape(1)), pad_end=i32(pad_end),
                padded=i32(padded), first=first, slot=i32(pick(slot_of)), next_e=i32(pick(next_of)))


def kernel(x, c, ctx, c_ctx, norm_mix, norm_ffn, w_ada, b_ada, w_in, a_q_norm, a_k_norm, a_sink,
           b_norm, b_spatial, b_spatial_bias, c_decay_fwd, c_decay_bwd, c_norm, w_branch, w_out,
           w_router_group, b_router_group, w_router_expert, b_router_expert, w_expert_in, w_expert_out):
    batch, n, d = x.shape
    ctx_rows = ctx.shape[1]
    depth = w_in.shape[0]
    assert batch == 1 and ctx_rows % 256 == 0 and n % 256 == 0
    rows = ctx_rows + n

    cond = jnp.stack([c[0], c_ctx], axis=0)
    cond_b = jnp.broadcast_to(cond[:, :, None], (2, d, LANES))
    mod = _ada_call(cond_b, w_ada, b_ada)[:, :2].reshape(depth, 2, N_MOD, d)
    modsel = jnp.pad(mod[:, ::-1], ((0, 0), (0, 0), (0, MOD_ROWS - N_MOD), (0, 0)))

    cos_t, sin_t = _rope_tables(n, ctx_rows)
    ret_tables = _retention_tables(c_decay_fwd, c_decay_bwd)
    sink_col = jnp.broadcast_to(a_sink.astype(F32).reshape(depth, A_KV_HEADS, A_GROUP, 1, 1),
                                (depth, A_KV_HEADS, A_GROUP, A_BLOCK, 1)).reshape(depth, A_KV_HEADS, A_GROUP * A_BLOCK, 1)
    ws = b_spatial.astype(MXU_DTYPE)
    bias_b = jnp.broadcast_to(b_spatial_bias.astype(F32)[:, :, :, None], (depth, B_GROUPS, CHUNK, LANES))
    wb = w_branch.astype(MXU_DTYPE)
    wo = w_out.astype(MXU_DTYPE)
    w_r = jnp.concatenate([w_router_group, w_router_expert], axis=-1).astype(F32)
    w_r = jnp.pad(w_r, ((0, 0), (0, 0), (0, LANES - w_r.shape[-1])))
    w_r_hi = w_r.astype(MXU_DTYPE)
    w_r_lo = (w_r - w_r_hi.astype(F32)).astype(MXU_DTYPE)
    b_r = jnp.concatenate([b_router_group, b_router_expert], axis=-1).astype(F32)
    b_r = jnp.pad(b_r, ((0, 0), (0, LANES - b_r.shape[-1]))).reshape(depth, 1, LANES)
    norm_mix3 = norm_mix.reshape(depth, 1, d)
    norm_ffn3 = norm_ffn.reshape(depth, 1, d)
    lane_ids = jnp.arange(HEAD_DIM)
    partner = jnp.where((lane_ids // 32) % 2 == 0, lane_ids + 32, lane_ids - 32)
    perm = (lane_ids[:, None] == partner[None, :]).astype(ACT_DTYPE)
    a_q_norm3 = jnp.stack([a_q_norm, a_q_norm[:, partner]], axis=1).astype(F32)
    a_k_norm3 = jnp.stack([a_k_norm, a_k_norm[:, partner]], axis=1).astype(F32)
    b_norm3 = b_norm.reshape(depth, 1, BRANCH_WIDTH)
    c_norm3 = c_norm.reshape(depth, 1, BRANCH_WIDTH)

    kvw = A_KV_HEADS * HEAD_DIM
    w = BRANCH_WIDTH
    n_assign = rows * 2
    n_slot_blocks = -(-(n_assign + N_EXPERTS * (MOE_BLOCK - 1)) // MOE_BLOCK)
    tok_tile = 256

    z = jnp.concatenate([ctx[0], x[0]], axis=0)
    h = _norm_mod_call(z, norm_mix3, modsel, 0, ctx_rows)
    for l in range(depth):
        g1 = _proj_call(h, w_in, l, 0, 2 * kvw)
        g2 = _proj_call(h, w_in, l, 2 * kvw, 2 * w)
        g3 = _proj_call(h, w_in, l, 2 * kvw + 2 * w, 5 * w)
        g4 = _proj_call(h, w_in, l, 2 * kvw + 7 * w, N_BRANCH * d)
        kn, qn, kr, qr = _prep_call(g1, g2, g3, cos_t, sin_t, a_q_norm3, a_k_norm3, perm, l)
        attn = _attn_call(qn, kn, g1, sink_col, l, ctx_rows)
        gm = _gmlp_call(g3, b_norm3, ws, bias_b, l)
        o_f, o_b = _ret_call(qr, kr, g2, ret_tables, l, ctx_rows)
        merged = _merge_call(attn, gm, o_f, o_b, g3, g4, c_norm3, wb, l)
        z, h2, info, counts = _route_call(merged, wo, z, norm_ffn3, modsel, w_r_hi, w_r_lo, b_r, l, ctx_rows)
        plan = _moe_plan(info, counts[0], n_slot_blocks)
        dest3 = plan["dest"].reshape(rows // tok_tile, 1, 2 * tok_tile)
        xs = _dispatch_call(plan["pad_end"], plan["padded"], plan["n_used"], dest3, h2, n_slot_blocks * MOE_BLOCK)
        ys = _expert_call(plan, xs, w_expert_in, w_expert_out, l)
        if l + 1 < depth:
            z, h = _combine_call(dest3, ys, info, z, modsel, l, ctx_rows, next_norm_w=norm_mix3)
        else:
            (z,) = _combine_call(dest3, ys, info, z, modsel, l, ctx_rows)
    return z[ctx_rows:][None]
```

```python
import functools
import math

import jax
import jax.numpy as jnp
from jax import lax
from jax.experimental import pallas as pl
from jax.experimental.pallas import tpu as pltpu

F32 = jnp.float32
MXU_DTYPE = jnp.bfloat16
ACT_DTYPE = jnp.bfloat16

LANES = 128
HEAD_DIM = 128
GRID_W = 64
ROPE_BASE = 10000.0
EPS = 1e-6
NEG_INF = -1e30
A_Q_HEADS = 8
A_KV_HEADS = 2
A_GROUP = A_Q_HEADS // A_KV_HEADS
A_BLOCK = 128
B_GROUPS = 8
C_HEADS = 8
CHUNK = 128
N_GROUPS = 4
EXPERTS_PER_GROUP = 8
N_EXPERTS = N_GROUPS * EXPERTS_PER_GROUP
D_EXPERT = 512
BRANCH_WIDTH = 1024
N_BRANCH = 3
MOE_BLOCK = 256
N_MOD = 6
MOD_ROWS = 8
MIB = 1024 * 1024


def _params(vmem_mib, n_grid, **kw):
    return pltpu.CompilerParams(dimension_semantics=("arbitrary",) * n_grid,
                                vmem_limit_bytes=vmem_mib * MIB, **kw)


def _silu(x):
    return x * (1.0 / (1.0 + jnp.exp(-x)))


def _sigmoid(x):
    return 1.0 / (1.0 + jnp.exp(-x))


def _gelu_tanh(x):
    return 0.5 * x * (1.0 + jnp.tanh(math.sqrt(2.0 / math.pi) * (x + 0.044715 * (x * x * x))))


def _ada_kernel(c_ref, w_ref, b_ref, o_ref):
    tn = w_ref.shape[1]
    s0 = _silu(c_ref[0])
    s1 = _silu(c_ref[1])
    o_ref[...] = jnp.zeros(o_ref.shape, o_ref.dtype)
    for j in range(tn // LANES):
        sl = slice(j * LANES, (j + 1) * LANES)
        wj = w_ref[:, sl]
        o_ref[0:1, sl] = jnp.sum(wj * s0, axis=0, keepdims=True) + b_ref[:, sl]
        o_ref[1:2, sl] = jnp.sum(wj * s1, axis=0, keepdims=True) + b_ref[:, sl]


def _ada_call(cond_b, w_ada, b_ada):
    depth, k, n = w_ada.shape
    tn = 1024
    return pl.pallas_call(
        _ada_kernel,
        grid=(depth, n // tn),
        in_specs=[pl.BlockSpec((2, k, LANES), lambda l, j: (0, 0, 0)),
                  pl.BlockSpec((None, k, tn), lambda l, j: (l, 0, j)),
                  pl.BlockSpec((None, 1, tn), lambda l, j: (l, 0, j))],
        out_specs=pl.BlockSpec((None, 8, tn), lambda l, j: (l, 0, j)),
        out_shape=jax.ShapeDtypeStruct((depth, 8, n), F32),
        compiler_params=_params(40, 2),
        name="adaln",
    )(cond_b, w_ada, b_ada.reshape(depth, 1, n))


def _norm_mod(z, g, mod, shift_row, scale_row):
    r = lax.rsqrt(jnp.mean(z * z, axis=-1, keepdims=True) + EPS)
    return (z * r * g) * (1.0 + mod[scale_row:scale_row + 1, :]) + mod[shift_row:shift_row + 1, :]


def _norm_mod_kernel(z_ref, g_ref, mod_ref, o_ref):
    o_ref[...] = _norm_mod(z_ref[...], g_ref[...], mod_ref[...], 0, 1).astype(o_ref.dtype)


def _norm_mod_call(z, norm_w, modsel, layer, ctx_rows):
    rows, d = z.shape
    tm = 256
    ctx_tiles = ctx_rows // tm
    return pl.pallas_call(
        _norm_mod_kernel,
        grid=(rows // tm,),
        in_specs=[pl.BlockSpec((tm, d), lambda i: (i, 0)),
                  pl.BlockSpec((None, 1, d), lambda i: (layer, 0, 0)),
                  pl.BlockSpec((None, None, MOD_ROWS, d),
                               lambda i: (layer, jnp.where(i >= ctx_tiles, 1, 0), 0, 0))],
        out_specs=pl.BlockSpec((tm, d), lambda i: (i, 0)),
        out_shape=jax.ShapeDtypeStruct((rows, d), ACT_DTYPE),
        compiler_params=_params(32, 1),
        name="norm_mod",
    )(z, norm_w, modsel)


def _proj_kernel(h_ref, w_ref, o_ref, wbf_ref):
    @pl.when(pl.program_id(1) == 0)
    def _():
        wbf_ref[...] = w_ref[...].astype(wbf_ref.dtype)

    o_ref[...] = jnp.dot(h_ref[...], wbf_ref[...], preferred_element_type=F32).astype(o_ref.dtype)


def _row_tile(rows, pref):
    for t in pref:
        if rows % t == 0:
            return t
    raise ValueError(f"no row tile for {rows}")


def _proj_call(h, w, layer, col_off, ncols):
    rows, k = h.shape
    tn = next(t for t in (1536, 1280, 1024, 512) if col_off % t == 0 and ncols % t == 0)
    tm = _row_tile(rows, (768, 512, 256))
    off = col_off // tn
    return pl.pallas_call(
        _proj_kernel,
        grid=(ncols // tn, rows // tm),
        in_specs=[pl.BlockSpec((tm, k), lambda j, i: (i, 0)),
                  pl.BlockSpec((None, k, tn), lambda j, i: (layer, 0, off + j))],
        out_specs=pl.BlockSpec((tm, tn), lambda j, i: (i, j)),
        out_shape=jax.ShapeDtypeStruct((rows, ncols), ACT_DTYPE),
        scratch_shapes=[pltpu.VMEM((k, tn), MXU_DTYPE)],
        compiler_params=_params(52, 2),
        name="proj_in",
    )(h, w)


def _prep_kernel(ak_ref, ck_ref, aq_ref, cq_ref, cos_ref, sin_ref, qn_ref, kn_ref, perm_ref,
                 okn_ref, oqn_ref, okr_ref, oqr_ref):
    cos = cos_ref[...]
    sin = sin_ref[...]
    perm = perm_ref[...]
    scale = HEAD_DIM ** -0.5

    def swap(x):
        return jnp.dot(x, perm, preferred_element_type=F32)

    def norm_rope(x, g_ref, out_scale):
        xf = x.astype(F32)
        r = lax.rsqrt(jnp.mean(xf * xf, axis=-1, keepdims=True) + EPS) * out_scale
        return (xf * (g_ref[0:1, :] * cos) + swap(x) * (g_ref[1:2, :] * sin)) * r

    for h in range(A_KV_HEADS):
        sl = slice(h * HEAD_DIM, (h + 1) * HEAD_DIM)
        okn_ref[:, sl] = norm_rope(ak_ref[:, sl], kn_ref, 1.0).astype(okn_ref.dtype)
    for h in range(A_Q_HEADS):
        sl = slice(h * HEAD_DIM, (h + 1) * HEAD_DIM)
        oqn_ref[:, sl] = norm_rope(aq_ref[:, sl], qn_ref, scale).astype(oqn_ref.dtype)
    for h in range(C_HEADS):
        sl = slice(h * HEAD_DIM, (h + 1) * HEAD_DIM)
        k = ck_ref[:, sl]
        q = cq_ref[:, sl]
        okr_ref[:, sl] = ((k.astype(F32) * cos + swap(k) * sin) * scale).astype(okr_ref.dtype)
        oqr_ref[:, sl] = (q.astype(F32) * cos + swap(q) * sin).astype(oqr_ref.dtype)


def _prep_call(g1, g2, g3, cos_t, sin_t, a_q_norm, a_k_norm, perm, layer):
    rows = g1.shape[0]
    tm = _row_tile(rows, (768, 512, 256))
    w = BRANCH_WIDTH
    kvw = A_KV_HEADS * HEAD_DIM
    return pl.pallas_call(
        _prep_kernel,
        grid=(rows // tm,),
        in_specs=[pl.BlockSpec((tm, kvw), lambda i: (i, 0)),
                  pl.BlockSpec((tm, w), lambda i: (i, 0)),
                  pl.BlockSpec((tm, w), lambda i: (i, 0)),
                  pl.BlockSpec((tm, w), lambda i: (i, 1)),
                  pl.BlockSpec((tm, LANES), lambda i: (i, 0)),
                  pl.BlockSpec((tm, LANES), lambda i: (i, 0)),
                  pl.BlockSpec((None, 2, HEAD_DIM), lambda i: (layer, 0, 0)),
                  pl.BlockSpec((None, 2, HEAD_DIM), lambda i: (layer, 0, 0)),
                  pl.BlockSpec((HEAD_DIM, HEAD_DIM), lambda i: (0, 0))],
        out_specs=[pl.BlockSpec((tm, kvw), lambda i: (i, 0)),
                   pl.BlockSpec((tm, w), lambda i: (i, 0)),
                   pl.BlockSpec((tm, w), lambda i: (i, 0)),
                   pl.BlockSpec((tm, w), lambda i: (i, 0))],
        out_shape=[jax.ShapeDtypeStruct((rows, kvw), ACT_DTYPE),
                   jax.ShapeDtypeStruct((rows, w), ACT_DTYPE),
                   jax.ShapeDtypeStruct((rows, w), ACT_DTYPE),
                   jax.ShapeDtypeStruct((rows, w), ACT_DTYPE)],
        compiler_params=_params(40, 1),
        name="qk_prep",
    )(g1, g2, g3, g3, cos_t, sin_t, a_q_norm, a_k_norm, perm)


def _attn_kernel(q_ref, kl_ref, km_ref, kr_ref, kc_ref, vl_ref, vm_ref, vr_ref, vc_ref, sink_ref,
                 o_ref, *, ctx_blocks, n_blocks):
    rb = pl.program_id(0)
    blk = A_BLOCK
    n_keys = 3 * blk + kc_ref.shape[0]
    row = lax.broadcasted_iota(jnp.int32, (A_GROUP * blk, n_keys), 0) % blk
    col = lax.broadcasted_iota(jnp.int32, (A_GROUP * blk, n_keys), 1)
    is_lat = rb >= ctx_blocks
    c_lo = jnp.where(rb >= ctx_blocks + 1, 0, blk)
    c_hi = jnp.where(rb <= n_blocks - 2, 3 * blk, 2 * blk)
    c_lo = jnp.where(is_lat, c_lo, 3 * blk)
    c_hi = jnp.where(is_lat, c_hi, 0)
    row_lo = jnp.maximum(col - 2 * blk, 0)
    row_hi = jnp.where(col < blk, col, blk - 1)
    valid = ((col >= c_lo) & (col < c_hi) & (row >= row_lo) & (row <= row_hi)) | (col >= 3 * blk)
    heads = range(A_KV_HEADS)
    hs = lambda hk: slice(hk * HEAD_DIM, (hk + 1) * HEAD_DIM)
    qs = lambda hk, g: slice((hk * A_GROUP + g) * HEAD_DIM, (hk * A_GROUP + g + 1) * HEAD_DIM)
    scores = []
    for hk in heads:
        q = jnp.concatenate([q_ref[:, qs(hk, g)] for g in range(A_GROUP)], axis=0)
        k = jnp.concatenate([kl_ref[:, hs(hk)], km_ref[:, hs(hk)], kr_ref[:, hs(hk)], kc_ref[:, hs(hk)]], axis=0)
        scores.append(lax.dot_general(q, k, (((1,), (1,)), ((), ())), preferred_element_type=F32))
    probs, denoms = [], []
    for hk in heads:
        s = jnp.where(valid, scores[hk], NEG_INF)
        sink = sink_ref[hk]
        m = jnp.maximum(jnp.max(s, axis=-1, keepdims=True), sink)
        p = jnp.exp(s - m)
        denoms.append(jnp.sum(p, axis=-1, keepdims=True) + jnp.exp(sink - m))
        probs.append(p.astype(vm_ref.dtype))
    for hk in heads:
        v = jnp.concatenate([vl_ref[:, hs(hk)], vm_ref[:, hs(hk)], vr_ref[:, hs(hk)], vc_ref[:, hs(hk)]], axis=0)
        o = jnp.dot(probs[hk], v, preferred_element_type=F32) / denoms[hk]
        for g in range(A_GROUP):
            o_ref[:, qs(hk, g)] = o[g * blk:(g + 1) * blk].astype(o_ref.dtype)


def _attn_call(qn, kn, g1, sink_col, layer, ctx_rows):
    rows = qn.shape[0]
    blk = A_BLOCK
    nb = rows // blk
    cb = ctx_rows // blk
    qw = A_Q_HEADS * HEAD_DIM
    kvw = A_KV_HEADS * HEAD_DIM
    lo = lambda r: jnp.maximum(r - 1, 0)
    hi = lambda r: jnp.minimum(r + 1, nb - 1)
    ident = lambda r: r
    kspec = lambda f: pl.BlockSpec((blk, kvw), lambda r: (f(r), 0))
    vspec = lambda f: pl.BlockSpec((blk, kvw), lambda r: (f(r), 1))
    return pl.pallas_call(
        functools.partial(_attn_kernel, ctx_blocks=cb, n_blocks=nb),
        grid=(nb,),
        in_specs=[pl.BlockSpec((blk, qw), lambda r: (r, 0)),
                  kspec(lo), kspec(ident), kspec(hi),
                  pl.BlockSpec((ctx_rows, kvw), lambda r: (0, 0)),
                  vspec(lo), vspec(ident), vspec(hi),
                  pl.BlockSpec((ctx_rows, kvw), lambda r: (0, 1)),
                  pl.BlockSpec((None, A_KV_HEADS, A_GROUP * blk, 1), lambda r: (layer, 0, 0, 0))],
        out_specs=pl.BlockSpec((blk, qw), lambda r: (r, 0)),
        out_shape=jax.ShapeDtypeStruct((rows, qw), ACT_DTYPE),
        compiler_params=_params(32, 1),
        name="window_attn",
    )(qn, kn, kn, kn, kn, g1, g1, g1, g1, sink_col)


GMLP_CHUNKS_PER_STEP = 2


def _gmlp_kernel(u_ref, v_ref, bn_ref, ws_ref, bias_ref, o_ref):
    chunks = range(u_ref.shape[0] // CHUNK)
    rows = lambda c: slice(c * CHUNK, (c + 1) * CHUNK)
    for g in range(B_GROUPS):
        sl = slice(g * LANES, (g + 1) * LANES)
        normed = []
        for c in chunks:
            v = _gelu_tanh(v_ref[rows(c), sl].astype(F32))
            vc = v - jnp.mean(v, axis=-1, keepdims=True)
            vh = vc * lax.rsqrt(jnp.mean(vc * vc, axis=-1, keepdims=True) + EPS) * bn_ref[:, sl]
            normed.append(vh.astype(ws_ref.dtype))
        mixed = jnp.dot(ws_ref[g], jnp.concatenate(normed, axis=1), preferred_element_type=F32)
        for c in chunks:
            m = mixed[:, c * LANES:(c + 1) * LANES] + bias_ref[g]
            o_ref[rows(c), sl] = (_gelu_tanh(u_ref[rows(c), sl].astype(F32)) * m).astype(o_ref.dtype)


def _gmlp_call(g3, b_norm_flat, ws, bias_b, layer):
    rows = g3.shape[0]
    w = BRANCH_WIDTH
    tm = GMLP_CHUNKS_PER_STEP * CHUNK
    return pl.pallas_call(
        _gmlp_kernel,
        grid=(rows // tm,),
        in_specs=[pl.BlockSpec((tm, w), lambda i: (i, 3)),
                  pl.BlockSpec((tm, w), lambda i: (i, 4)),
                  pl.BlockSpec((None, 1, w), lambda i: (layer, 0, 0)),
                  pl.BlockSpec((None, B_GROUPS, CHUNK, CHUNK), lambda i: (layer, 0, 0, 0)),
                  pl.BlockSpec((None, B_GROUPS, CHUNK, LANES), lambda i: (layer, 0, 0, 0))],
        out_specs=pl.BlockSpec((tm, w), lambda i: (i, 0)),
        out_shape=jax.ShapeDtypeStruct((rows, w), ACT_DTYPE),
        compiler_params=_params(32, 1),
        name="chunk_gmlp",
    )(g3, g3, b_norm_flat, ws, bias_b)


def _ret_kernel(qf_ref, kf_ref, vf_ref, qb_ref, kb_ref, vb_ref, intra_ref, qdec_ref, kdec_ref, cdec_ref,
                of_ref, ob_ref, *state_refs):
    @pl.when(pl.program_id(0) == 0)
    def _():
        for s_ref in state_refs:
            s_ref[...] = jnp.zeros(s_ref.shape, s_ref.dtype)

    dirs = ((qf_ref, kf_ref, vf_ref, of_ref), (qb_ref, kb_ref, vb_ref, ob_ref))
    chains = [(d, h) for d in range(2) for h in range(C_HEADS)]
    head = lambda h: slice(h * HEAD_DIM, (h + 1) * HEAD_DIM)

    scores = []
    for d, h in chains:
        q_ref, k_ref, _, _ = dirs[d]
        q = q_ref[:, head(h)]
        a = lax.dot_general(q, k_ref[:, head(h)], (((1,), (1,)), ((), ())), preferred_element_type=F32)
        qd = (q.astype(F32) * qdec_ref[d, h]).astype(q.dtype)
        scores.append(jnp.concatenate([(a * intra_ref[d, h]).astype(q.dtype), qd], axis=1))
    for (d, h), lhs in zip(chains, scores):
        _, _, v_ref, o_ref = dirs[d]
        v = v_ref[:, head(h)]
        rhs = jnp.concatenate([v, state_refs[d * C_HEADS + h][...].astype(v.dtype)], axis=0)
        o_ref[:, head(h)] = jnp.dot(lhs, rhs, preferred_element_type=F32).astype(o_ref.dtype)
    for d, h in chains:
        _, k_ref, v_ref, _ = dirs[d]
        s_ref = state_refs[d * C_HEADS + h]
        k = k_ref[:, head(h)]
        kd = (k.astype(F32) * kdec_ref[d, h]).astype(k.dtype)
        upd = lax.dot_general(kd, v_ref[:, head(h)], (((0,), (0,)), ((), ())), preferred_element_type=F32)
        s_ref[...] = s_ref[...] * cdec_ref[d, h] + upd


def _ret_call(qr, kr, g2, tables, layer, ctx_rows):
    rows = qr.shape[0]
    w = BRANCH_WIDTH
    nc = rows // CHUNK
    cc = ctx_rows // CHUNK

    def bwd(s):
        return jnp.where(s < cc, cc - 1 - s, nc - 1 + cc - s)

    fq = pl.BlockSpec((CHUNK, w), lambda s: (s, 0))
    fv = pl.BlockSpec((CHUNK, w), lambda s: (s, 1))
    bq = pl.BlockSpec((CHUNK, w), lambda s: (bwd(s), 0))
    bv = pl.BlockSpec((CHUNK, w), lambda s: (bwd(s), 1))
    tab = pl.BlockSpec((None, 2, C_HEADS, CHUNK, LANES), lambda s: (layer, 0, 0, 0, 0))
    return pl.pallas_call(
        _ret_kernel,
        grid=(nc,),
        in_specs=[fq, fq, fv, bq, bq, bv, tab, tab, tab, tab],
        out_specs=[pl.BlockSpec((CHUNK, w), lambda s: (s, 0)),
                   pl.BlockSpec((CHUNK, w), lambda s: (bwd(s), 0))],
        out_shape=[jax.ShapeDtypeStruct((rows, w), F32), jax.ShapeDtypeStruct((rows, w), F32)],
        scratch_shapes=[pltpu.VMEM((HEAD_DIM, HEAD_DIM), F32) for _ in range(2 * C_HEADS)],
        compiler_params=_params(32, 1),
        name="retention",
    )(qr, kr, g2, qr, kr, g2, *tables)


def _merge_kernel(attn_ref, gm_ref, of_ref, ob_ref, rg_ref, gate_ref, cn_ref, wb_ref, o_ref):
    d = o_ref.shape[1]
    ret_parts = []
    for h in range(C_HEADS):
        sl = slice(h * HEAD_DIM, (h + 1) * HEAD_DIM)
        o = of_ref[:, sl] + ob_ref[:, sl]
        oc = o - jnp.mean(o, axis=-1, keepdims=True)
        y = oc * lax.rsqrt(jnp.mean(oc * oc, axis=-1, keepdims=True) + EPS) * cn_ref[:, sl]
        ret_parts.append((_silu(rg_ref[:, sl].astype(F32)) * y).astype(wb_ref.dtype))
    ret = jnp.concatenate(ret_parts, axis=1)
    branches = (attn_ref[...].astype(wb_ref.dtype), gm_ref[...].astype(wb_ref.dtype), ret)
    acc = None
    for b in range(N_BRANCH):
        proj = jnp.dot(branches[b], wb_ref[b], preferred_element_type=F32)
        term = _sigmoid(gate_ref[:, b * d:(b + 1) * d].astype(F32)) * proj
        acc = term if acc is None else acc + term
    o_ref[...] = acc.astype(o_ref.dtype)


def _merge_call(attn, gm, o_f, o_b, g3, g4, c_norm_flat, wb, layer):
    rows = attn.shape[0]
    w = BRANCH_WIDTH
    d = wb.shape[-1]
    tm = 256
    row = lambda c: pl.BlockSpec((tm, w), lambda i: (i, c))
    return pl.pallas_call(
        _merge_kernel,
        grid=(rows // tm,),
        in_specs=[row(0), row(0), row(0), row(0), row(2),
                  pl.BlockSpec((tm, N_BRANCH * d), lambda i: (i, 0)),
                  pl.BlockSpec((None, 1, w), lambda i: (layer, 0, 0)),
                  pl.BlockSpec((None, N_BRANCH, w, d), lambda i: (layer, 0, 0, 0))],
        out_specs=pl.BlockSpec((tm, d), lambda i: (i, 0)),
        out_shape=jax.ShapeDtypeStruct((rows, d), ACT_DTYPE),
        compiler_params=_params(52, 1),
        name="branch_merge",
    )(attn, gm, o_f, o_b, g3, g4, c_norm_flat, wb)


def _route_kernel(m_ref, wo_ref, z_ref, g_ref, mod_ref, whi_ref, wlo_ref, rb_ref,
                  znew_ref, h_ref, info_ref, cnt_ref, carry_ref):
    i = pl.program_id(0)

    @pl.when(i == 0)
    def _():
        carry_ref[...] = jnp.zeros(carry_ref.shape, carry_ref.dtype)

    z = z_ref[...] + mod_ref[2:3, :] * jnp.dot(m_ref[...], wo_ref[...], preferred_element_type=F32)
    znew_ref[...] = z
    h = _norm_mod(z, g_ref[...], mod_ref[...], 3, 4)
    h_ref[...] = h
    h_hi = h.astype(whi_ref.dtype)
    h_lo = (h - h_hi.astype(F32)).astype(whi_ref.dtype)
    logits = (jnp.dot(h_hi, whi_ref[...], preferred_element_type=F32)
              + jnp.dot(h_hi, wlo_ref[...], preferred_element_type=F32)
              + jnp.dot(h_lo, whi_ref[...], preferred_element_type=F32)) + rb_ref[...]
    tm = logits.shape[0]
    lane = lax.broadcasted_iota(jnp.int32, logits.shape, 1).astype(F32)
    first = lambda hit: jnp.min(jnp.where(hit, lane, 4.0 * LANES), axis=-1, keepdims=True)

    is_g = lane < N_GROUPS
    gl = jnp.where(is_g, logits, NEG_INF)
    gmax = jnp.max(gl, axis=-1, keepdims=True)
    g_sel = first(gl == gmax)
    g_w = 1.0 / jnp.sum(jnp.where(is_g, jnp.exp(gl - gmax), 0.0), axis=-1, keepdims=True)

    e_id = lane - N_GROUPS
    in_group = (e_id >= g_sel * EXPERTS_PER_GROUP) & (e_id < (g_sel + 1.0) * EXPERTS_PER_GROUP)
    el = jnp.where(in_group, logits, NEG_INF)
    m1 = jnp.max(el, axis=-1, keepdims=True)
    i1 = first(el == m1)
    el2 = jnp.where(lane == i1, NEG_INF, el)
    m2 = jnp.max(el2, axis=-1, keepdims=True)
    i2 = first(el2 == m2)
    r = jnp.exp(m2 - m1)
    w1 = g_w / (1.0 + r)
    w2 = g_w * r / (1.0 + r)
    e1 = i1 - N_GROUPS
    e2 = i2 - N_GROUPS

    hot1 = lane == e1
    hot2 = lane == e2
    hot = jnp.where(hot1 | hot2, 1.0, 0.0)
    rr = lax.broadcasted_iota(jnp.int32, (tm, tm), 0)
    cc = lax.broadcasted_iota(jnp.int32, (tm, tm), 1)
    tri = jnp.where(cc < rr, 1.0, 0.0).astype(MXU_DTYPE)
    before = jnp.dot(tri, hot.astype(MXU_DTYPE), preferred_element_type=F32) + carry_ref[0:1, :]
    rank1 = jnp.sum(jnp.where(hot1, before, 0.0), axis=-1, keepdims=True)
    rank2 = jnp.sum(jnp.where(hot2, before, 0.0), axis=-1, keepdims=True)
    carry_ref[0:1, :] = carry_ref[0:1, :] + jnp.sum(hot, axis=0, keepdims=True)
    cnt_ref[...] = carry_ref[...]

    info = jnp.where(lane == 0, e1, 0.0)
    info = jnp.where(lane == 1, e2, info)
    info = jnp.where(lane == 2, w1, info)
    info = jnp.where(lane == 3, w2, info)
    info = jnp.where(lane == 4, rank1, info)
    info = jnp.where(lane == 5, rank2, info)
    info_ref[...] = info


def _route_call(merged, wo, z, norm_w, modsel, w_hi, w_lo, rbias, layer, ctx_rows):
    rows, d = z.shape
    tm = 256
    ctx_tiles = ctx_rows // tm
    return pl.pallas_call(
        _route_kernel,
        grid=(rows // tm,),
        in_specs=[pl.BlockSpec((tm, d), lambda i: (i, 0)),
                  pl.BlockSpec((None, d, d), lambda i: (layer, 0, 0)),
                  pl.BlockSpec((tm, d), lambda i: (i, 0)),
                  pl.BlockSpec((None, 1, d), lambda i: (layer, 0, 0)),
                  pl.BlockSpec((None, None, MOD_ROWS, d),
                               lambda i: (layer, jnp.where(i >= ctx_tiles, 1, 0), 0, 0)),
                  pl.BlockSpec((None, d, LANES), lambda i: (layer, 0, 0)),
                  pl.BlockSpec((None, d, LANES), lambda i: (layer, 0, 0)),
                  pl.BlockSpec((None, 1, LANES), lambda i: (layer, 0, 0))],
        out_specs=[pl.BlockSpec((tm, d), lambda i: (i, 0)),
                   pl.BlockSpec((tm, d), lambda i: (i, 0)),
                   pl.BlockSpec((tm, LANES), lambda i: (i, 0)),
                   pl.BlockSpec((8, LANES), lambda i: (0, 0))],
        out_shape=[jax.ShapeDtypeStruct((rows, d), F32),
                   jax.ShapeDtypeStruct((rows, d), F32),
                   jax.ShapeDtypeStruct((rows, LANES), F32),
                   jax.ShapeDtypeStruct((8, LANES), F32)],
        scratch_shapes=[pltpu.VMEM((8, LANES), F32)],
        compiler_params=_params(48, 1),
        name="outproj_route",
    )(merged, wo, z, norm_w, modsel, w_hi, w_lo, rbias)


def _dispatch_kernel(pad_end_ref, padded_ref, n_used_ref, dest_ref, h_ref, xs_ref, zero_ref, sem, zero_sem):
    tm = h_ref.shape[0]
    n_blocks = xs_ref.shape[0] // MOE_BLOCK

    @pl.when(pl.program_id(0) == 0)
    def _():
        zero_ref[...] = jnp.zeros(zero_ref.shape, zero_ref.dtype)

        def block_copy(first):
            first = pl.multiple_of(first, MOE_BLOCK)
            return pltpu.make_async_copy(zero_ref, xs_ref.at[pl.ds(first, MOE_BLOCK)], zero_sem)

        def start_unused(b, carry):
            block_copy(b * MOE_BLOCK).start()
            return carry

        def wait_unused(b, carry):
            block_copy(b * MOE_BLOCK).wait()
            return carry

        for e in range(N_EXPERTS):
            @pl.when(padded_ref[e] > 0)
            def _(e=e):
                block_copy(pad_end_ref[e] - MOE_BLOCK).start()
        lax.fori_loop(n_used_ref[0], n_blocks, start_unused, 0)
        for e in range(N_EXPERTS):
            @pl.when(padded_ref[e] > 0)
            def _(e=e):
                block_copy(pad_end_ref[e] - MOE_BLOCK).wait()
        lax.fori_loop(n_used_ref[0], n_blocks, wait_unused, 0)

    def row_copy(t, slot):
        return pltpu.make_async_copy(h_ref.at[pl.ds(t, 1)], xs_ref.at[pl.ds(slot, 1)], sem)

    def start(t, carry):
        row_copy(t, dest_ref[0, 0, 2 * t]).start()
        row_copy(t, dest_ref[0, 0, 2 * t + 1]).start()
        return carry

    lax.fori_loop(0, tm, start, 0, unroll=4)
    all_rows = pltpu.make_async_copy(h_ref, xs_ref.at[pl.ds(0, tm)], sem)
    all_rows.wait()
    all_rows.wait()


def _dispatch_call(pad_end, padded, n_used, dest3, h, n_slots):
    rows, d = h.shape
    tm = dest3.shape[2] // 2
    grid_spec = pltpu.PrefetchScalarGridSpec(
        num_scalar_prefetch=3,
        grid=(rows // tm,),
        in_specs=[pl.BlockSpec((1, 1, 2 * tm), lambda i, pe, pd, nu: (i, 0, 0), memory_space=pltpu.SMEM),
                  pl.BlockSpec((tm, d), lambda i, pe, pd, nu: (i, 0))],
        out_specs=pl.BlockSpec(memory_space=pl.ANY),
        scratch_shapes=[pltpu.VMEM((MOE_BLOCK, d), h.dtype),
                        pltpu.SemaphoreType.DMA(()), pltpu.SemaphoreType.DMA(())])
    return pl.pallas_call(
        _dispatch_kernel,
        grid_spec=grid_spec,
        out_shape=jax.ShapeDtypeStruct((n_slots, d), h.dtype),
        compiler_params=_params(32, 1),
        name="moe_dispatch",
    )(pad_end, padded, n_used, dest3, h)


def _expert_kernel(be_ref, nu_ref, first_ref, slot_ref, next_ref, xs_ref, w1_hbm, w2_hbm, ys_ref,
                   w1f_ref, w2f_ref, w1bf_ref, w2bf_ref, sem, *, layer):
    i = pl.program_id(0)
    used = i < nu_ref[0]

    def fetch(e, s):
        return (pltpu.make_async_copy(w1_hbm.at[layer, e], w1f_ref.at[s], sem.at[s]),
                pltpu.make_async_copy(w2_hbm.at[layer, e], w2f_ref.at[s], sem.at[s]))

    @pl.when(i == 0)
    def _():
        for cp in fetch(be_ref[0], slot_ref[0]):
            cp.start()

    @pl.when(jnp.logical_and(used, first_ref[i] == 1))
    def _():
        s = slot_ref[i]
        for cp in fetch(be_ref[i], s):
            cp.wait()

        @pl.when(next_ref[i] >= 0)
        def _():
            for cp in fetch(next_ref[i], 1 - s):
                cp.start()

        w1bf_ref[...] = w1f_ref[s].astype(w1bf_ref.dtype)
        w2bf_ref[...] = w2f_ref[s].astype(w2bf_ref.dtype)

    @pl.when(used)
    def _():
        de = w2bf_ref.shape[0]
        hcat = jnp.dot(xs_ref[...].astype(w1bf_ref.dtype), w1bf_ref[...], preferred_element_type=F32)
        act = _silu(hcat[:, :de]) * hcat[:, de:]
        ys_ref[...] = jnp.dot(act.astype(w2bf_ref.dtype), w2bf_ref[...], preferred_element_type=F32)

    @pl.when(jnp.logical_not(used))
    def _():
        ys_ref[...] = jnp.zeros(ys_ref.shape, ys_ref.dtype)


def _expert_call(plan, xs, w_e1, w_e2, layer):
    slots, d = xs.shape
    de = w_e2.shape[2]
    nb = slots // MOE_BLOCK
    blk = lambda i, be, nu, *_: (jnp.minimum(i, nu[0] - 1), 0)
    grid_spec = pltpu.PrefetchScalarGridSpec(
        num_scalar_prefetch=5,
        grid=(nb,),
        in_specs=[pl.BlockSpec((MOE_BLOCK, d), blk),
                  pl.BlockSpec(memory_space=pl.ANY),
                  pl.BlockSpec(memory_space=pl.ANY)],
        out_specs=pl.BlockSpec((MOE_BLOCK, d), lambda i, *_: (i, 0)),
        scratch_shapes=[pltpu.VMEM((2, d, 2 * de), w_e1.dtype), pltpu.VMEM((2, de, d), w_e2.dtype),
                        pltpu.VMEM((d, 2 * de), MXU_DTYPE), pltpu.VMEM((de, d), MXU_DTYPE),
                        pltpu.SemaphoreType.DMA((2,))])
    return pl.pallas_call(
        functools.partial(_expert_kernel, layer=layer),
        grid_spec=grid_spec,
        out_shape=jax.ShapeDtypeStruct((slots, d), F32),
        compiler_params=_params(48, 1),
        name="moe_experts",
    )(plan["block_e"], plan["n_used"], plan["first"], plan["slot"], plan["next_e"], xs, w_e1, w_e2)


def _combine_kernel(dest_ref, dest_next_ref, ys_ref, info_ref, z_ref, mod_ref, *rest, emit_next):
    if emit_next:
        gn_ref, modn_ref, o_ref, hn_ref, buf_ref, sem = rest
    else:
        o_ref, buf_ref, sem = rest
    i = pl.program_id(0)
    tm = z_ref.shape[0]
    slot = i % 2

    def issue(d_ref, s):
        def start(t, carry):
            for k in range(2):
                pltpu.make_async_copy(ys_ref.at[pl.ds(d_ref[0, 0, 2 * t + k], 1)],
                                      buf_ref.at[s, k, pl.ds(t, 1)], sem.at[s]).start()
            return carry

        lax.fori_loop(0, tm, start, 0, unroll=4)

    @pl.when(i == 0)
    def _():
        issue(dest_ref, 0)

    @pl.when(i + 1 < pl.num_programs(0))
    def _():
        issue(dest_next_ref, 1 - slot)

    for k in range(2):
        pltpu.make_async_copy(ys_ref.at[pl.ds(0, tm)], buf_ref.at[slot, k], sem.at[slot]).wait()
    info = info_ref[...]
    lane = lax.broadcasted_iota(jnp.int32, info.shape, 1)
    w1 = jnp.sum(jnp.where(lane == 2, info, 0.0), axis=-1, keepdims=True)
    w2 = jnp.sum(jnp.where(lane == 3, info, 0.0), axis=-1, keepdims=True)
    y = buf_ref[slot, 0] * w1 + buf_ref[slot, 1] * w2
    z = z_ref[...] + mod_ref[5:6, :] * y
    o_ref[...] = z
    if emit_next:
        hn_ref[...] = _norm_mod(z, gn_ref[...], modn_ref[...], 0, 1).astype(hn_ref.dtype)


def _combine_call(dest3, ys, info, z, modsel, layer, ctx_rows, next_norm_w=None):
    rows, d = z.shape
    tm = dest3.shape[2] // 2
    ctx_tiles = ctx_rows // tm
    n_tiles = rows // tm
    emit_next = next_norm_w is not None
    mod_spec = lambda l: pl.BlockSpec((None, None, MOD_ROWS, d),
                                      lambda i: (l, jnp.where(i >= ctx_tiles, 1, 0), 0, 0))
    row_spec = pl.BlockSpec((tm, d), lambda i: (i, 0))
    in_specs = [pl.BlockSpec((1, 1, 2 * tm), lambda i: (i, 0, 0), memory_space=pltpu.SMEM),
                pl.BlockSpec((1, 1, 2 * tm), lambda i: (jnp.minimum(i + 1, n_tiles - 1), 0, 0),
                             memory_space=pltpu.SMEM),
                pl.BlockSpec(memory_space=pl.ANY),
                pl.BlockSpec((tm, LANES), lambda i: (i, 0)),
                row_spec,
                mod_spec(layer)]
    args = [dest3, dest3, ys, info, z, modsel]
    if emit_next:
        out_specs = [row_spec]
        out_shape = [jax.ShapeDtypeStruct((rows, d), F32)]
    else:
        out_specs = [pl.BlockSpec((tm, d), lambda i: (jnp.maximum(i - ctx_tiles, 0), 0))]
        out_shape = [jax.ShapeDtypeStruct((rows - ctx_rows, d), F32)]
    if emit_next:
        in_specs += [pl.BlockSpec((None, 1, d), lambda i: (layer + 1, 0, 0)), mod_spec(layer + 1)]
        args += [next_norm_w, modsel]
        out_specs.append(row_spec)
        out_shape.append(jax.ShapeDtypeStruct((rows, d), ACT_DTYPE))
    return pl.pallas_call(
        functools.partial(_combine_kernel, emit_next=emit_next),
        grid=(n_tiles,),
        in_specs=in_specs,
        out_specs=out_specs,
        out_shape=out_shape,
        scratch_shapes=[pltpu.VMEM((2, 2, tm, d), F32), pltpu.SemaphoreType.DMA((2,))],
        compiler_params=_params(40, 1),
        name="moe_combine",
    )(*args)


def _rope_tables(n, ctx_rows):
    rows = n // GRID_W
    row = jnp.repeat(jnp.arange(rows, dtype=F32), GRID_W)
    col = jnp.tile(jnp.arange(GRID_W, dtype=F32), rows)
    nq = HEAD_DIM // 4
    inv = ROPE_BASE ** (-jnp.arange(nq, dtype=F32) / nq)
    ar, ac = row[:, None] * inv, col[:, None] * inv
    cos = jnp.concatenate([jnp.cos(ar), jnp.cos(ar), jnp.cos(ac), jnp.cos(ac)], axis=1)
    sin = jnp.concatenate([-jnp.sin(ar), jnp.sin(ar), -jnp.sin(ac), jnp.sin(ac)], axis=1)
    cos = jnp.concatenate([jnp.ones((ctx_rows, HEAD_DIM), F32), cos], axis=0)
    sin = jnp.concatenate([jnp.zeros((ctx_rows, HEAD_DIM), F32), sin], axis=0)
    return cos, sin


def _retention_tables(c_decay_fwd, c_decay_bwd):
    lg_f = jax.nn.log_sigmoid(c_decay_fwd.astype(F32))[:, :, None, None]
    lg_b = jax.nn.log_sigmoid(c_decay_bwd.astype(F32))[:, :, None, None]
    idx = jnp.arange(CHUNK, dtype=F32)
    diff = idx[:, None] - idx[None, :]
    ones = jnp.ones((CHUNK, CHUNK), F32)
    t_col = idx[:, None] * ones
    intra_f = jnp.where(diff >= 0, jnp.exp(lg_f * jnp.maximum(diff, 0.0)), 0.0)
    intra_b = jnp.where(diff <= 0, jnp.exp(lg_b * jnp.maximum(-diff, 0.0)), 0.0)
    qdec_f = jnp.exp(lg_f * (t_col + 1.0))
    qdec_b = jnp.exp(lg_b * (CHUNK - t_col))
    kdec_f = jnp.exp(lg_f * (CHUNK - 1.0 - t_col))
    kdec_b = jnp.exp(lg_b * t_col)
    cdec_f = jnp.exp(lg_f * CHUNK) * ones
    cdec_b = jnp.exp(lg_b * CHUNK) * ones
    pair = lambda a, b: jnp.stack([a, b], axis=1)
    return (pair(intra_f, intra_b), pair(qdec_f, qdec_b), pair(kdec_f, kdec_b), pair(cdec_f, cdec_b))


def _moe_plan(info, counts_row, n_slots_blocks):
    e = info[:, 0:2].astype(jnp.int32)
    rank = info[:, 4:6].astype(jnp.int32)
    counts = counts_row[:N_EXPERTS].astype(jnp.int32)
    padded = (counts + MOE_BLOCK - 1) // MOE_BLOCK * MOE_BLOCK
    pad_end = jnp.cumsum(padded)
    pad_start = pad_end - padded
    hit = e[:, :, None] == jnp.arange(N_EXPERTS, dtype=jnp.int32)
    dest = jnp.sum(jnp.where(hit, pad_start, 0), axis=-1) + rank
    n_used = pad_end[-1] // MOE_BLOCK
    blocks = jnp.arange(n_slots_blocks, dtype=jnp.int32)
    first_slot = jnp.minimum(blocks, n_used - 1) * MOE_BLOCK
    block_e = jnp.sum((pad_end[None, :] <= first_slot[:, None]).astype(jnp.int32), axis=1)
    ids = jnp.arange(N_EXPERTS, dtype=jnp.int32)
    nonempty = counts > 0
    slot_of = (jnp.cumsum(nonempty.astype(jnp.int32)) - 1) % 2
    later = nonempty[None, :] & (ids[None, :] > ids[:, None])
    next_of = jnp.min(jnp.where(later, ids[None, :], N_EXPERTS), axis=1)
    next_of = jnp.where(next_of == N_EXPERTS, -1, next_of)
    pick = lambda table: jnp.sum(jnp.where(block_e[:, None] == ids[None, :], table[None, :], 0), axis=1)
    prev_e = jnp.concatenate([jnp.full((1,), -1, jnp.int32), block_e[:-1]])
    first = ((block_e != prev_e) & (blocks < n_used)).astype(jnp.int32)
    i32 = lambda a: a.astype(jnp.int32)
    return dict(dest=dest, block_e=i32(block_e), n_used=i32(n_used.reshape(1)), pad_end=i32(pad_end),
                padded=i32(padded), first=first, slot=i32(pick(slot_of)), next_e=i32(pick(next_of)))


def kernel(x, c, ctx, c_ctx, norm_mix, norm_ffn, w_ada, b_ada, w_in, a_q_norm, a_k_norm, a_sink,
           b_norm, b_spatial, b_spatial_bias, c_decay_fwd, c_decay_bwd, c_norm, w_branch, w_out,
           w_router_group, b_router_group, w_router_expert, b_router_expert, w_expert_in, w_expert_out):
    batch, n, d = x.shape
    ctx_rows = ctx.shape[1]
    depth = w_in.shape[0]
    assert batch == 1 and ctx_rows % 256 == 0 and n % 256 == 0
    rows = ctx_rows + n

    cond = jnp.stack([c[0], c_ctx], axis=0)
    cond_b = jnp.broadcast_to(cond[:, :, None], (2, d, LANES))
    mod = _ada_call(cond_b, w_ada, b_ada)[:, :2].reshape(depth, 2, N_MOD, d)
    modsel = jnp.pad(mod[:, ::-1], ((0, 0), (0, 0), (0, MOD_ROWS - N_MOD), (0, 0)))

    cos_t, sin_t = _rope_tables(n, ctx_rows)
    ret_tables = _retention_tables(c_decay_fwd, c_decay_bwd)
    sink_col = jnp.broadcast_to(a_sink.astype(F32).reshape(depth, A_KV_HEADS, A_GROUP, 1, 1),
                                (depth, A_KV_HEADS, A_GROUP, A_BLOCK, 1)).reshape(depth, A_KV_HEADS, A_GROUP * A_BLOCK, 1)
    ws = b_spatial.astype(MXU_DTYPE)
    bias_b = jnp.broadcast_to(b_spatial_bias.astype(F32)[:, :, :, None], (depth, B_GROUPS, CHUNK, LANES))
    wb = w_branch.astype(MXU_DTYPE)
    wo = w_out.astype(MXU_DTYPE)
    w_r = jnp.concatenate([w_router_group, w_router_expert], axis=-1).astype(F32)
    w_r = jnp.pad(w_r, ((0, 0), (0, 0), (0, LANES - w_r.shape[-1])))
    w_r_hi = w_r.astype(MXU_DTYPE)
    w_r_lo = (w_r - w_r_hi.astype(F32)).astype(MXU_DTYPE)
    b_r = jnp.concatenate([b_router_group, b_router_expert], axis=-1).astype(F32)
    b_r = jnp.pad(b_r, ((0, 0), (0, LANES - b_r.shape[-1]))).reshape(depth, 1, LANES)
    norm_mix3 = norm_mix.reshape(depth, 1, d)
    norm_ffn3 = norm_ffn.reshape(depth, 1, d)
    lane_ids = jnp.arange(HEAD_DIM)
    partner = jnp.where((lane_ids // 32) % 2 == 0, lane_ids + 32, lane_ids - 32)
    perm = (lane_ids[:, None] == partner[None, :]).astype(ACT_DTYPE)
    a_q_norm3 = jnp.stack([a_q_norm, a_q_norm[:, partner]], axis=1).astype(F32)
    a_k_norm3 = jnp.stack([a_k_norm, a_k_norm[:, partner]], axis=1).astype(F32)
    b_norm3 = b_norm.reshape(depth, 1, BRANCH_WIDTH)
    c_norm3 = c_norm.reshape(depth, 1, BRANCH_WIDTH)

    kvw = A_KV_HEADS * HEAD_DIM
    w = BRANCH_WIDTH
    n_assign = rows * 2
    n_slot_blocks = -(-(n_assign + N_EXPERTS * (MOE_BLOCK - 1)) // MOE_BLOCK)
    tok_tile = 256
    disp_tile = _row_tile(rows, (1056, 768, 256))

    z = jnp.concatenate([ctx[0], x[0]], axis=0)
    h = _norm_mod_call(z, norm_mix3, modsel, 0, ctx_rows)
    for l in range(depth):
        g1 = _proj_call(h, w_in, l, 0, 2 * kvw)
        g2 = _proj_call(h, w_in, l, 2 * kvw, 2 * w)
        g3 = _proj_call(h, w_in, l, 2 * kvw + 2 * w, 5 * w)
        g4 = _proj_call(h, w_in, l, 2 * kvw + 7 * w, N_BRANCH * d)
        kn, qn, kr, qr = _prep_call(g1, g2, g3, cos_t, sin_t, a_q_norm3, a_k_norm3, perm, l)
        attn = _attn_call(qn, kn, g1, sink_col, l, ctx_rows)
        gm = _gmlp_call(g3, b_norm3, ws, bias_b, l)
        o_f, o_b = _ret_call(qr, kr, g2, ret_tables, l, ctx_rows)
        merged = _merge_call(attn, gm, o_f, o_b, g3, g4, c_norm3, wb, l)
        z, h2, info, counts = _route_call(merged, wo, z, norm_ffn3, modsel, w_r_hi, w_r_lo, b_r, l, ctx_rows)
        plan = _moe_plan(info, counts[0], n_slot_blocks)
        dest3 = plan["dest"].reshape(rows // tok_tile, 1, 2 * tok_tile)
        dest3_disp = plan["dest"].reshape(rows // disp_tile, 1, 2 * disp_tile)
        xs = _dispatch_call(plan["pad_end"], plan["padded"], plan["n_used"], dest3_disp, h2,
                            n_slot_blocks * MOE_BLOCK)
        ys = _expert_call(plan, xs, w_expert_in, w_expert_out, l)
        if l + 1 < depth:
            z, h = _combine_call(dest3, ys, info, z, modsel, l, ctx_rows, next_norm_w=norm_mix3)
        else:
            (z_latent,) = _combine_call(dest3, ys, info, z, modsel, l, ctx_rows)
    return z_latent[None]
```

```python
import functools
import math

import jax
import jax.numpy as jnp
from jax import lax
from jax.experimental import pallas as pl
from jax.experimental.pallas import tpu as pltpu

F32 = jnp.float32
MXU_DTYPE = jnp.bfloat16
ACT_DTYPE = jnp.bfloat16

LANES = 128
HEAD_DIM = 128
GRID_W = 64
ROPE_BASE = 10000.0
EPS = 1e-6
NEG_INF = -1e30
A_Q_HEADS = 8
A_KV_HEADS = 2
A_GROUP = A_Q_HEADS // A_KV_HEADS
A_BLOCK = 128
B_GROUPS = 8
C_HEADS = 8
CHUNK = 128
N_GROUPS = 4
EXPERTS_PER_GROUP = 8
N_EXPERTS = N_GROUPS * EXPERTS_PER_GROUP
D_EXPERT = 512
BRANCH_WIDTH = 1024
N_BRANCH = 3
MOE_BLOCK = 256
N_MOD = 6
MOD_ROWS = 8
MIB = 1024 * 1024


def _params(vmem_mib, n_grid, **kw):
    return pltpu.CompilerParams(dimension_semantics=("arbitrary",) * n_grid,
                                vmem_limit_bytes=vmem_mib * MIB, **kw)


def _sigmoid(x):
    return 0.5 + 0.5 * jnp.tanh(0.5 * x)


def _silu(x):
    return x * _sigmoid(x)


PACK_DTYPE = jnp.uint32


def _pack_halves(x):
    n = x.shape[1] // 2
    rounded = lambda v: lax.bitcast_convert_type(v.astype(jnp.bfloat16).astype(F32), PACK_DTYPE)
    return rounded(x[:, n:]) | (rounded(x[:, :n]) >> 16)


def _unpack_halves(p):
    lo = lax.bitcast_convert_type(p << 16, F32)
    hi = lax.bitcast_convert_type(p & jnp.asarray(0xFFFF0000, PACK_DTYPE), F32)
    return lo, hi


def _gelu_tanh(x):
    return 0.5 * x * (1.0 + jnp.tanh(math.sqrt(2.0 / math.pi) * (x + 0.044715 * (x * x * x))))


def _ada_kernel(c_ref, w_ref, b_ref, o_ref):
    tn = w_ref.shape[1]
    s0 = _silu(c_ref[0])
    s1 = _silu(c_ref[1])
    o_ref[...] = jnp.zeros(o_ref.shape, o_ref.dtype)
    for j in range(tn // LANES):
        sl = slice(j * LANES, (j + 1) * LANES)
        wj = w_ref[:, sl]
        o_ref[0:1, sl] = jnp.sum(wj * s0, axis=0, keepdims=True) + b_ref[:, sl]
        o_ref[1:2, sl] = jnp.sum(wj * s1, axis=0, keepdims=True) + b_ref[:, sl]


def _ada_call(cond_b, w_ada, b_ada):
    depth, k, n = w_ada.shape
    tn = 1024
    return pl.pallas_call(
        _ada_kernel,
        grid=(depth, n // tn),
        in_specs=[pl.BlockSpec((2, k, LANES), lambda l, j: (0, 0, 0)),
                  pl.BlockSpec((None, k, tn), lambda l, j: (l, 0, j)),
                  pl.BlockSpec((None, 1, tn), lambda l, j: (l, 0, j))],
        out_specs=pl.BlockSpec((None, 8, tn), lambda l, j: (l, 0, j)),
        out_shape=jax.ShapeDtypeStruct((depth, 8, n), F32),
        compiler_params=_params(40, 2),
        name="adaln",
    )(cond_b, w_ada, b_ada.reshape(depth, 1, n))


def _norm_mod(z, g, mod, shift_row, scale_row):
    r = lax.rsqrt(jnp.mean(z * z, axis=-1, keepdims=True) + EPS)
    return (z * r * g) * (1.0 + mod[scale_row:scale_row + 1, :]) + mod[shift_row:shift_row + 1, :]


def _norm_mod_kernel(z_ref, g_ref, mod_ref, o_ref):
    o_ref[...] = _norm_mod(z_ref[...], g_ref[...], mod_ref[...], 0, 1).astype(o_ref.dtype)


def _norm_mod_call(z, norm_w, modsel, layer, ctx_rows):
    rows, d = z.shape
    tm = 256
    ctx_tiles = ctx_rows // tm
    return pl.pallas_call(
        _norm_mod_kernel,
        grid=(rows // tm,),
        in_specs=[pl.BlockSpec((tm, d), lambda i: (i, 0)),
                  pl.BlockSpec((None, 1, d), lambda i: (layer, 0, 0)),
                  pl.BlockSpec((None, None, MOD_ROWS, d),
                               lambda i: (layer, jnp.where(i >= ctx_tiles, 1, 0), 0, 0))],
        out_specs=pl.BlockSpec((tm, d), lambda i: (i, 0)),
        out_shape=jax.ShapeDtypeStruct((rows, d), ACT_DTYPE),
        compiler_params=_params(32, 1),
        name="norm_mod",
    )(z, norm_w, modsel)


def _proj_kernel(h_ref, w_ref, o_ref, wbf_ref):
    @pl.when(pl.program_id(1) == 0)
    def _():
        wbf_ref[...] = w_ref[...].astype(wbf_ref.dtype)

    o_ref[...] = jnp.dot(h_ref[...], wbf_ref[...], preferred_element_type=F32).astype(o_ref.dtype)


def _row_tile(rows, pref):
    for t in pref:
        if rows % t == 0:
            return t
    raise ValueError(f"no row tile for {rows}")


def _proj_call(h, w, layer, col_off, ncols):
    rows, k = h.shape
    tn = next(t for t in (1536, 1280, 1024, 512) if col_off % t == 0 and ncols % t == 0)
    tm = _row_tile(rows, (768, 512, 256))
    off = col_off // tn
    return pl.pallas_call(
        _proj_kernel,
        grid=(ncols // tn, rows // tm),
        in_specs=[pl.BlockSpec((tm, k), lambda j, i: (i, 0)),
                  pl.BlockSpec((None, k, tn), lambda j, i: (layer, 0, off + j))],
        out_specs=pl.BlockSpec((tm, tn), lambda j, i: (i, j)),
        out_shape=jax.ShapeDtypeStruct((rows, ncols), ACT_DTYPE),
        scratch_shapes=[pltpu.VMEM((k, tn), MXU_DTYPE)],
        compiler_params=_params(52, 2),
        name="proj_in",
    )(h, w)


def _prep_kernel(ak_ref, ck_ref, aq_ref, cq_ref, cos_ref, sin_ref, qn_ref, kn_ref, perm_ref,
                 okn_ref, oqn_ref, okr_ref, oqr_ref):
    cos = cos_ref[...]
    sin = sin_ref[...]
    perm = perm_ref[...]
    scale = HEAD_DIM ** -0.5

    def swap(x):
        return jnp.dot(x, perm, preferred_element_type=F32)

    def norm_rope(x, g_ref, out_scale):
        xf = x.astype(F32)
        r = lax.rsqrt(jnp.mean(xf * xf, axis=-1, keepdims=True) + EPS) * out_scale
        return (xf * (g_ref[0:1, :] * cos) + swap(x) * (g_ref[1:2, :] * sin)) * r

    for h in range(A_KV_HEADS):
        sl = slice(h * HEAD_DIM, (h + 1) * HEAD_DIM)
        okn_ref[:, sl] = norm_rope(ak_ref[:, sl], kn_ref, 1.0).astype(okn_ref.dtype)
    for h in range(A_Q_HEADS):
        sl = slice(h * HEAD_DIM, (h + 1) * HEAD_DIM)
        oqn_ref[:, sl] = norm_rope(aq_ref[:, sl], qn_ref, scale).astype(oqn_ref.dtype)
    for h in range(C_HEADS):
        sl = slice(h * HEAD_DIM, (h + 1) * HEAD_DIM)
        k = ck_ref[:, sl]
        q = cq_ref[:, sl]
        okr_ref[:, sl] = ((k.astype(F32) * cos + swap(k) * sin) * scale).astype(okr_ref.dtype)
        oqr_ref[:, sl] = (q.astype(F32) * cos + swap(q) * sin).astype(oqr_ref.dtype)


def _prep_call(g1, g2, g3, cos_t, sin_t, a_q_norm, a_k_norm, perm, layer):
    rows = g1.shape[0]
    tm = _row_tile(rows, (768, 512, 256))
    w = BRANCH_WIDTH
    kvw = A_KV_HEADS * HEAD_DIM
    return pl.pallas_call(
        _prep_kernel,
        grid=(rows // tm,),
        in_specs=[pl.BlockSpec((tm, kvw), lambda i: (i, 0)),
                  pl.BlockSpec((tm, w), lambda i: (i, 0)),
                  pl.BlockSpec((tm, w), lambda i: (i, 0)),
                  pl.BlockSpec((tm, w), lambda i: (i, 1)),
                  pl.BlockSpec((tm, LANES), lambda i: (i, 0)),
                  pl.BlockSpec((tm, LANES), lambda i: (i, 0)),
                  pl.BlockSpec((None, 2, HEAD_DIM), lambda i: (layer, 0, 0)),
                  pl.BlockSpec((None, 2, HEAD_DIM), lambda i: (layer, 0, 0)),
                  pl.BlockSpec((HEAD_DIM, HEAD_DIM), lambda i: (0, 0))],
        out_specs=[pl.BlockSpec((tm, kvw), lambda i: (i, 0)),
                   pl.BlockSpec((tm, w), lambda i: (i, 0)),
                   pl.BlockSpec((tm, w), lambda i: (i, 0)),
                   pl.BlockSpec((tm, w), lambda i: (i, 0))],
        out_shape=[jax.ShapeDtypeStruct((rows, kvw), ACT_DTYPE),
                   jax.ShapeDtypeStruct((rows, w), ACT_DTYPE),
                   jax.ShapeDtypeStruct((rows, w), ACT_DTYPE),
                   jax.ShapeDtypeStruct((rows, w), ACT_DTYPE)],
        compiler_params=_params(40, 1),
        name="qk_prep",
    )(g1, g2, g3, g3, cos_t, sin_t, a_q_norm, a_k_norm, perm)


def _attn_kernel(q_ref, kl_ref, km_ref, kr_ref, kc_ref, vl_ref, vm_ref, vr_ref, vc_ref, sink_ref,
                 o_ref, *, ctx_blocks, n_blocks):
    rb = pl.program_id(0)
    blk = A_BLOCK
    n_keys = 3 * blk + kc_ref.shape[0]
    row = lax.broadcasted_iota(jnp.int32, (A_GROUP * blk, n_keys), 0) % blk
    col = lax.broadcasted_iota(jnp.int32, (A_GROUP * blk, n_keys), 1)
    is_lat = rb >= ctx_blocks
    c_lo = jnp.where(rb >= ctx_blocks + 1, 0, blk)
    c_hi = jnp.where(rb <= n_blocks - 2, 3 * blk, 2 * blk)
    c_lo = jnp.where(is_lat, c_lo, 3 * blk)
    c_hi = jnp.where(is_lat, c_hi, 0)
    row_lo = jnp.maximum(col - 2 * blk, 0)
    row_hi = jnp.where(col < blk, col, blk - 1)
    valid = ((col >= c_lo) & (col < c_hi) & (row >= row_lo) & (row <= row_hi)) | (col >= 3 * blk)
    heads = range(A_KV_HEADS)
    hs = lambda hk: slice(hk * HEAD_DIM, (hk + 1) * HEAD_DIM)
    qs = lambda hk, g: slice((hk * A_GROUP + g) * HEAD_DIM, (hk * A_GROUP + g + 1) * HEAD_DIM)
    scores = []
    for hk in heads:
        q = jnp.concatenate([q_ref[:, qs(hk, g)] for g in range(A_GROUP)], axis=0)
        k = jnp.concatenate([kl_ref[:, hs(hk)], km_ref[:, hs(hk)], kr_ref[:, hs(hk)], kc_ref[:, hs(hk)]], axis=0)
        scores.append(lax.dot_general(q, k, (((1,), (1,)), ((), ())), preferred_element_type=F32))
    probs, denoms = [], []
    for hk in heads:
        s = jnp.where(valid, scores[hk], NEG_INF)
        sink = sink_ref[hk]
        m = jnp.maximum(jnp.max(s, axis=-1, keepdims=True), sink)
        p = jnp.exp(s - m)
        denoms.append(jnp.sum(p, axis=-1, keepdims=True) + jnp.exp(sink - m))
        probs.append(p.astype(vm_ref.dtype))
    for hk in heads:
        v = jnp.concatenate([vl_ref[:, hs(hk)], vm_ref[:, hs(hk)], vr_ref[:, hs(hk)], vc_ref[:, hs(hk)]], axis=0)
        o = jnp.dot(probs[hk], v, preferred_element_type=F32) / denoms[hk]
        for g in range(A_GROUP):
            o_ref[:, qs(hk, g)] = o[g * blk:(g + 1) * blk].astype(o_ref.dtype)


def _attn_call(qn, kn, g1, sink_col, layer, ctx_rows):
    rows = qn.shape[0]
    blk = A_BLOCK
    nb = rows // blk
    cb = ctx_rows // blk
    qw = A_Q_HEADS * HEAD_DIM
    kvw = A_KV_HEADS * HEAD_DIM
    lo = lambda r: jnp.maximum(r - 1, 0)
    hi = lambda r: jnp.minimum(r + 1, nb - 1)
    ident = lambda r: r
    kspec = lambda f: pl.BlockSpec((blk, kvw), lambda r: (f(r), 0))
    vspec = lambda f: pl.BlockSpec((blk, kvw), lambda r: (f(r), 1))
    return pl.pallas_call(
        functools.partial(_attn_kernel, ctx_blocks=cb, n_blocks=nb),
        grid=(nb,),
        in_specs=[pl.BlockSpec((blk, qw), lambda r: (r, 0)),
                  kspec(lo), kspec(ident), kspec(hi),
                  pl.BlockSpec((ctx_rows, kvw), lambda r: (0, 0)),
                  vspec(lo), vspec(ident), vspec(hi),
                  pl.BlockSpec((ctx_rows, kvw), lambda r: (0, 1)),
                  pl.BlockSpec((None, A_KV_HEADS, A_GROUP * blk, 1), lambda r: (layer, 0, 0, 0))],
        out_specs=pl.BlockSpec((blk, qw), lambda r: (r, 0)),
        out_shape=jax.ShapeDtypeStruct((rows, qw), ACT_DTYPE),
        compiler_params=_params(32, 1),
        name="window_attn",
    )(qn, kn, kn, kn, kn, g1, g1, g1, g1, sink_col)


GMLP_CHUNKS_PER_STEP = 2


def _gmlp_kernel(u_ref, v_ref, bn_ref, ws_ref, bias_ref, o_ref):
    chunks = range(u_ref.shape[0] // CHUNK)
    rows = lambda c: slice(c * CHUNK, (c + 1) * CHUNK)
    for g in range(B_GROUPS):
        sl = slice(g * LANES, (g + 1) * LANES)
        normed = []
        for c in chunks:
            v = _gelu_tanh(v_ref[rows(c), sl].astype(F32))
            vc = v - jnp.mean(v, axis=-1, keepdims=True)
            vh = vc * lax.rsqrt(jnp.mean(vc * vc, axis=-1, keepdims=True) + EPS) * bn_ref[:, sl]
            normed.append(vh.astype(ws_ref.dtype))
        mixed = jnp.dot(ws_ref[g], jnp.concatenate(normed, axis=1), preferred_element_type=F32)
        for c in chunks:
            m = mixed[:, c * LANES:(c + 1) * LANES] + bias_ref[g]
            o_ref[rows(c), sl] = (_gelu_tanh(u_ref[rows(c), sl].astype(F32)) * m).astype(o_ref.dtype)


def _gmlp_call(g3, b_norm_flat, ws, bias_b, layer):
    rows = g3.shape[0]
    w = BRANCH_WIDTH
    tm = GMLP_CHUNKS_PER_STEP * CHUNK
    return pl.pallas_call(
        _gmlp_kernel,
        grid=(rows // tm,),
        in_specs=[pl.BlockSpec((tm, w), lambda i: (i, 3)),
                  pl.BlockSpec((tm, w), lambda i: (i, 4)),
                  pl.BlockSpec((None, 1, w), lambda i: (layer, 0, 0)),
                  pl.BlockSpec((None, B_GROUPS, CHUNK, CHUNK), lambda i: (layer, 0, 0, 0)),
                  pl.BlockSpec((None, B_GROUPS, CHUNK, LANES), lambda i: (layer, 0, 0, 0))],
        out_specs=pl.BlockSpec((tm, w), lambda i: (i, 0)),
        out_shape=jax.ShapeDtypeStruct((rows, w), ACT_DTYPE),
        compiler_params=_params(32, 1),
        name="chunk_gmlp",
    )(g3, g3, b_norm_flat, ws, bias_b)


def _ret_kernel(qf_ref, kf_ref, vf_ref, qb_ref, kb_ref, vb_ref, intra_ref, qdec_ref, kdec_ref, cdec_ref,
                of_ref, ob_ref, *state_refs):
    @pl.when(pl.program_id(0) == 0)
    def _():
        for s_ref in state_refs:
            s_ref[...] = jnp.zeros(s_ref.shape, s_ref.dtype)

    dirs = ((qf_ref, kf_ref, vf_ref, of_ref), (qb_ref, kb_ref, vb_ref, ob_ref))
    chains = [(d, h) for d in range(2) for h in range(C_HEADS)]
    head = lambda h: slice(h * HEAD_DIM, (h + 1) * HEAD_DIM)

    scores = []
    for d, h in chains:
        q_ref, k_ref, _, _ = dirs[d]
        q = q_ref[:, head(h)]
        a = lax.dot_general(q, k_ref[:, head(h)], (((1,), (1,)), ((), ())), preferred_element_type=F32)
        qd = (q.astype(F32) * qdec_ref[d, h]).astype(q.dtype)
        scores.append(jnp.concatenate([(a * intra_ref[d, h]).astype(q.dtype), qd], axis=1))
    for (d, h), lhs in zip(chains, scores):
        _, _, v_ref, o_ref = dirs[d]
        v = v_ref[:, head(h)]
        rhs = jnp.concatenate([v, state_refs[d * C_HEADS + h][...].astype(v.dtype)], axis=0)
        o_ref[:, head(h)] = jnp.dot(lhs, rhs, preferred_element_type=F32).astype(o_ref.dtype)
    for d, h in chains:
        _, k_ref, v_ref, _ = dirs[d]
        s_ref = state_refs[d * C_HEADS + h]
        k = k_ref[:, head(h)]
        kd = (k.astype(F32) * kdec_ref[d, h]).astype(k.dtype)
        upd = lax.dot_general(kd, v_ref[:, head(h)], (((0,), (0,)), ((), ())), preferred_element_type=F32)
        s_ref[...] = s_ref[...] * cdec_ref[d, h] + upd


def _ret_call(qr, kr, g2, tables, layer, ctx_rows):
    rows = qr.shape[0]
    w = BRANCH_WIDTH
    nc = rows // CHUNK
    cc = ctx_rows // CHUNK

    def bwd(s):
        return jnp.where(s < cc, cc - 1 - s, nc - 1 + cc - s)

    fq = pl.BlockSpec((CHUNK, w), lambda s: (s, 0))
    fv = pl.BlockSpec((CHUNK, w), lambda s: (s, 1))
    bq = pl.BlockSpec((CHUNK, w), lambda s: (bwd(s), 0))
    bv = pl.BlockSpec((CHUNK, w), lambda s: (bwd(s), 1))
    tab = pl.BlockSpec((None, 2, C_HEADS, CHUNK, LANES), lambda s: (layer, 0, 0, 0, 0))
    return pl.pallas_call(
        _ret_kernel,
        grid=(nc,),
        in_specs=[fq, fq, fv, bq, bq, bv, tab, tab, tab, tab],
        out_specs=[pl.BlockSpec((CHUNK, w), lambda s: (s, 0)),
                   pl.BlockSpec((CHUNK, w), lambda s: (bwd(s), 0))],
        out_shape=[jax.ShapeDtypeStruct((rows, w), F32), jax.ShapeDtypeStruct((rows, w), F32)],
        scratch_shapes=[pltpu.VMEM((HEAD_DIM, HEAD_DIM), F32) for _ in range(2 * C_HEADS)],
        compiler_params=_params(32, 1),
        name="retention",
    )(qr, kr, g2, qr, kr, g2, *tables)


def _merge_kernel(attn_ref, gm_ref, of_ref, ob_ref, rg_ref, gate_ref, cn_ref, wb_ref, o_ref):
    d = o_ref.shape[1]
    ret_parts = []
    for h in range(C_HEADS):
        sl = slice(h * HEAD_DIM, (h + 1) * HEAD_DIM)
        o = of_ref[:, sl] + ob_ref[:, sl]
        oc = o - jnp.mean(o, axis=-1, keepdims=True)
        y = oc * lax.rsqrt(jnp.mean(oc * oc, axis=-1, keepdims=True) + EPS) * cn_ref[:, sl]
        ret_parts.append((_silu(rg_ref[:, sl].astype(F32)) * y).astype(wb_ref.dtype))
    ret = jnp.concatenate(ret_parts, axis=1)
    branches = (attn_ref[...].astype(wb_ref.dtype), gm_ref[...].astype(wb_ref.dtype), ret)
    acc = None
    for b in range(N_BRANCH):
        proj = jnp.dot(branches[b], wb_ref[b], preferred_element_type=F32)
        term = _sigmoid(gate_ref[:, b * d:(b + 1) * d].astype(F32)) * proj
        acc = term if acc is None else acc + term
    o_ref[...] = acc.astype(o_ref.dtype)


def _merge_call(attn, gm, o_f, o_b, g3, g4, c_norm_flat, wb, layer):
    rows = attn.shape[0]
    w = BRANCH_WIDTH
    d = wb.shape[-1]
    tm = 256
    row = lambda c: pl.BlockSpec((tm, w), lambda i: (i, c))
    return pl.pallas_call(
        _merge_kernel,
        grid=(rows // tm,),
        in_specs=[row(0), row(0), row(0), row(0), row(2),
                  pl.BlockSpec((tm, N_BRANCH * d), lambda i: (i, 0)),
                  pl.BlockSpec((None, 1, w), lambda i: (layer, 0, 0)),
                  pl.BlockSpec((None, N_BRANCH, w, d), lambda i: (layer, 0, 0, 0))],
        out_specs=pl.BlockSpec((tm, d), lambda i: (i, 0)),
        out_shape=jax.ShapeDtypeStruct((rows, d), ACT_DTYPE),
        compiler_params=_params(52, 1),
        name="branch_merge",
    )(attn, gm, o_f, o_b, g3, g4, c_norm_flat, wb)


def _route_kernel(m_ref, wo_ref, z_ref, g_ref, mod_ref, whi_ref, wlo_ref, rb_ref,
                  znew_ref, h_ref, info_ref, cnt_ref, carry_ref, logits_ref):
    i = pl.program_id(0)

    @pl.when(i == 0)
    def _():
        carry_ref[...] = jnp.zeros(carry_ref.shape, carry_ref.dtype)
        logits_ref[...] = jnp.zeros(logits_ref.shape, logits_ref.dtype)

    prev_logits = logits_ref[(i + 1) % 2]
    y = jnp.dot(m_ref[...], wo_ref[...], preferred_element_type=F32)
    _route_stage(prev_logits, jnp.where(i >= 1, 1.0, 0.0), info_ref, cnt_ref, carry_ref)
    z = z_ref[...] + mod_ref[2:3, :] * y
    znew_ref[...] = z
    h = _norm_mod(z, g_ref[...], mod_ref[...], 3, 4)
    h_ref[...] = _pack_halves(h)
    h_hi = h.astype(whi_ref.dtype)
    h_lo = (h - h_hi.astype(F32)).astype(whi_ref.dtype)
    logits_ref[i % 2] = (jnp.dot(h_hi, whi_ref[...], preferred_element_type=F32)
                         + jnp.dot(h_hi, wlo_ref[...], preferred_element_type=F32)
                         + jnp.dot(h_lo, whi_ref[...], preferred_element_type=F32)) + rb_ref[...]


def _route_stage(logits, count_gate, info_ref, cnt_ref, carry_ref):
    tm = logits.shape[0]
    lane = lax.broadcasted_iota(jnp.int32, logits.shape, 1).astype(F32)
    first = lambda hit: jnp.min(jnp.where(hit, lane, 4.0 * LANES), axis=-1, keepdims=True)

    is_g = lane < N_GROUPS
    gl = jnp.where(is_g, logits, NEG_INF)
    gmax = jnp.max(gl, axis=-1, keepdims=True)
    g_sel = first(gl == gmax)
    g_w = 1.0 / jnp.sum(jnp.where(is_g, jnp.exp(gl - gmax), 0.0), axis=-1, keepdims=True)

    e_id = lane - N_GROUPS
    in_group = (e_id >= g_sel * EXPERTS_PER_GROUP) & (e_id < (g_sel + 1.0) * EXPERTS_PER_GROUP)
    el = jnp.where(in_group, logits, NEG_INF)
    m1 = jnp.max(el, axis=-1, keepdims=True)
    i1 = first(el == m1)
    el2 = jnp.where(lane == i1, NEG_INF, el)
    m2 = jnp.max(el2, axis=-1, keepdims=True)
    i2 = first(el2 == m2)
    r = jnp.exp(m2 - m1)
    w1 = g_w / (1.0 + r)
    w2 = g_w * r / (1.0 + r)
    e1 = i1 - N_GROUPS
    e2 = i2 - N_GROUPS

    hot1 = lane == e1
    hot2 = lane == e2
    hot = jnp.where(hot1 | hot2, 1.0, 0.0)
    rr = lax.broadcasted_iota(jnp.int32, (tm, tm), 0)
    cc = lax.broadcasted_iota(jnp.int32, (tm, tm), 1)
    tri = jnp.where(cc < rr, 1.0, 0.0).astype(MXU_DTYPE)
    before = jnp.dot(tri, hot.astype(MXU_DTYPE), preferred_element_type=F32) + carry_ref[0:1, :]
    rank1 = jnp.sum(jnp.where(hot1, before, 0.0), axis=-1, keepdims=True)
    rank2 = jnp.sum(jnp.where(hot2, before, 0.0), axis=-1, keepdims=True)
    carry_ref[0:1, :] = carry_ref[0:1, :] + count_gate * jnp.sum(hot, axis=0, keepdims=True)
    cnt_ref[...] = carry_ref[...]

    info = jnp.where(lane == 0, e1, 0.0)
    info = jnp.where(lane == 1, e2, info)
    info = jnp.where(lane == 2, w1, info)
    info = jnp.where(lane == 3, w2, info)
    info = jnp.where(lane == 4, rank1, info)
    info = jnp.where(lane == 5, rank2, info)
    info_ref[...] = info


def _route_call(merged, wo, z, norm_w, modsel, w_hi, w_lo, rbias, layer, ctx_rows):
    rows, d = z.shape
    tm = 256
    ctx_tiles = ctx_rows // tm
    n_tiles = rows // tm
    cur = lambda i: jnp.minimum(i, n_tiles - 1)
    prev = lambda i: jnp.maximum(i - 1, 0)
    return pl.pallas_call(
        _route_kernel,
        grid=(n_tiles + 1,),
        in_specs=[pl.BlockSpec((tm, d), lambda i: (cur(i), 0)),
                  pl.BlockSpec((None, d, d), lambda i: (layer, 0, 0)),
                  pl.BlockSpec((tm, d), lambda i: (cur(i), 0)),
                  pl.BlockSpec((None, 1, d), lambda i: (layer, 0, 0)),
                  pl.BlockSpec((None, None, MOD_ROWS, d),
                               lambda i: (layer, jnp.where(cur(i) >= ctx_tiles, 1, 0), 0, 0)),
                  pl.BlockSpec((None, d, LANES), lambda i: (layer, 0, 0)),
                  pl.BlockSpec((None, d, LANES), lambda i: (layer, 0, 0)),
                  pl.BlockSpec((None, 1, LANES), lambda i: (layer, 0, 0))],
        out_specs=[pl.BlockSpec((tm, d), lambda i: (cur(i), 0)),
                   pl.BlockSpec((tm, d // 2), lambda i: (cur(i), 0)),
                   pl.BlockSpec((tm, LANES), lambda i: (prev(i), 0)),
                   pl.BlockSpec((8, LANES), lambda i: (0, 0))],
        out_shape=[jax.ShapeDtypeStruct((rows, d), F32),
                   jax.ShapeDtypeStruct((rows, d // 2), PACK_DTYPE),
                   jax.ShapeDtypeStruct((rows, LANES), F32),
                   jax.ShapeDtypeStruct((8, LANES), F32)],
        scratch_shapes=[pltpu.VMEM((8, LANES), F32), pltpu.VMEM((2, tm, LANES), F32)],
        compiler_params=_params(48, 1),
        name="outproj_route",
    )(merged, wo, z, norm_w, modsel, w_hi, w_lo, rbias)


def _dispatch_kernel(pad_end_ref, padded_ref, n_used_ref, dest_ref, h_ref, xs_ref, zero_ref, sem, zero_sem):
    tm = h_ref.shape[0]
    n_blocks = xs_ref.shape[0] // MOE_BLOCK

    @pl.when(pl.program_id(0) == 0)
    def _():
        zero_ref[...] = jnp.zeros(zero_ref.shape, zero_ref.dtype)

        def block_copy(first):
            first = pl.multiple_of(first, MOE_BLOCK)
            return pltpu.make_async_copy(zero_ref, xs_ref.at[pl.ds(first, MOE_BLOCK)], zero_sem)

        def start_unused(b, carry):
            block_copy(b * MOE_BLOCK).start()
            return carry

        def wait_unused(b, carry):
            block_copy(b * MOE_BLOCK).wait()
            return carry

        for e in range(N_EXPERTS):
            @pl.when(padded_ref[e] > 0)
            def _(e=e):
                block_copy(pad_end_ref[e] - MOE_BLOCK).start()
        lax.fori_loop(n_used_ref[0], n_blocks, start_unused, 0)
        for e in range(N_EXPERTS):
            @pl.when(padded_ref[e] > 0)
            def _(e=e):
                block_copy(pad_end_ref[e] - MOE_BLOCK).wait()
        lax.fori_loop(n_used_ref[0], n_blocks, wait_unused, 0)

    def row_copy(t, slot):
        return pltpu.make_async_copy(h_ref.at[pl.ds(t, 1)], xs_ref.at[pl.ds(slot, 1)], sem)

    def start(t, carry):
        row_copy(t, dest_ref[0, 0, 2 * t]).start()
        row_copy(t, dest_ref[0, 0, 2 * t + 1]).start()
        return carry

    lax.fori_loop(0, tm, start, 0, unroll=4)
    all_rows = pltpu.make_async_copy(h_ref, xs_ref.at[pl.ds(0, tm)], sem)
    all_rows.wait()
    all_rows.wait()


def _dispatch_call(pad_end, padded, n_used, dest3, h, n_slots):
    rows, d = h.shape
    tm = dest3.shape[2] // 2
    grid_spec = pltpu.PrefetchScalarGridSpec(
        num_scalar_prefetch=3,
        grid=(rows // tm,),
        in_specs=[pl.BlockSpec((1, 1, 2 * tm), lambda i, pe, pd, nu: (i, 0, 0), memory_space=pltpu.SMEM),
                  pl.BlockSpec((tm, d), lambda i, pe, pd, nu: (i, 0))],
        out_specs=pl.BlockSpec(memory_space=pl.ANY),
        scratch_shapes=[pltpu.VMEM((MOE_BLOCK, d), h.dtype),
                        pltpu.SemaphoreType.DMA(()), pltpu.SemaphoreType.DMA(())])
    return pl.pallas_call(
        _dispatch_kernel,
        grid_spec=grid_spec,
        out_shape=jax.ShapeDtypeStruct((n_slots, d), h.dtype),
        compiler_params=_params(32, 1),
        name="moe_dispatch",
    )(pad_end, padded, n_used, dest3, h)


def _expert_kernel(be_ref, nu_ref, first_ref, slot_ref, next_ref, xs_ref, w1_hbm, w2_hbm, ys_ref,
                   w1f_ref, w2f_ref, w1bf_ref, w2bf_ref, sem, *, layer):
    i = pl.program_id(0)
    used = i < nu_ref[0]

    def fetch(e, s):
        return (pltpu.make_async_copy(w1_hbm.at[layer, e], w1f_ref.at[s], sem.at[s]),
                pltpu.make_async_copy(w2_hbm.at[layer, e], w2f_ref.at[s], sem.at[s]))

    @pl.when(i == 0)
    def _():
        for cp in fetch(be_ref[0], slot_ref[0]):
            cp.start()

    @pl.when(jnp.logical_and(used, first_ref[i] == 1))
    def _():
        s = slot_ref[i]
        for cp in fetch(be_ref[i], s):
            cp.wait()

        @pl.when(next_ref[i] >= 0)
        def _():
            for cp in fetch(next_ref[i], 1 - s):
                cp.start()

        w1bf_ref[...] = w1f_ref[s].astype(w1bf_ref.dtype)
        w2bf_ref[...] = w2f_ref[s].astype(w2bf_ref.dtype)

    @pl.when(used)
    def _():
        de = w2bf_ref.shape[0]
        x = jnp.concatenate(_unpack_halves(xs_ref[...]), axis=1).astype(w1bf_ref.dtype)
        hcat = jnp.dot(x, w1bf_ref[...], preferred_element_type=F32)
        act = _silu(hcat[:, :de]) * hcat[:, de:]
        ys_ref[...] = _pack_halves(jnp.dot(act.astype(w2bf_ref.dtype), w2bf_ref[...], preferred_element_type=F32))

    @pl.when(jnp.logical_not(used))
    def _():
        ys_ref[...] = jnp.zeros(ys_ref.shape, ys_ref.dtype)


def _expert_call(plan, xs, w_e1, w_e2, layer):
    slots, dp = xs.shape
    d = 2 * dp
    de = w_e2.shape[2]
    nb = slots // MOE_BLOCK
    blk = lambda i, be, nu, *_: (jnp.minimum(i, nu[0] - 1), 0)
    grid_spec = pltpu.PrefetchScalarGridSpec(
        num_scalar_prefetch=5,
        grid=(nb,),
        in_specs=[pl.BlockSpec((MOE_BLOCK, dp), blk),
                  pl.BlockSpec(memory_space=pl.ANY),
                  pl.BlockSpec(memory_space=pl.ANY)],
        out_specs=pl.BlockSpec((MOE_BLOCK, dp), lambda i, *_: (i, 0)),
        scratch_shapes=[pltpu.VMEM((2, d, 2 * de), w_e1.dtype), pltpu.VMEM((2, de, d), w_e2.dtype),
                        pltpu.VMEM((d, 2 * de), MXU_DTYPE), pltpu.VMEM((de, d), MXU_DTYPE),
                        pltpu.SemaphoreType.DMA((2,))])
    return pl.pallas_call(
        functools.partial(_expert_kernel, layer=layer),
        grid_spec=grid_spec,
        out_shape=jax.ShapeDtypeStruct((slots, dp), PACK_DTYPE),
        compiler_params=_params(48, 1),
        name="moe_experts",
    )(plan["block_e"], plan["n_used"], plan["first"], plan["slot"], plan["next_e"], xs, w_e1, w_e2)


def _combine_kernel(dest_ref, dest_next_ref, ys_ref, info_ref, z_ref, mod_ref, *rest, emit_next):
    if emit_next:
        gn_ref, modn_ref, o_ref, hn_ref, buf_ref, sem = rest
    else:
        o_ref, buf_ref, sem = rest
    i = pl.program_id(0)
    tm = z_ref.shape[0]
    slot = i % 2

    def issue(d_ref, s):
        def start(t, carry):
            for k in range(2):
                pltpu.make_async_copy(ys_ref.at[pl.ds(d_ref[0, 0, 2 * t + k], 1)],
                                      buf_ref.at[s, k, pl.ds(t, 1)], sem.at[s]).start()
            return carry

        lax.fori_loop(0, tm, start, 0, unroll=4)

    @pl.when(i == 0)
    def _():
        issue(dest_ref, 0)

    @pl.when(i + 1 < pl.num_programs(0))
    def _():
        issue(dest_next_ref, 1 - slot)

    for k in range(2):
        pltpu.make_async_copy(ys_ref.at[pl.ds(0, tm)], buf_ref.at[slot, k], sem.at[slot]).wait()
    info = info_ref[...]
    lane = lax.broadcasted_iota(jnp.int32, info.shape, 1)
    w1 = jnp.sum(jnp.where(lane == 2, info, 0.0), axis=-1, keepdims=True)
    w2 = jnp.sum(jnp.where(lane == 3, info, 0.0), axis=-1, keepdims=True)
    lo1, hi1 = _unpack_halves(buf_ref[slot, 0])
    lo2, hi2 = _unpack_halves(buf_ref[slot, 1])
    y = jnp.concatenate([lo1 * w1 + lo2 * w2, hi1 * w1 + hi2 * w2], axis=1)
    z = z_ref[...] + mod_ref[5:6, :] * y
    o_ref[...] = z
    if emit_next:
        hn_ref[...] = _norm_mod(z, gn_ref[...], modn_ref[...], 0, 1).astype(hn_ref.dtype)


def _combine_call(dest3, ys, info, z, modsel, layer, ctx_rows, next_norm_w=None):
    rows, d = z.shape
    tm = dest3.shape[2] // 2
    ctx_tiles = ctx_rows // tm
    n_tiles = rows // tm
    emit_next = next_norm_w is not None
    mod_spec = lambda l: pl.BlockSpec((None, None, MOD_ROWS, d),
                                      lambda i: (l, jnp.where(i >= ctx_tiles, 1, 0), 0, 0))
    row_spec = pl.BlockSpec((tm, d), lambda i: (i, 0))
    in_specs = [pl.BlockSpec((1, 1, 2 * tm), lambda i: (i, 0, 0), memory_space=pltpu.SMEM),
                pl.BlockSpec((1, 1, 2 * tm), lambda i: (jnp.minimum(i + 1, n_tiles - 1), 0, 0),
                             memory_space=pltpu.SMEM),
                pl.BlockSpec(memory_space=pl.ANY),
                pl.BlockSpec((tm, LANES), lambda i: (i, 0)),
                row_spec,
                mod_spec(layer)]
    args = [dest3, dest3, ys, info, z, modsel]
    if emit_next:
        out_specs = [row_spec]
        out_shape = [jax.ShapeDtypeStruct((rows, d), F32)]
    else:
        out_specs = [pl.BlockSpec((tm, d), lambda i: (jnp.maximum(i - ctx_tiles, 0), 0))]
        out_shape = [jax.ShapeDtypeStruct((rows - ctx_rows, d), F32)]
    if emit_next:
        in_specs += [pl.BlockSpec((None, 1, d), lambda i: (layer + 1, 0, 0)), mod_spec(layer + 1)]
        args += [next_norm_w, modsel]
        out_specs.append(row_spec)
        out_shape.append(jax.ShapeDtypeStruct((rows, d), ACT_DTYPE))
    return pl.pallas_call(
        functools.partial(_combine_kernel, emit_next=emit_next),
        grid=(n_tiles,),
        in_specs=in_specs,
        out_specs=out_specs,
        out_shape=out_shape,
        scratch_shapes=[pltpu.VMEM((2, 2, tm, ys.shape[1]), ys.dtype), pltpu.SemaphoreType.DMA((2,))],
        compiler_params=_params(40, 1),
        name="moe_combine",
    )(*args)


def _rope_tables(n, ctx_rows):
    rows = n // GRID_W
    row = jnp.repeat(jnp.arange(rows, dtype=F32), GRID_W)
    col = jnp.tile(jnp.arange(GRID_W, dtype=F32), rows)
    nq = HEAD_DIM // 4
    inv = ROPE_BASE ** (-jnp.arange(nq, dtype=F32) / nq)
    ar, ac = row[:, None] * inv, col[:, None] * inv
    cos = jnp.concatenate([jnp.cos(ar), jnp.cos(ar), jnp.cos(ac), jnp.cos(ac)], axis=1)
    sin = jnp.concatenate([-jnp.sin(ar), jnp.sin(ar), -jnp.sin(ac), jnp.sin(ac)], axis=1)
    cos = jnp.concatenate([jnp.ones((ctx_rows, HEAD_DIM), F32), cos], axis=0)
    sin = jnp.concatenate([jnp.zeros((ctx_rows, HEAD_DIM), F32), sin], axis=0)
    return cos, sin


def _retention_tables(c_decay_fwd, c_decay_bwd):
    lg_f = jax.nn.log_sigmoid(c_decay_fwd.astype(F32))[:, :, None, None]
    lg_b = jax.nn.log_sigmoid(c_decay_bwd.astype(F32))[:, :, None, None]
    idx = jnp.arange(CHUNK, dtype=F32)
    diff = idx[:, None] - idx[None, :]
    ones = jnp.ones((CHUNK, CHUNK), F32)
    t_col = idx[:, None] * ones
    intra_f = jnp.where(diff >= 0, jnp.exp(lg_f * jnp.maximum(diff, 0.0)), 0.0)
    intra_b = jnp.where(diff <= 0, jnp.exp(lg_b * jnp.maximum(-diff, 0.0)), 0.0)
    qdec_f = jnp.exp(lg_f * (t_col + 1.0))
    qdec_b = jnp.exp(lg_b * (CHUNK - t_col))
    kdec_f = jnp.exp(lg_f * (CHUNK - 1.0 - t_col))
    kdec_b = jnp.exp(lg_b * t_col)
    cdec_f = jnp.exp(lg_f * CHUNK) * ones
    cdec_b = jnp.exp(lg_b * CHUNK) * ones
    pair = lambda a, b: jnp.stack([a, b], axis=1)
    return (pair(intra_f, intra_b), pair(qdec_f, qdec_b), pair(kdec_f, kdec_b), pair(cdec_f, cdec_b))


def _moe_plan(info, counts_row, n_slots_blocks):
    e = info[:, 0:2].astype(jnp.int32)
    rank = info[:, 4:6].astype(jnp.int32)
    counts = counts_row[:N_EXPERTS].astype(jnp.int32)
    padded = (counts + MOE_BLOCK - 1) // MOE_BLOCK * MOE_BLOCK
    pad_end = jnp.cumsum(padded)
    pad_start = pad_end - padded
    hit = e[:, :, None] == jnp.arange(N_EXPERTS, dtype=jnp.int32)
    dest = jnp.sum(jnp.where(hit, pad_start, 0), axis=-1) + rank
    n_used = pad_end[-1] // MOE_BLOCK
    blocks = jnp.arange(n_slots_blocks, dtype=jnp.int32)
    first_slot = jnp.minimum(blocks, n_used - 1) * MOE_BLOCK
    block_e = jnp.sum((pad_end[None, :] <= first_slot[:, None]).astype(jnp.int32), axis=1)
    ids = jnp.arange(N_EXPERTS, dtype=jnp.int32)
    nonempty = counts > 0
    slot_of = (jnp.cumsum(nonempty.astype(jnp.int32)) - 1) % 2
    later = nonempty[None, :] & (ids[None, :] > ids[:, None])
    next_of = jnp.min(jnp.where(later, ids[None, :], N_EXPERTS), axis=1)
    next_of = jnp.where(next_of == N_EXPERTS, -1, next_of)
    pick = lambda table: jnp.sum(jnp.where(block_e[:, None] == ids[None, :], table[None, :], 0), axis=1)
    prev_e = jnp.concatenate([jnp.full((1,), -1, jnp.int32), block_e[:-1]])
    first = ((block_e != prev_e) & (blocks < n_used)).astype(jnp.int32)
    i32 = lambda a: a.astype(jnp.int32)
    return dict(dest=dest, block_e=i32(block_e), n_used=i32(n_used.reshape(1)), pad_end=i32(pad_end),
                padded=i32(padded), first=first, slot=i32(pick(slot_of)), next_e=i32(pick(next_of)))


def kernel(x, c, ctx, c_ctx, norm_mix, norm_ffn, w_ada, b_ada, w_in, a_q_norm, a_k_norm, a_sink,
           b_norm, b_spatial, b_spatial_bias, c_decay_fwd, c_decay_bwd, c_norm, w_branch, w_out,
           w_router_group, b_router_group, w_router_expert, b_router_expert, w_expert_in, w_expert_out):
    batch, n, d = x.shape
    ctx_rows = ctx.shape[1]
    depth = w_in.shape[0]
    assert batch == 1 and ctx_rows % 256 == 0 and n % 256 == 0
    rows = ctx_rows + n

    cond = jnp.stack([c[0], c_ctx], axis=0)
    cond_b = jnp.broadcast_to(cond[:, :, None], (2, d, LANES))
    mod = _ada_call(cond_b, w_ada, b_ada)[:, :2].reshape(depth, 2, N_MOD, d)
    modsel = jnp.pad(mod[:, ::-1], ((0, 0), (0, 0), (0, MOD_ROWS - N_MOD), (0, 0)))

    cos_t, sin_t = _rope_tables(n, ctx_rows)
    ret_tables = _retention_tables(c_decay_fwd, c_decay_bwd)
    sink_col = jnp.broadcast_to(a_sink.astype(F32).reshape(depth, A_KV_HEADS, A_GROUP, 1, 1),
                                (depth, A_KV_HEADS, A_GROUP, A_BLOCK, 1)).reshape(depth, A_KV_HEADS, A_GROUP * A_BLOCK, 1)
    ws = b_spatial.astype(MXU_DTYPE)
    bias_b = jnp.broadcast_to(b_spatial_bias.astype(F32)[:, :, :, None], (depth, B_GROUPS, CHUNK, LANES))
    wb = w_branch.astype(MXU_DTYPE)
    wo = w_out.astype(MXU_DTYPE)
    w_r = jnp.concatenate([w_router_group, w_router_expert], axis=-1).astype(F32)
    w_r = jnp.pad(w_r, ((0, 0), (0, 0), (0, LANES - w_r.shape[-1])))
    w_r_hi = w_r.astype(MXU_DTYPE)
    w_r_lo = (w_r - w_r_hi.astype(F32)).astype(MXU_DTYPE)
    b_r = jnp.concatenate([b_router_group, b_router_expert], axis=-1).astype(F32)
    b_r = jnp.pad(b_r, ((0, 0), (0, LANES - b_r.shape[-1]))).reshape(depth, 1, LANES)
    norm_mix3 = norm_mix.reshape(depth, 1, d)
    norm_ffn3 = norm_ffn.reshape(depth, 1, d)
    lane_ids = jnp.arange(HEAD_DIM)
    partner = jnp.where((lane_ids // 32) % 2 == 0, lane_ids + 32, lane_ids - 32)
    perm = (lane_ids[:, None] == partner[None, :]).astype(ACT_DTYPE)
    a_q_norm3 = jnp.stack([a_q_norm, a_q_norm[:, partner]], axis=1).astype(F32)
    a_k_norm3 = jnp.stack([a_k_norm, a_k_norm[:, partner]], axis=1).astype(F32)
    b_norm3 = b_norm.reshape(depth, 1, BRANCH_WIDTH)
    c_norm3 = c_norm.reshape(depth, 1, BRANCH_WIDTH)

    kvw = A_KV_HEADS * HEAD_DIM
    w = BRANCH_WIDTH
    n_assign = rows * 2
    n_slot_blocks = -(-(n_assign + N_EXPERTS * (MOE_BLOCK - 1)) // MOE_BLOCK)
    tok_tile = 256
    disp_tile = _row_tile(rows, (1056, 768, 256))

    z = jnp.concatenate([ctx[0], x[0]], axis=0)
    h = _norm_mod_call(z, norm_mix3, modsel, 0, ctx_rows)
    for l in range(depth):
        g1 = _proj_call(h, w_in, l, 0, 2 * kvw)
        g2 = _proj_call(h, w_in, l, 2 * kvw, 2 * w)
        g3 = _proj_call(h, w_in, l, 2 * kvw + 2 * w, 5 * w)
        g4 = _proj_call(h, w_in, l, 2 * kvw + 7 * w, N_BRANCH * d)
        kn, qn, kr, qr = _prep_call(g1, g2, g3, cos_t, sin_t, a_q_norm3, a_k_norm3, perm, l)
        attn = _attn_call(qn, kn, g1, sink_col, l, ctx_rows)
        gm = _gmlp_call(g3, b_norm3, ws, bias_b, l)
        o_f, o_b = _ret_call(qr, kr, g2, ret_tables, l, ctx_rows)
        merged = _merge_call(attn, gm, o_f, o_b, g3, g4, c_norm3, wb, l)
        z, h2, info, counts = _route_call(merged, wo, z, norm_ffn3, modsel, w_r_hi, w_r_lo, b_r, l, ctx_rows)
        plan = _moe_plan(info, counts[0], n_slot_blocks)
        dest3 = plan["dest"].reshape(rows // tok_tile, 1, 2 * tok_tile)
        dest3_disp = plan["dest"].reshape(rows // disp_tile, 1, 2 * disp_tile)
        xs = _dispatch_call(plan["pad_end"], plan["padded"], plan["n_used"], dest3_disp, h2,
                            n_slot_blocks * MOE_BLOCK)
        ys = _expert_call(plan, xs, w_expert_in, w_expert_out, l)
        if l + 1 < depth:
            z, h = _combine_call(dest3, ys, info, z, modsel, l, ctx_rows, next_norm_w=norm_mix3)
        else:
            (z_latent,) = _combine_call(dest3, ys, info, z, modsel, l, ctx_rows)
    return z_latent[None]
```

```python
import functools
import math

import jax
import jax.numpy as jnp
from jax import lax
from jax.experimental import pallas as pl
from jax.experimental.pallas import tpu as pltpu

F32 = jnp.float32
MXU_DTYPE = jnp.bfloat16
ACT_DTYPE = jnp.bfloat16

LANES = 128
HEAD_DIM = 128
GRID_W = 64
ROPE_BASE = 10000.0
EPS = 1e-6
NEG_INF = -1e30
A_Q_HEADS = 8
A_KV_HEADS = 2
A_GROUP = A_Q_HEADS // A_KV_HEADS
A_BLOCK = 128
B_GROUPS = 8
C_HEADS = 8
CHUNK = 128
N_GROUPS = 4
EXPERTS_PER_GROUP = 8
N_EXPERTS = N_GROUPS * EXPERTS_PER_GROUP
D_EXPERT = 512
BRANCH_WIDTH = 1024
N_BRANCH = 3
MOE_BLOCK = 256
N_MOD = 6
MOD_ROWS = 8
MIB = 1024 * 1024


def _params(vmem_mib, n_grid, **kw):
    return pltpu.CompilerParams(dimension_semantics=("arbitrary",) * n_grid,
                                vmem_limit_bytes=vmem_mib * MIB, **kw)


def _sigmoid(x):
    return 0.5 + 0.5 * jnp.tanh(0.5 * x)


def _silu(x):
    return x * _sigmoid(x)


PACK_DTYPE = jnp.uint32


def _pack_halves(x):
    n = x.shape[1] // 2
    rounded = lambda v: lax.bitcast_convert_type(v.astype(jnp.bfloat16).astype(F32), PACK_DTYPE)
    return rounded(x[:, n:]) | (rounded(x[:, :n]) >> 16)


def _unpack_halves(p):
    lo = lax.bitcast_convert_type(p << 16, F32)
    hi = lax.bitcast_convert_type(p & jnp.asarray(0xFFFF0000, PACK_DTYPE), F32)
    return lo, hi


def _gelu_tanh(x):
    return 0.5 * x * (1.0 + jnp.tanh(math.sqrt(2.0 / math.pi) * (x + 0.044715 * (x * x * x))))


def _ada_kernel(c_ref, w_ref, b_ref, o_ref):
    tn = w_ref.shape[1]
    s0 = _silu(c_ref[0])
    s1 = _silu(c_ref[1])
    o_ref[...] = jnp.zeros(o_ref.shape, o_ref.dtype)
    for j in range(tn // LANES):
        sl = slice(j * LANES, (j + 1) * LANES)
        wj = w_ref[:, sl]
        o_ref[0:1, sl] = jnp.sum(wj * s0, axis=0, keepdims=True) + b_ref[:, sl]
        o_ref[1:2, sl] = jnp.sum(wj * s1, axis=0, keepdims=True) + b_ref[:, sl]


def _ada_call(cond_b, w_ada, b_ada):
    depth, k, n = w_ada.shape
    tn = 2048
    return pl.pallas_call(
        _ada_kernel,
        grid=(depth, n // tn),
        in_specs=[pl.BlockSpec((2, k, LANES), lambda l, j: (0, 0, 0)),
                  pl.BlockSpec((None, k, tn), lambda l, j: (l, 0, j)),
                  pl.BlockSpec((None, 1, tn), lambda l, j: (l, 0, j))],
        out_specs=pl.BlockSpec((None, 8, tn), lambda l, j: (l, 0, j)),
        out_shape=jax.ShapeDtypeStruct((depth, 8, n), F32),
        compiler_params=_params(48, 2),
        name="adaln",
    )(cond_b, w_ada, b_ada.reshape(depth, 1, n))


def _norm_mod(z, g, mod, shift_row, scale_row):
    r = lax.rsqrt(jnp.mean(z * z, axis=-1, keepdims=True) + EPS)
    return (z * r * g) * (1.0 + mod[scale_row:scale_row + 1, :]) + mod[shift_row:shift_row + 1, :]


def _norm_mod_kernel(z_ref, g_ref, mod_ref, o_ref):
    o_ref[...] = _norm_mod(z_ref[...], g_ref[...], mod_ref[...], 0, 1).astype(o_ref.dtype)


def _norm_mod_call(z, norm_w, modsel, layer, ctx_rows):
    rows, d = z.shape
    tm = 256
    ctx_tiles = ctx_rows // tm
    return pl.pallas_call(
        _norm_mod_kernel,
        grid=(rows // tm,),
        in_specs=[pl.BlockSpec((tm, d), lambda i: (i, 0)),
                  pl.BlockSpec((None, 1, d), lambda i: (layer, 0, 0)),
                  pl.BlockSpec((None, None, MOD_ROWS, d),
                               lambda i: (layer, jnp.where(i >= ctx_tiles, 1, 0), 0, 0))],
        out_specs=pl.BlockSpec((tm, d), lambda i: (i, 0)),
        out_shape=jax.ShapeDtypeStruct((rows, d), ACT_DTYPE),
        compiler_params=_params(32, 1),
        name="norm_mod",
    )(z, norm_w, modsel)


def _proj_kernel(h_ref, w_ref, o_ref, wbf_ref):
    @pl.when(pl.program_id(1) == 0)
    def _():
        wbf_ref[...] = w_ref[...].astype(wbf_ref.dtype)

    o_ref[...] = jnp.dot(h_ref[...], wbf_ref[...], preferred_element_type=F32).astype(o_ref.dtype)


def _row_tile(rows, pref):
    for t in pref:
        if rows % t == 0:
            return t
    raise ValueError(f"no row tile for {rows}")


def _proj_call(h, w, layer, col_off, ncols):
    rows, k = h.shape
    tn = next(t for t in (1536, 1280, 1024, 512) if col_off % t == 0 and ncols % t == 0)
    tm = _row_tile(rows, (768, 512, 256))
    off = col_off // tn
    return pl.pallas_call(
        _proj_kernel,
        grid=(ncols // tn, rows // tm),
        in_specs=[pl.BlockSpec((tm, k), lambda j, i: (i, 0)),
                  pl.BlockSpec((None, k, tn), lambda j, i: (layer, 0, off + j))],
        out_specs=pl.BlockSpec((tm, tn), lambda j, i: (i, j)),
        out_shape=jax.ShapeDtypeStruct((rows, ncols), ACT_DTYPE),
        scratch_shapes=[pltpu.VMEM((k, tn), MXU_DTYPE)],
        compiler_params=_params(52, 2),
        name="proj_in",
    )(h, w)


def _prep_kernel(ak_ref, ck0_ref, ck1_ref, aq_ref, cq_ref, cos_ref, sin_ref, qn_ref, kn_ref, perm_ref,
                 okn_ref, oqn_ref, okr_ref, oqr_ref):
    cos = cos_ref[...]
    sin = sin_ref[...]
    perm = perm_ref[...]
    scale = HEAD_DIM ** -0.5

    def swap(x):
        return jnp.dot(x, perm, preferred_element_type=F32)

    def norm_rope(x, g_ref, out_scale):
        xf = x.astype(F32)
        r = lax.rsqrt(jnp.mean(xf * xf, axis=-1, keepdims=True) + EPS) * out_scale
        return (xf * (g_ref[0:1, :] * cos) + swap(x) * (g_ref[1:2, :] * sin)) * r

    for h in range(A_KV_HEADS):
        sl = slice(h * HEAD_DIM, (h + 1) * HEAD_DIM)
        okn_ref[:, sl] = norm_rope(ak_ref[:, sl], kn_ref, 1.0).astype(okn_ref.dtype)
    for h in range(A_Q_HEADS):
        sl = slice(h * HEAD_DIM, (h + 1) * HEAD_DIM)
        oqn_ref[:, sl] = norm_rope(aq_ref[:, sl], qn_ref, scale).astype(oqn_ref.dtype)
    for h in range(C_HEADS):
        sl = slice(h * HEAD_DIM, (h + 1) * HEAD_DIM)
        half = C_HEADS // 2
        k = (ck0_ref if h < half else ck1_ref)[:, (h % half) * HEAD_DIM:(h % half + 1) * HEAD_DIM]
        q = cq_ref[:, sl]
        okr_ref[:, sl] = ((k.astype(F32) * cos + swap(k) * sin) * scale).astype(okr_ref.dtype)
        oqr_ref[:, sl] = (q.astype(F32) * cos + swap(q) * sin).astype(oqr_ref.dtype)


def _prep_call(g12, g3, cos_t, sin_t, a_q_norm, a_k_norm, perm, layer):
    rows = g12.shape[0]
    tm = _row_tile(rows, (768, 512, 256))
    w = BRANCH_WIDTH
    kvw = A_KV_HEADS * HEAD_DIM
    return pl.pallas_call(
        _prep_kernel,
        grid=(rows // tm,),
        in_specs=[pl.BlockSpec((tm, kvw), lambda i: (i, 0)),
                  pl.BlockSpec((tm, w // 2), lambda i: (i, 1)),
                  pl.BlockSpec((tm, w // 2), lambda i: (i, 2)),
                  pl.BlockSpec((tm, w), lambda i: (i, 0)),
                  pl.BlockSpec((tm, w), lambda i: (i, 1)),
                  pl.BlockSpec((tm, LANES), lambda i: (i, 0)),
                  pl.BlockSpec((tm, LANES), lambda i: (i, 0)),
                  pl.BlockSpec((None, 2, HEAD_DIM), lambda i: (layer, 0, 0)),
                  pl.BlockSpec((None, 2, HEAD_DIM), lambda i: (layer, 0, 0)),
                  pl.BlockSpec((HEAD_DIM, HEAD_DIM), lambda i: (0, 0))],
        out_specs=[pl.BlockSpec((tm, kvw), lambda i: (i, 0)),
                   pl.BlockSpec((tm, w), lambda i: (i, 0)),
                   pl.BlockSpec((tm, w), lambda i: (i, 0)),
                   pl.BlockSpec((tm, w), lambda i: (i, 0))],
        out_shape=[jax.ShapeDtypeStruct((rows, kvw), ACT_DTYPE),
                   jax.ShapeDtypeStruct((rows, w), ACT_DTYPE),
                   jax.ShapeDtypeStruct((rows, w), ACT_DTYPE),
                   jax.ShapeDtypeStruct((rows, w), ACT_DTYPE)],
        compiler_params=_params(40, 1),
        name="qk_prep",
    )(g12, g12, g12, g3, g3, cos_t, sin_t, a_q_norm, a_k_norm, perm)


def _attn_kernel(q_ref, kl_ref, km_ref, kr_ref, kc_ref, vl_ref, vm_ref, vr_ref, vc_ref, sink_ref,
                 o_ref, *, ctx_blocks, n_blocks):
    rb = pl.program_id(0)
    blk = A_BLOCK
    n_keys = 3 * blk + kc_ref.shape[0]
    row = lax.broadcasted_iota(jnp.int32, (A_GROUP * blk, n_keys), 0) % blk
    col = lax.broadcasted_iota(jnp.int32, (A_GROUP * blk, n_keys), 1)
    is_lat = rb >= ctx_blocks
    c_lo = jnp.where(rb >= ctx_blocks + 1, 0, blk)
    c_hi = jnp.where(rb <= n_blocks - 2, 3 * blk, 2 * blk)
    c_lo = jnp.where(is_lat, c_lo, 3 * blk)
    c_hi = jnp.where(is_lat, c_hi, 0)
    row_lo = jnp.maximum(col - 2 * blk, 0)
    row_hi = jnp.where(col < blk, col, blk - 1)
    valid = ((col >= c_lo) & (col < c_hi) & (row >= row_lo) & (row <= row_hi)) | (col >= 3 * blk)
    heads = range(A_KV_HEADS)
    hs = lambda hk: slice(hk * HEAD_DIM, (hk + 1) * HEAD_DIM)
    qs = lambda hk, g: slice((hk * A_GROUP + g) * HEAD_DIM, (hk * A_GROUP + g + 1) * HEAD_DIM)
    scores = []
    for hk in heads:
        q = jnp.concatenate([q_ref[:, qs(hk, g)] for g in range(A_GROUP)], axis=0)
        k = jnp.concatenate([kl_ref[:, hs(hk)], km_ref[:, hs(hk)], kr_ref[:, hs(hk)], kc_ref[:, hs(hk)]], axis=0)
        scores.append(lax.dot_general(q, k, (((1,), (1,)), ((), ())), preferred_element_type=F32))
    probs, denoms = [], []
    for hk in heads:
        s = jnp.where(valid, scores[hk], NEG_INF)
        sink = sink_ref[hk]
        m = jnp.maximum(jnp.max(s, axis=-1, keepdims=True), sink)
        p = jnp.exp(s - m)
        denoms.append(jnp.sum(p, axis=-1, keepdims=True) + jnp.exp(sink - m))
        probs.append(p.astype(vm_ref.dtype))
    for hk in heads:
        v = jnp.concatenate([vl_ref[:, hs(hk)], vm_ref[:, hs(hk)], vr_ref[:, hs(hk)], vc_ref[:, hs(hk)]], axis=0)
        o = jnp.dot(probs[hk], v, preferred_element_type=F32) / denoms[hk]
        for g in range(A_GROUP):
            o_ref[:, qs(hk, g)] = o[g * blk:(g + 1) * blk].astype(o_ref.dtype)


def _attn_call(qn, kn, g1, sink_col, layer, ctx_rows):
    rows = qn.shape[0]
    blk = A_BLOCK
    nb = rows // blk
    cb = ctx_rows // blk
    qw = A_Q_HEADS * HEAD_DIM
    kvw = A_KV_HEADS * HEAD_DIM
    lo = lambda r: jnp.maximum(r - 1, 0)
    hi = lambda r: jnp.minimum(r + 1, nb - 1)
    ident = lambda r: r
    kspec = lambda f: pl.BlockSpec((blk, kvw), lambda r: (f(r), 0))
    vspec = lambda f: pl.BlockSpec((blk, kvw), lambda r: (f(r), 1))
    return pl.pallas_call(
        functools.partial(_attn_kernel, ctx_blocks=cb, n_blocks=nb),
        grid=(nb,),
        in_specs=[pl.BlockSpec((blk, qw), lambda r: (r, 0)),
                  kspec(lo), kspec(ident), kspec(hi),
                  pl.BlockSpec((ctx_rows, kvw), lambda r: (0, 0)),
                  vspec(lo), vspec(ident), vspec(hi),
                  pl.BlockSpec((ctx_rows, kvw), lambda r: (0, 1)),
                  pl.BlockSpec((None, A_KV_HEADS, A_GROUP * blk, 1), lambda r: (layer, 0, 0, 0))],
        out_specs=pl.BlockSpec((blk, qw), lambda r: (r, 0)),
        out_shape=jax.ShapeDtypeStruct((rows, qw), ACT_DTYPE),
        compiler_params=_params(32, 1),
        name="window_attn",
    )(qn, kn, kn, kn, kn, g1, g1, g1, g1, sink_col)


GMLP_CHUNKS_PER_STEP = 2


def _gmlp_kernel(u_ref, v_ref, bn_ref, ws_ref, bias_ref, o_ref):
    chunks = range(u_ref.shape[0] // CHUNK)
    rows = lambda c: slice(c * CHUNK, (c + 1) * CHUNK)
    for g in range(B_GROUPS):
        sl = slice(g * LANES, (g + 1) * LANES)
        normed = []
        for c in chunks:
            v = _gelu_tanh(v_ref[rows(c), sl].astype(F32))
            vc = v - jnp.mean(v, axis=-1, keepdims=True)
            vh = vc * lax.rsqrt(jnp.mean(vc * vc, axis=-1, keepdims=True) + EPS) * bn_ref[:, sl]
            normed.append(vh.astype(ws_ref.dtype))
        mixed = jnp.dot(ws_ref[g], jnp.concatenate(normed, axis=1), preferred_element_type=F32)
        for c in chunks:
            m = mixed[:, c * LANES:(c + 1) * LANES] + bias_ref[g]
            o_ref[rows(c), sl] = (_gelu_tanh(u_ref[rows(c), sl].astype(F32)) * m).astype(o_ref.dtype)


def _gmlp_call(g3, b_norm_flat, ws, bias_b, layer):
    rows = g3.shape[0]
    w = BRANCH_WIDTH
    tm = GMLP_CHUNKS_PER_STEP * CHUNK
    return pl.pallas_call(
        _gmlp_kernel,
        grid=(rows // tm,),
        in_specs=[pl.BlockSpec((tm, w), lambda i: (i, 3)),
                  pl.BlockSpec((tm, w), lambda i: (i, 4)),
                  pl.BlockSpec((None, 1, w), lambda i: (layer, 0, 0)),
                  pl.BlockSpec((None, B_GROUPS, CHUNK, CHUNK), lambda i: (layer, 0, 0, 0)),
                  pl.BlockSpec((None, B_GROUPS, CHUNK, LANES), lambda i: (layer, 0, 0, 0))],
        out_specs=pl.BlockSpec((tm, w), lambda i: (i, 0)),
        out_shape=jax.ShapeDtypeStruct((rows, w), ACT_DTYPE),
        compiler_params=_params(32, 1),
        name="chunk_gmlp",
    )(g3, g3, b_norm_flat, ws, bias_b)


def _ret_kernel(qf_ref, kf_ref, vf0_ref, vf1_ref, qb_ref, kb_ref, vb0_ref, vb1_ref,
                intra_ref, qdec_ref, kdec_ref, cdec_ref,
                of_ref, ob_ref, *state_refs):
    @pl.when(pl.program_id(0) == 0)
    def _():
        for s_ref in state_refs:
            s_ref[...] = jnp.zeros(s_ref.shape, s_ref.dtype)

    dirs = ((qf_ref, kf_ref, of_ref), (qb_ref, kb_ref, ob_ref))
    v_halves = ((vf0_ref, vf1_ref), (vb0_ref, vb1_ref))
    chains = [(d, h) for d in range(2) for h in range(C_HEADS)]
    head = lambda h: slice(h * HEAD_DIM, (h + 1) * HEAD_DIM)
    half = C_HEADS // 2
    value = lambda d, h: v_halves[d][h // half][:, (h % half) * HEAD_DIM:(h % half + 1) * HEAD_DIM]

    scores = []
    for d, h in chains:
        q_ref, k_ref, _ = dirs[d]
        q = q_ref[:, head(h)]
        a = lax.dot_general(q, k_ref[:, head(h)], (((1,), (1,)), ((), ())), preferred_element_type=F32)
        qd = (q.astype(F32) * qdec_ref[d, h]).astype(q.dtype)
        scores.append(jnp.concatenate([(a * intra_ref[d, h]).astype(q.dtype), qd], axis=1))
    for (d, h), lhs in zip(chains, scores):
        o_ref = dirs[d][2]
        v = value(d, h)
        rhs = jnp.concatenate([v, state_refs[d * C_HEADS + h][...].astype(v.dtype)], axis=0)
        o_ref[:, head(h)] = jnp.dot(lhs, rhs, preferred_element_type=F32).astype(o_ref.dtype)
    for d, h in chains:
        k_ref = dirs[d][1]
        s_ref = state_refs[d * C_HEADS + h]
        k = k_ref[:, head(h)]
        kd = (k.astype(F32) * kdec_ref[d, h]).astype(k.dtype)
        upd = lax.dot_general(kd, value(d, h), (((0,), (0,)), ((), ())), preferred_element_type=F32)
        s_ref[...] = s_ref[...] * cdec_ref[d, h] + upd


def _ret_call(qr, kr, g12, tables, layer, ctx_rows):
    rows = qr.shape[0]
    w = BRANCH_WIDTH
    nc = rows // CHUNK
    cc = ctx_rows // CHUNK

    def bwd(s):
        return jnp.where(s < cc, cc - 1 - s, nc - 1 + cc - s)

    fwd = lambda s: s
    qk = lambda f: pl.BlockSpec((CHUNK, w), lambda s: (f(s), 0))
    vh = lambda f, c: pl.BlockSpec((CHUNK, w // 2), lambda s: (f(s), c))
    tab = pl.BlockSpec((None, 2, C_HEADS, CHUNK, LANES), lambda s: (layer, 0, 0, 0, 0))
    return pl.pallas_call(
        _ret_kernel,
        grid=(nc,),
        in_specs=[qk(fwd), qk(fwd), vh(fwd, 3), vh(fwd, 4), qk(bwd), qk(bwd), vh(bwd, 3), vh(bwd, 4),
                  tab, tab, tab, tab],
        out_specs=[pl.BlockSpec((CHUNK, w), lambda s: (s, 0)),
                   pl.BlockSpec((CHUNK, w), lambda s: (bwd(s), 0))],
        out_shape=[jax.ShapeDtypeStruct((rows, w), F32), jax.ShapeDtypeStruct((rows, w), F32)],
        scratch_shapes=[pltpu.VMEM((HEAD_DIM, HEAD_DIM), F32) for _ in range(2 * C_HEADS)],
        compiler_params=_params(32, 1),
        name="retention",
    )(qr, kr, g12, g12, qr, kr, g12, g12, *tables)


def _merge_kernel(attn_ref, gm_ref, of_ref, ob_ref, rg_ref, gate_ref, cn_ref, wb_ref, o_ref):
    d = o_ref.shape[1]
    ret_parts = []
    for h in range(C_HEADS):
        sl = slice(h * HEAD_DIM, (h + 1) * HEAD_DIM)
        o = of_ref[:, sl] + ob_ref[:, sl]
        oc = o - jnp.mean(o, axis=-1, keepdims=True)
        y = oc * lax.rsqrt(jnp.mean(oc * oc, axis=-1, keepdims=True) + EPS) * cn_ref[:, sl]
        ret_parts.append((_silu(rg_ref[:, sl].astype(F32)) * y).astype(wb_ref.dtype))
    ret = jnp.concatenate(ret_parts, axis=1)
    branches = (attn_ref[...].astype(wb_ref.dtype), gm_ref[...].astype(wb_ref.dtype), ret)
    acc = None
    for b in range(N_BRANCH):
        proj = jnp.dot(branches[b], wb_ref[b], preferred_element_type=F32)
        term = _sigmoid(gate_ref[:, b * d:(b + 1) * d].astype(F32)) * proj
        acc = term if acc is None else acc + term
    o_ref[...] = acc.astype(o_ref.dtype)


def _merge_call(attn, gm, o_f, o_b, g3, g4, c_norm_flat, wb, layer):
    rows = attn.shape[0]
    w = BRANCH_WIDTH
    d = wb.shape[-1]
    tm = 256
    row = lambda c: pl.BlockSpec((tm, w), lambda i: (i, c))
    return pl.pallas_call(
        _merge_kernel,
        grid=(rows // tm,),
        in_specs=[row(0), row(0), row(0), row(0), row(2),
                  pl.BlockSpec((tm, N_BRANCH * d), lambda i: (i, 0)),
                  pl.BlockSpec((None, 1, w), lambda i: (layer, 0, 0)),
                  pl.BlockSpec((None, N_BRANCH, w, d), lambda i: (layer, 0, 0, 0))],
        out_specs=pl.BlockSpec((tm, d), lambda i: (i, 0)),
        out_shape=jax.ShapeDtypeStruct((rows, d), ACT_DTYPE),
        compiler_params=_params(52, 1),
        name="branch_merge",
    )(attn, gm, o_f, o_b, g3, g4, c_norm_flat, wb)


def _route_kernel(m_ref, wo_ref, z_ref, g_ref, mod_ref, whi_ref, wlo_ref, rb_ref,
                  znew_ref, h_ref, info_ref, cnt_ref, carry_ref, logits_ref):
    i = pl.program_id(0)

    @pl.when(i == 0)
    def _():
        carry_ref[...] = jnp.zeros(carry_ref.shape, carry_ref.dtype)
        logits_ref[...] = jnp.zeros(logits_ref.shape, logits_ref.dtype)

    prev_logits = logits_ref[(i + 1) % 2]
    y = jnp.dot(m_ref[...], wo_ref[...], preferred_element_type=F32)
    _route_stage(prev_logits, jnp.where(i >= 1, 1.0, 0.0), info_ref, cnt_ref, carry_ref)
    z = z_ref[...] + mod_ref[2:3, :] * y
    znew_ref[...] = z
    h = _norm_mod(z, g_ref[...], mod_ref[...], 3, 4)
    h_ref[...] = _pack_halves(h)
    h_hi = h.astype(whi_ref.dtype)
    h_lo = (h - h_hi.astype(F32)).astype(whi_ref.dtype)
    logits_ref[i % 2] = (jnp.dot(h_hi, whi_ref[...], preferred_element_type=F32)
                         + jnp.dot(h_hi, wlo_ref[...], preferred_element_type=F32)
                         + jnp.dot(h_lo, whi_ref[...], preferred_element_type=F32)) + rb_ref[...]


def _route_stage(logits, count_gate, info_ref, cnt_ref, carry_ref):
    tm = logits.shape[0]
    lane = lax.broadcasted_iota(jnp.int32, logits.shape, 1).astype(F32)
    first = lambda hit: jnp.min(jnp.where(hit, lane, 4.0 * LANES), axis=-1, keepdims=True)

    is_g = lane < N_GROUPS
    gl = jnp.where(is_g, logits, NEG_INF)
    gmax = jnp.max(gl, axis=-1, keepdims=True)
    g_sel = first(gl == gmax)
    g_w = 1.0 / jnp.sum(jnp.where(is_g, jnp.exp(gl - gmax), 0.0), axis=-1, keepdims=True)

    e_id = lane - N_GROUPS
    in_group = (e_id >= g_sel * EXPERTS_PER_GROUP) & (e_id < (g_sel + 1.0) * EXPERTS_PER_GROUP)
    el = jnp.where(in_group, logits, NEG_INF)
    m1 = jnp.max(el, axis=-1, keepdims=True)
    i1 = first(el == m1)
    el2 = jnp.where(lane == i1, NEG_INF, el)
    m2 = jnp.max(el2, axis=-1, keepdims=True)
    i2 = first(el2 == m2)
    r = jnp.exp(m2 - m1)
    w1 = g_w / (1.0 + r)
    w2 = g_w * r / (1.0 + r)
    e1 = i1 - N_GROUPS
    e2 = i2 - N_GROUPS

    hot1 = lane == e1
    hot2 = lane == e2
    hot = jnp.where(hot1 | hot2, 1.0, 0.0)
    rr = lax.broadcasted_iota(jnp.int32, (tm, tm), 0)
    cc = lax.broadcasted_iota(jnp.int32, (tm, tm), 1)
    tri = jnp.where(cc < rr, 1.0, 0.0).astype(MXU_DTYPE)
    before = jnp.dot(tri, hot.astype(MXU_DTYPE), preferred_element_type=F32) + carry_ref[0:1, :]
    rank1 = jnp.sum(jnp.where(hot1, before, 0.0), axis=-1, keepdims=True)
    rank2 = jnp.sum(jnp.where(hot2, before, 0.0), axis=-1, keepdims=True)
    carry_ref[0:1, :] = carry_ref[0:1, :] + count_gate * jnp.sum(hot, axis=0, keepdims=True)
    cnt_ref[...] = carry_ref[...]

    info = jnp.where(lane == 0, e1, 0.0)
    info = jnp.where(lane == 1, e2, info)
    info = jnp.where(lane == 2, w1, info)
    info = jnp.where(lane == 3, w2, info)
    info = jnp.where(lane == 4, rank1, info)
    info = jnp.where(lane == 5, rank2, info)
    info_ref[...] = info


def _route_call(merged, wo, z, norm_w, modsel, w_hi, w_lo, rbias, layer, ctx_rows):
    rows, d = z.shape
    tm = 256
    ctx_tiles = ctx_rows // tm
    n_tiles = rows // tm
    cur = lambda i: jnp.minimum(i, n_tiles - 1)
    prev = lambda i: jnp.maximum(i - 1, 0)
    return pl.pallas_call(
        _route_kernel,
        grid=(n_tiles + 1,),
        in_specs=[pl.BlockSpec((tm, d), lambda i: (cur(i), 0)),
                  pl.BlockSpec((None, d, d), lambda i: (layer, 0, 0)),
                  pl.BlockSpec((tm, d), lambda i: (cur(i), 0)),
                  pl.BlockSpec((None, 1, d), lambda i: (layer, 0, 0)),
                  pl.BlockSpec((None, None, MOD_ROWS, d),
                               lambda i: (layer, jnp.where(cur(i) >= ctx_tiles, 1, 0), 0, 0)),
                  pl.BlockSpec((None, d, LANES), lambda i: (layer, 0, 0)),
                  pl.BlockSpec((None, d, LANES), lambda i: (layer, 0, 0)),
                  pl.BlockSpec((None, 1, LANES), lambda i: (layer, 0, 0))],
        out_specs=[pl.BlockSpec((tm, d), lambda i: (cur(i), 0)),
                   pl.BlockSpec((tm, d // 2), lambda i: (cur(i), 0)),
                   pl.BlockSpec((tm, LANES), lambda i: (prev(i), 0)),
                   pl.BlockSpec((8, LANES), lambda i: (0, 0))],
        out_shape=[jax.ShapeDtypeStruct((rows, d), F32),
                   jax.ShapeDtypeStruct((rows, d // 2), PACK_DTYPE),
                   jax.ShapeDtypeStruct((rows, LANES), F32),
                   jax.ShapeDtypeStruct((8, LANES), F32)],
        scratch_shapes=[pltpu.VMEM((8, LANES), F32), pltpu.VMEM((2, tm, LANES), F32)],
        compiler_params=_params(48, 1),
        name="outproj_route",
    )(merged, wo, z, norm_w, modsel, w_hi, w_lo, rbias)


def _dispatch_kernel(pad_end_ref, padded_ref, n_used_ref, dest_ref, h_ref, xs_ref, zero_ref, sem, zero_sem):
    tm = h_ref.shape[0]
    n_blocks = xs_ref.shape[0] // MOE_BLOCK

    @pl.when(pl.program_id(0) == 0)
    def _():
        zero_ref[...] = jnp.zeros(zero_ref.shape, zero_ref.dtype)

        def block_copy(first):
            first = pl.multiple_of(first, MOE_BLOCK)
            return pltpu.make_async_copy(zero_ref, xs_ref.at[pl.ds(first, MOE_BLOCK)], zero_sem)

        def start_unused(b, carry):
            block_copy(b * MOE_BLOCK).start()
            return carry

        def wait_unused(b, carry):
            block_copy(b * MOE_BLOCK).wait()
            return carry

        for e in range(N_EXPERTS):
            @pl.when(padded_ref[e] > 0)
            def _(e=e):
                block_copy(pad_end_ref[e] - MOE_BLOCK).start()
        lax.fori_loop(n_used_ref[0], n_blocks, start_unused, 0)
        for e in range(N_EXPERTS):
            @pl.when(padded_ref[e] > 0)
            def _(e=e):
                block_copy(pad_end_ref[e] - MOE_BLOCK).wait()
        lax.fori_loop(n_used_ref[0], n_blocks, wait_unused, 0)

    def row_copy(t, slot):
        return pltpu.make_async_copy(h_ref.at[pl.ds(t, 1)], xs_ref.at[pl.ds(slot, 1)], sem)

    def start(t, carry):
        row_copy(t, dest_ref[0, 0, 2 * t]).start()
        row_copy(t, dest_ref[0, 0, 2 * t + 1]).start()
        return carry

    lax.fori_loop(0, tm, start, 0, unroll=16)
    all_rows = pltpu.make_async_copy(h_ref, xs_ref.at[pl.ds(0, tm)], sem)
    all_rows.wait()
    all_rows.wait()


def _dispatch_call(pad_end, padded, n_used, dest3, h, n_slots):
    rows, d = h.shape
    tm = dest3.shape[2] // 2
    grid_spec = pltpu.PrefetchScalarGridSpec(
        num_scalar_prefetch=3,
        grid=(rows // tm,),
        in_specs=[pl.BlockSpec((1, 1, 2 * tm), lambda i, pe, pd, nu: (i, 0, 0), memory_space=pltpu.SMEM),
                  pl.BlockSpec((tm, d), lambda i, pe, pd, nu: (i, 0))],
        out_specs=pl.BlockSpec(memory_space=pl.ANY),
        scratch_shapes=[pltpu.VMEM((MOE_BLOCK, d), h.dtype),
                        pltpu.SemaphoreType.DMA(()), pltpu.SemaphoreType.DMA(())])
    return pl.pallas_call(
        _dispatch_kernel,
        grid_spec=grid_spec,
        out_shape=jax.ShapeDtypeStruct((n_slots, d), h.dtype),
        compiler_params=_params(32, 1),
        name="moe_dispatch",
    )(pad_end, padded, n_used, dest3, h)


def _expert_kernel(be_ref, nu_ref, first_ref, slot_ref, next_ref, xs_ref, w1_hbm, w2_hbm, ys_ref,
                   w1f_ref, w2f_ref, w1bf_ref, w2bf_ref, sem, *, layer):
    i = pl.program_id(0)
    used = i < nu_ref[0]

    def fetch(e, s):
        return (pltpu.make_async_copy(w1_hbm.at[layer, e], w1f_ref.at[s], sem.at[s]),
                pltpu.make_async_copy(w2_hbm.at[layer, e], w2f_ref.at[s], sem.at[s]))

    @pl.when(i == 0)
    def _():
        for cp in fetch(be_ref[0], slot_ref[0]):
            cp.start()

    @pl.when(jnp.logical_and(used, first_ref[i] == 1))
    def _():
        s = slot_ref[i]
        for cp in fetch(be_ref[i], s):
            cp.wait()

        @pl.when(next_ref[i] >= 0)
        def _():
            for cp in fetch(next_ref[i], 1 - s):
                cp.start()

        w1bf_ref[...] = w1f_ref[s].astype(w1bf_ref.dtype)
        w2bf_ref[...] = w2f_ref[s].astype(w2bf_ref.dtype)

    @pl.when(used)
    def _():
        de = w2bf_ref.shape[0]
        x = jnp.concatenate(_unpack_halves(xs_ref[...]), axis=1).astype(w1bf_ref.dtype)
        hcat = jnp.dot(x, w1bf_ref[...], preferred_element_type=F32)
        act = _silu(hcat[:, :de]) * hcat[:, de:]
        ys_ref[...] = _pack_halves(jnp.dot(act.astype(w2bf_ref.dtype), w2bf_ref[...], preferred_element_type=F32))

    @pl.when(jnp.logical_not(used))
    def _():
        ys_ref[...] = jnp.zeros(ys_ref.shape, ys_ref.dtype)


def _expert_call(plan, xs, w_e1, w_e2, layer):
    slots, dp = xs.shape
    d = 2 * dp
    de = w_e2.shape[2]
    nb = slots // MOE_BLOCK
    blk = lambda i, be, nu, *_: (jnp.minimum(i, nu[0] - 1), 0)
    grid_spec = pltpu.PrefetchScalarGridSpec(
        num_scalar_prefetch=5,
        grid=(nb,),
        in_specs=[pl.BlockSpec((MOE_BLOCK, dp), blk),
                  pl.BlockSpec(memory_space=pl.ANY),
                  pl.BlockSpec(memory_space=pl.ANY)],
        out_specs=pl.BlockSpec((MOE_BLOCK, dp), lambda i, *_: (i, 0)),
        scratch_shapes=[pltpu.VMEM((2, d, 2 * de), w_e1.dtype), pltpu.VMEM((2, de, d), w_e2.dtype),
                        pltpu.VMEM((d, 2 * de), MXU_DTYPE), pltpu.VMEM((de, d), MXU_DTYPE),
                        pltpu.SemaphoreType.DMA((2,))])
    return pl.pallas_call(
        functools.partial(_expert_kernel, layer=layer),
        grid_spec=grid_spec,
        out_shape=jax.ShapeDtypeStruct((slots, dp), PACK_DTYPE),
        compiler_params=_params(48, 1),
        name="moe_experts",
    )(plan["block_e"], plan["n_used"], plan["first"], plan["slot"], plan["next_e"], xs, w_e1, w_e2)


def _combine_kernel(dest_ref, dest_next_ref, ys_ref, info_ref, z_ref, mod_ref, *rest, emit_next):
    if emit_next:
        gn_ref, modn_ref, o_ref, hn_ref, buf_ref, sem = rest
    else:
        o_ref, buf_ref, sem = rest
    i = pl.program_id(0)
    tm = z_ref.shape[0]
    slot = i % 2

    def issue(d_ref, s):
        for t in range(tm):
            for k in range(2):
                pltpu.make_async_copy(ys_ref.at[pl.ds(d_ref[0, 0, 2 * t + k], 1)],
                                      buf_ref.at[s, k, pl.ds(t, 1)], sem.at[s]).start()

    @pl.when(i == 0)
    def _():
        issue(dest_ref, 0)

    @pl.when(i + 1 < pl.num_programs(0))
    def _():
        issue(dest_next_ref, 1 - slot)

    for k in range(2):
        pltpu.make_async_copy(ys_ref.at[pl.ds(0, tm)], buf_ref.at[slot, k], sem.at[slot]).wait()
    info = info_ref[...]
    lane = lax.broadcasted_iota(jnp.int32, info.shape, 1)
    w1 = jnp.sum(jnp.where(lane == 2, info, 0.0), axis=-1, keepdims=True)
    w2 = jnp.sum(jnp.where(lane == 3, info, 0.0), axis=-1, keepdims=True)
    lo1, hi1 = _unpack_halves(buf_ref[slot, 0])
    lo2, hi2 = _unpack_halves(buf_ref[slot, 1])
    y = jnp.concatenate([lo1 * w1 + lo2 * w2, hi1 * w1 + hi2 * w2], axis=1)
    z = z_ref[...] + mod_ref[5:6, :] * y
    o_ref[...] = z
    if emit_next:
        hn_ref[...] = _norm_mod(z, gn_ref[...], modn_ref[...], 0, 1).astype(hn_ref.dtype)


def _combine_call(dest3, ys, info, z, modsel, layer, ctx_rows, next_norm_w=None):
    rows, d = z.shape
    tm = dest3.shape[2] // 2
    ctx_tiles = ctx_rows // tm
    n_tiles = rows // tm
    emit_next = next_norm_w is not None
    mod_spec = lambda l: pl.BlockSpec((None, None, MOD_ROWS, d),
                                      lambda i: (l, jnp.where(i >= ctx_tiles, 1, 0), 0, 0))
    row_spec = pl.BlockSpec((tm, d), lambda i: (i, 0))
    in_specs = [pl.BlockSpec((1, 1, 2 * tm), lambda i: (i, 0, 0), memory_space=pltpu.SMEM),
                pl.BlockSpec((1, 1, 2 * tm), lambda i: (jnp.minimum(i + 1, n_tiles - 1), 0, 0),
                             memory_space=pltpu.SMEM),
                pl.BlockSpec(memory_space=pl.ANY),
                pl.BlockSpec((tm, LANES), lambda i: (i, 0)),
                row_spec,
                mod_spec(layer)]
    args = [dest3, dest3, ys, info, z, modsel]
    if emit_next:
        out_specs = [row_spec]
        out_shape = [jax.ShapeDtypeStruct((rows, d), F32)]
    else:
        out_specs = [pl.BlockSpec((tm, d), lambda i: (jnp.maximum(i - ctx_tiles, 0), 0))]
        out_shape = [jax.ShapeDtypeStruct((rows - ctx_rows, d), F32)]
    if emit_next:
        in_specs += [pl.BlockSpec((None, 1, d), lambda i: (layer + 1, 0, 0)), mod_spec(layer + 1)]
        args += [next_norm_w, modsel]
        out_specs.append(row_spec)
        out_shape.append(jax.ShapeDtypeStruct((rows, d), ACT_DTYPE))
    return pl.pallas_call(
        functools.partial(_combine_kernel, emit_next=emit_next),
        grid=(n_tiles,),
        in_specs=in_specs,
        out_specs=out_specs,
        out_shape=out_shape,
        scratch_shapes=[pltpu.VMEM((2, 2, tm, ys.shape[1]), ys.dtype), pltpu.SemaphoreType.DMA((2,))],
        compiler_params=_params(40, 1),
        name="moe_combine",
    )(*args)


def _rope_tables(n, ctx_rows):
    rows = n // GRID_W
    row = jnp.repeat(jnp.arange(rows, dtype=F32), GRID_W)
    col = jnp.tile(jnp.arange(GRID_W, dtype=F32), rows)
    nq = HEAD_DIM // 4
    inv = ROPE_BASE ** (-jnp.arange(nq, dtype=F32) / nq)
    ar, ac = row[:, None] * inv, col[:, None] * inv
    cos = jnp.concatenate([jnp.cos(ar), jnp.cos(ar), jnp.cos(ac), jnp.cos(ac)], axis=1)
    sin = jnp.concatenate([-jnp.sin(ar), jnp.sin(ar), -jnp.sin(ac), jnp.sin(ac)], axis=1)
    cos = jnp.concatenate([jnp.ones((ctx_rows, HEAD_DIM), F32), cos], axis=0)
    sin = jnp.concatenate([jnp.zeros((ctx_rows, HEAD_DIM), F32), sin], axis=0)
    return cos, sin


def _retention_tables(c_decay_fwd, c_decay_bwd):
    lg_f = jax.nn.log_sigmoid(c_decay_fwd.astype(F32))[:, :, None, None]
    lg_b = jax.nn.log_sigmoid(c_decay_bwd.astype(F32))[:, :, None, None]
    idx = jnp.arange(CHUNK, dtype=F32)
    diff = idx[:, None] - idx[None, :]
    ones = jnp.ones((CHUNK, CHUNK), F32)
    t_col = idx[:, None] * ones
    intra_f = jnp.where(diff >= 0, jnp.exp(lg_f * jnp.maximum(diff, 0.0)), 0.0)
    intra_b = jnp.where(diff <= 0, jnp.exp(lg_b * jnp.maximum(-diff, 0.0)), 0.0)
    qdec_f = jnp.exp(lg_f * (t_col + 1.0))
    qdec_b = jnp.exp(lg_b * (CHUNK - t_col))
    kdec_f = jnp.exp(lg_f * (CHUNK - 1.0 - t_col))
    kdec_b = jnp.exp(lg_b * t_col)
    cdec_f = jnp.exp(lg_f * CHUNK) * ones
    cdec_b = jnp.exp(lg_b * CHUNK) * ones
    pair = lambda a, b: jnp.stack([a, b], axis=1)
    return (pair(intra_f, intra_b), pair(qdec_f, qdec_b), pair(kdec_f, kdec_b), pair(cdec_f, cdec_b))


def _moe_plan(info, counts_row, n_slots_blocks):
    e = info[:, 0:2].astype(jnp.int32)
    rank = info[:, 4:6].astype(jnp.int32)
    counts = counts_row[:N_EXPERTS].astype(jnp.int32)
    padded = (counts + MOE_BLOCK - 1) // MOE_BLOCK * MOE_BLOCK
    pad_end = jnp.cumsum(padded)
    pad_start = pad_end - padded
    hit = e[:, :, None] == jnp.arange(N_EXPERTS, dtype=jnp.int32)
    dest = jnp.sum(jnp.where(hit, pad_start, 0), axis=-1) + rank
    n_used = pad_end[-1] // MOE_BLOCK
    blocks = jnp.arange(n_slots_blocks, dtype=jnp.int32)
    first_slot = jnp.minimum(blocks, n_used - 1) * MOE_BLOCK
    block_e = jnp.sum((pad_end[None, :] <= first_slot[:, None]).astype(jnp.int32), axis=1)
    ids = jnp.arange(N_EXPERTS, dtype=jnp.int32)
    nonempty = counts > 0
    slot_of = (jnp.cumsum(nonempty.astype(jnp.int32)) - 1) % 2
    later = nonempty[None, :] & (ids[None, :] > ids[:, None])
    next_of = jnp.min(jnp.where(later, ids[None, :], N_EXPERTS), axis=1)
    next_of = jnp.where(next_of == N_EXPERTS, -1, next_of)
    pick = lambda table: jnp.sum(jnp.where(block_e[:, None] == ids[None, :], table[None, :], 0), axis=1)
    prev_e = jnp.concatenate([jnp.full((1,), -1, jnp.int32), block_e[:-1]])
    first = ((block_e != prev_e) & (blocks < n_used)).astype(jnp.int32)
    i32 = lambda a: a.astype(jnp.int32)
    return dict(dest=dest, block_e=i32(block_e), n_used=i32(n_used.reshape(1)), pad_end=i32(pad_end),
                padded=i32(padded), first=first, slot=i32(pick(slot_of)), next_e=i32(pick(next_of)))


def kernel(x, c, ctx, c_ctx, norm_mix, norm_ffn, w_ada, b_ada, w_in, a_q_norm, a_k_norm, a_sink,
           b_norm, b_spatial, b_spatial_bias, c_decay_fwd, c_decay_bwd, c_norm, w_branch, w_out,
           w_router_group, b_router_group, w_router_expert, b_router_expert, w_expert_in, w_expert_out):
    batch, n, d = x.shape
    ctx_rows = ctx.shape[1]
    depth = w_in.shape[0]
    assert batch == 1 and ctx_rows % 256 == 0 and n % 256 == 0
    rows = ctx_rows + n

    cond = jnp.stack([c[0], c_ctx], axis=0)
    cond_b = jnp.broadcast_to(cond[:, :, None], (2, d, LANES))
    mod = _ada_call(cond_b, w_ada, b_ada)[:, :2].reshape(depth, 2, N_MOD, d)
    modsel = jnp.pad(mod[:, ::-1], ((0, 0), (0, 0), (0, MOD_ROWS - N_MOD), (0, 0)))

    cos_t, sin_t = _rope_tables(n, ctx_rows)
    ret_tables = _retention_tables(c_decay_fwd, c_decay_bwd)
    sink_col = jnp.broadcast_to(a_sink.astype(F32).reshape(depth, A_KV_HEADS, A_GROUP, 1, 1),
                                (depth, A_KV_HEADS, A_GROUP, A_BLOCK, 1)).reshape(depth, A_KV_HEADS, A_GROUP * A_BLOCK, 1)
    ws = b_spatial.astype(MXU_DTYPE)
    bias_b = jnp.broadcast_to(b_spatial_bias.astype(F32)[:, :, :, None], (depth, B_GROUPS, CHUNK, LANES))
    wb = w_branch.astype(MXU_DTYPE)
    wo = w_out.astype(MXU_DTYPE)
    w_r = jnp.concatenate([w_router_group, w_router_expert], axis=-1).astype(F32)
    w_r = jnp.pad(w_r, ((0, 0), (0, 0), (0, LANES - w_r.shape[-1])))
    w_r_hi = w_r.astype(MXU_DTYPE)
    w_r_lo = (w_r - w_r_hi.astype(F32)).astype(MXU_DTYPE)
    b_r = jnp.concatenate([b_router_group, b_router_expert], axis=-1).astype(F32)
    b_r = jnp.pad(b_r, ((0, 0), (0, LANES - b_r.shape[-1]))).reshape(depth, 1, LANES)
    norm_mix3 = norm_mix.reshape(depth, 1, d)
    norm_ffn3 = norm_ffn.reshape(depth, 1, d)
    lane_ids = jnp.arange(HEAD_DIM)
    partner = jnp.where((lane_ids // 32) % 2 == 0, lane_ids + 32, lane_ids - 32)
    perm = (lane_ids[:, None] == partner[None, :]).astype(ACT_DTYPE)
    a_q_norm3 = jnp.stack([a_q_norm, a_q_norm[:, partner]], axis=1).astype(F32)
    a_k_norm3 = jnp.stack([a_k_norm, a_k_norm[:, partner]], axis=1).astype(F32)
    b_norm3 = b_norm.reshape(depth, 1, BRANCH_WIDTH)
    c_norm3 = c_norm.reshape(depth, 1, BRANCH_WIDTH)

    kvw = A_KV_HEADS * HEAD_DIM
    w = BRANCH_WIDTH
    n_assign = rows * 2
    n_slot_blocks = -(-(n_assign + N_EXPERTS * (MOE_BLOCK - 1)) // MOE_BLOCK)
    tok_tile = 256
    disp_tile = _row_tile(rows, (1056, 768, 256))

    z = jnp.concatenate([ctx[0], x[0]], axis=0)
    h = _norm_mod_call(z, norm_mix3, modsel, 0, ctx_rows)
    for l in range(depth):
        g12 = _proj_call(h, w_in, l, 0, 2 * kvw + 2 * w)
        g3 = _proj_call(h, w_in, l, 2 * kvw + 2 * w, 5 * w)
        g4 = _proj_call(h, w_in, l, 2 * kvw + 7 * w, N_BRANCH * d)
        kn, qn, kr, qr = _prep_call(g12, g3, cos_t, sin_t, a_q_norm3, a_k_norm3, perm, l)
        attn = _attn_call(qn, kn, g12, sink_col, l, ctx_rows)
        gm = _gmlp_call(g3, b_norm3, ws, bias_b, l)
        o_f, o_b = _ret_call(qr, kr, g12, ret_tables, l, ctx_rows)
        merged = _merge_call(attn, gm, o_f, o_b, g3, g4, c_norm3, wb, l)
        z, h2, info, counts = _route_call(merged, wo, z, norm_ffn3, modsel, w_r_hi, w_r_lo, b_r, l, ctx_rows)
        plan = _moe_plan(info, counts[0], n_slot_blocks)
        dest3 = plan["dest"].reshape(rows // tok_tile, 1, 2 * tok_tile)
        dest3_disp = plan["dest"].reshape(rows // disp_tile, 1, 2 * disp_tile)
        xs = _dispatch_call(plan["pad_end"], plan["padded"], plan["n_used"], dest3_disp, h2,
                            n_slot_blocks * MOE_BLOCK)
        ys = _expert_call(plan, xs, w_expert_in, w_expert_out, l)
        if l + 1 < depth:
            z, h = _combine_call(dest3, ys, info, z, modsel, l, ctx_rows, next_norm_w=norm_mix3)
        else:
            (z_latent,) = _combine_call(dest3, ys, info, z, modsel, l, ctx_rows)
    return z_latent[None]
```

```python
import functools
import math

import jax
import jax.numpy as jnp
from jax import lax
from jax.experimental import pallas as pl
from jax.experimental.pallas import tpu as pltpu

F32 = jnp.float32
MXU_DTYPE = jnp.bfloat16
ACT_DTYPE = jnp.bfloat16

LANES = 128
HEAD_DIM = 128
GRID_W = 64
ROPE_BASE = 10000.0
EPS = 1e-6
NEG_INF = -1e30
A_Q_HEADS = 8
A_KV_HEADS = 2
A_GROUP = A_Q_HEADS // A_KV_HEADS
A_BLOCK = 128
B_GROUPS = 8
C_HEADS = 8
CHUNK = 128
N_GROUPS = 4
EXPERTS_PER_GROUP = 8
N_EXPERTS = N_GROUPS * EXPERTS_PER_GROUP
D_EXPERT = 512
BRANCH_WIDTH = 1024
N_BRANCH = 3
MOE_BLOCK = 256
N_MOD = 6
MOD_ROWS = 8
MIB = 1024 * 1024


def _params(vmem_mib, n_grid, **kw):
    return pltpu.CompilerParams(dimension_semantics=("arbitrary",) * n_grid,
                                vmem_limit_bytes=vmem_mib * MIB, **kw)


def _sigmoid(x):
    return 0.5 + 0.5 * jnp.tanh(0.5 * x)


def _silu(x):
    return x * _sigmoid(x)


PACK_DTYPE = jnp.uint32


def _pack_halves(x):
    n = x.shape[1] // 2
    rounded = lambda v: lax.bitcast_convert_type(v.astype(jnp.bfloat16).astype(F32), PACK_DTYPE)
    return rounded(x[:, n:]) | (rounded(x[:, :n]) >> 16)


def _unpack_halves(p):
    lo = lax.bitcast_convert_type(p << 16, F32)
    hi = lax.bitcast_convert_type(p & jnp.asarray(0xFFFF0000, PACK_DTYPE), F32)
    return lo, hi


def _gelu_tanh(x):
    return 0.5 * x * (1.0 + jnp.tanh(math.sqrt(2.0 / math.pi) * (x + 0.044715 * (x * x * x))))


def _ada_kernel(c_ref, w_ref, b_ref, o_ref):
    tn = w_ref.shape[1]
    s0 = _silu(c_ref[0])
    s1 = _silu(c_ref[1])
    o_ref[...] = jnp.zeros(o_ref.shape, o_ref.dtype)
    for j in range(tn // LANES):
        sl = slice(j * LANES, (j + 1) * LANES)
        wj = w_ref[:, sl]
        o_ref[0:1, sl] = jnp.sum(wj * s0, axis=0, keepdims=True) + b_ref[:, sl]
        o_ref[1:2, sl] = jnp.sum(wj * s1, axis=0, keepdims=True) + b_ref[:, sl]


def _ada_call(cond_b, w_ada, b_ada):
    depth, k, n = w_ada.shape
    tn = 2048
    return pl.pallas_call(
        _ada_kernel,
        grid=(depth, n // tn),
        in_specs=[pl.BlockSpec((2, k, LANES), lambda l, j: (0, 0, 0)),
                  pl.BlockSpec((None, k, tn), lambda l, j: (l, 0, j)),
                  pl.BlockSpec((None, 1, tn), lambda l, j: (l, 0, j))],
        out_specs=pl.BlockSpec((None, 8, tn), lambda l, j: (l, 0, j)),
        out_shape=jax.ShapeDtypeStruct((depth, 8, n), F32),
        compiler_params=_params(48, 2),
        name="adaln",
    )(cond_b, w_ada, b_ada.reshape(depth, 1, n))


def _norm_mod(z, g, mod, shift_row, scale_row):
    r = lax.rsqrt(jnp.mean(z * z, axis=-1, keepdims=True) + EPS)
    return (z * r * g) * (1.0 + mod[scale_row:scale_row + 1, :]) + mod[shift_row:shift_row + 1, :]


def _first_norm_kernel(ctx_ref, x_ref, g_ref, mod_ref, z_ref, h_ref, *, ctx_tiles):
    def emit(src_ref):
        z = src_ref[...]
        z_ref[...] = z
        h_ref[...] = _norm_mod(z, g_ref[...], mod_ref[...], 0, 1).astype(h_ref.dtype)

    @pl.when(pl.program_id(0) < ctx_tiles)
    def _():
        emit(ctx_ref)

    @pl.when(pl.program_id(0) >= ctx_tiles)
    def _():
        emit(x_ref)


def _first_norm_call(ctx2, x2, norm_w, modsel):
    ctx_rows, d = ctx2.shape
    rows = ctx_rows + x2.shape[0]
    tm = 256
    ctx_tiles = ctx_rows // tm
    row_spec = pl.BlockSpec((tm, d), lambda i: (i, 0))
    return pl.pallas_call(
        functools.partial(_first_norm_kernel, ctx_tiles=ctx_tiles),
        grid=(rows // tm,),
        in_specs=[pl.BlockSpec((tm, d), lambda i: (jnp.minimum(i, ctx_tiles - 1), 0)),
                  pl.BlockSpec((tm, d), lambda i: (jnp.maximum(i - ctx_tiles, 0), 0)),
                  pl.BlockSpec((None, 1, d), lambda i: (0, 0, 0)),
                  pl.BlockSpec((None, None, MOD_ROWS, d),
                               lambda i: (0, jnp.where(i >= ctx_tiles, 1, 0), 0, 0))],
        out_specs=[row_spec, row_spec],
        out_shape=[jax.ShapeDtypeStruct((rows, d), F32), jax.ShapeDtypeStruct((rows, d), ACT_DTYPE)],
        compiler_params=_params(32, 1),
        name="stack_norm_mod",
    )(ctx2, x2, norm_w, modsel)


def _proj_kernel(h_ref, w_ref, o_ref, wbf_ref):
    @pl.when(pl.program_id(1) == 0)
    def _():
        wbf_ref[...] = w_ref[...].astype(wbf_ref.dtype)

    o_ref[...] = jnp.dot(h_ref[...], wbf_ref[...], preferred_element_type=F32).astype(o_ref.dtype)


def _row_tile(rows, pref):
    for t in pref:
        if rows % t == 0:
            return t
    raise ValueError(f"no row tile for {rows}")


def _proj_call(h, w, layer, col_off, ncols):
    rows, k = h.shape
    tn = next(t for t in (1536, 1280, 1024, 512) if col_off % t == 0 and ncols % t == 0)
    tm = _row_tile(rows, (768, 512, 256))
    off = col_off // tn
    return pl.pallas_call(
        _proj_kernel,
        grid=(ncols // tn, rows // tm),
        in_specs=[pl.BlockSpec((tm, k), lambda j, i: (i, 0)),
                  pl.BlockSpec((None, k, tn), lambda j, i: (layer, 0, off + j))],
        out_specs=pl.BlockSpec((tm, tn), lambda j, i: (i, j)),
        out_shape=jax.ShapeDtypeStruct((rows, ncols), ACT_DTYPE),
        scratch_shapes=[pltpu.VMEM((k, tn), MXU_DTYPE)],
        compiler_params=_params(52, 2),
        name="proj_in",
    )(h, w)


def _prep_kernel(ak_ref, ck0_ref, ck1_ref, aq_ref, cq_ref, cos_ref, sin_ref, qn_ref, kn_ref, perm_ref,
                 okn_ref, oqn_ref, okr_ref, oqr_ref):
    cos = cos_ref[...]
    sin = sin_ref[...]
    perm = perm_ref[...]
    scale = HEAD_DIM ** -0.5

    def swap(x):
        return jnp.dot(x, perm, preferred_element_type=F32)

    def norm_rope(x, g_ref, out_scale):
        xf = x.astype(F32)
        r = lax.rsqrt(jnp.mean(xf * xf, axis=-1, keepdims=True) + EPS) * out_scale
        return (xf * (g_ref[0:1, :] * cos) + swap(x) * (g_ref[1:2, :] * sin)) * r

    for h in range(A_KV_HEADS):
        sl = slice(h * HEAD_DIM, (h + 1) * HEAD_DIM)
        okn_ref[:, sl] = norm_rope(ak_ref[:, sl], kn_ref, 1.0).astype(okn_ref.dtype)
    for h in range(A_Q_HEADS):
        sl = slice(h * HEAD_DIM, (h + 1) * HEAD_DIM)
        oqn_ref[:, sl] = norm_rope(aq_ref[:, sl], qn_ref, scale).astype(oqn_ref.dtype)
    for h in range(C_HEADS):
        sl = slice(h * HEAD_DIM, (h + 1) * HEAD_DIM)
        half = C_HEADS // 2
        k = (ck0_ref if h < half else ck1_ref)[:, (h % half) * HEAD_DIM:(h % half + 1) * HEAD_DIM]
        q = cq_ref[:, sl]
        okr_ref[:, sl] = ((k.astype(F32) * cos + swap(k) * sin) * scale).astype(okr_ref.dtype)
        oqr_ref[:, sl] = (q.astype(F32) * cos + swap(q) * sin).astype(oqr_ref.dtype)


def _prep_call(g12, g3, cos_t, sin_t, a_q_norm, a_k_norm, perm, layer):
    rows = g12.shape[0]
    tm = _row_tile(rows, (768, 512, 256))
    w = BRANCH_WIDTH
    kvw = A_KV_HEADS * HEAD_DIM
    return pl.pallas_call(
        _prep_kernel,
        grid=(rows // tm,),
        in_specs=[pl.BlockSpec((tm, kvw), lambda i: (i, 0)),
                  pl.BlockSpec((tm, w // 2), lambda i: (i, 1)),
                  pl.BlockSpec((tm, w // 2), lambda i: (i, 2)),
                  pl.BlockSpec((tm, w), lambda i: (i, 0)),
                  pl.BlockSpec((tm, w), lambda i: (i, 1)),
                  pl.BlockSpec((tm, LANES), lambda i: (i, 0)),
                  pl.BlockSpec((tm, LANES), lambda i: (i, 0)),
                  pl.BlockSpec((None, 2, HEAD_DIM), lambda i: (layer, 0, 0)),
                  pl.BlockSpec((None, 2, HEAD_DIM), lambda i: (layer, 0, 0)),
                  pl.BlockSpec((HEAD_DIM, HEAD_DIM), lambda i: (0, 0))],
        out_specs=[pl.BlockSpec((tm, kvw), lambda i: (i, 0)),
                   pl.BlockSpec((tm, w), lambda i: (i, 0)),
                   pl.BlockSpec((tm, w), lambda i: (i, 0)),
                   pl.BlockSpec((tm, w), lambda i: (i, 0))],
        out_shape=[jax.ShapeDtypeStruct((rows, kvw), ACT_DTYPE),
                   jax.ShapeDtypeStruct((rows, w), ACT_DTYPE),
                   jax.ShapeDtypeStruct((rows, w), ACT_DTYPE),
                   jax.ShapeDtypeStruct((rows, w), ACT_DTYPE)],
        compiler_params=_params(40, 1),
        name="qk_prep",
    )(g12, g12, g12, g3, g3, cos_t, sin_t, a_q_norm, a_k_norm, perm)


def _attn_kernel(q_ref, kp_ref, km_ref, kn_ref, kc_ref, vp_ref, vm_ref, vn_ref, vc_ref, sink_ref,
                 o_ref, *, ctx_blocks, n_blocks):
    blk = A_BLOCK
    n_keys = 3 * blk + kc_ref.shape[0]
    row = lax.broadcasted_iota(jnp.int32, (A_GROUP * blk, n_keys), 0) % blk
    col = lax.broadcasted_iota(jnp.int32, (A_GROUP * blk, n_keys), 1)
    in_window = (row >= jnp.maximum(col - 2 * blk, 0)) & (row <= jnp.where(col < blk, col, blk - 1))
    is_ctx_key = col >= 3 * blk

    def valid_mask(rb):
        is_lat = rb >= ctx_blocks
        c_lo = jnp.where(rb >= ctx_blocks + 1, 0, blk)
        c_hi = jnp.where(rb <= n_blocks - 2, 3 * blk, 2 * blk)
        c_lo = jnp.where(is_lat, c_lo, 3 * blk)
        c_hi = jnp.where(is_lat, c_hi, 0)
        return ((col >= c_lo) & (col < c_hi) & in_window) | is_ctx_key

    hs = lambda hk: slice(hk * HEAD_DIM, (hk + 1) * HEAD_DIM)
    qs = lambda hk, g: slice((hk * A_GROUP + g) * HEAD_DIM, (hk * A_GROUP + g + 1) * HEAD_DIM)
    lo_rows, hi_rows = slice(0, blk), slice(blk, 2 * blk)

    def band(j, prev_ref, mid_ref, next_ref, ctx_ref, hk):
        parts = ((prev_ref[:, hs(hk)], mid_ref[lo_rows, hs(hk)], mid_ref[hi_rows, hs(hk)]) if j == 0 else
                 (mid_ref[lo_rows, hs(hk)], mid_ref[hi_rows, hs(hk)], next_ref[:, hs(hk)]))
        return jnp.concatenate(parts + (ctx_ref[:, hs(hk)],), axis=0)

    chains = [(j, hk) for j in range(2) for hk in range(A_KV_HEADS)]
    masks = [valid_mask(2 * pl.program_id(0) + j) for j in range(2)]
    scores = []
    for j, hk in chains:
        q = jnp.concatenate([q_ref[j * blk:(j + 1) * blk, qs(hk, g)] for g in range(A_GROUP)], axis=0)
        k = band(j, kp_ref, km_ref, kn_ref, kc_ref, hk)
        scores.append(lax.dot_general(q, k, (((1,), (1,)), ((), ())), preferred_element_type=F32))
    probs, denoms = [], []
    for (j, hk), s in zip(chains, scores):
        s = jnp.where(masks[j], s, NEG_INF)
        sink = sink_ref[hk]
        m = jnp.maximum(jnp.max(s, axis=-1, keepdims=True), sink)
        p = jnp.exp(s - m)
        denoms.append(jnp.sum(p, axis=-1, keepdims=True) + jnp.exp(sink - m))
        probs.append(p.astype(vm_ref.dtype))
    for (j, hk), p, den in zip(chains, probs, denoms):
        v = band(j, vp_ref, vm_ref, vn_ref, vc_ref, hk)
        o = jnp.dot(p, v, preferred_element_type=F32) / den
        for g in range(A_GROUP):
            o_ref[j * blk:(j + 1) * blk, qs(hk, g)] = o[g * blk:(g + 1) * blk].astype(o_ref.dtype)


def _attn_call(qn, kn, g12, sink_col, layer, ctx_rows):
    rows = qn.shape[0]
    blk = A_BLOCK
    nb = rows // blk
    cb = ctx_rows // blk
    assert nb % 2 == 0 and cb % 2 == 0
    qw = A_Q_HEADS * HEAD_DIM
    kvw = A_KV_HEADS * HEAD_DIM
    before = lambda i: jnp.maximum(2 * i - 1, 0)
    after = lambda i: jnp.minimum(2 * i + 2, nb - 1)
    edge = lambda f, c: pl.BlockSpec((blk, kvw), lambda i: (f(i), c))
    pair = lambda c: pl.BlockSpec((2 * blk, kvw), lambda i: (i, c))
    return pl.pallas_call(
        functools.partial(_attn_kernel, ctx_blocks=cb, n_blocks=nb),
        grid=(nb // 2,),
        in_specs=[pl.BlockSpec((2 * blk, qw), lambda i: (i, 0)),
                  edge(before, 0), pair(0), edge(after, 0),
                  pl.BlockSpec((ctx_rows, kvw), lambda i: (0, 0)),
                  edge(before, 1), pair(1), edge(after, 1),
                  pl.BlockSpec((ctx_rows, kvw), lambda i: (0, 1)),
                  pl.BlockSpec((None, A_KV_HEADS, A_GROUP * blk, 1), lambda i: (layer, 0, 0, 0))],
        out_specs=pl.BlockSpec((2 * blk, qw), lambda i: (i, 0)),
        out_shape=jax.ShapeDtypeStruct((rows, qw), ACT_DTYPE),
        compiler_params=_params(40, 1),
        name="window_attn",
    )(qn, kn, kn, kn, kn, g12, g12, g12, g12, sink_col)


GMLP_CHUNKS_PER_STEP = 2


def _gmlp_kernel(u_ref, v_ref, bn_ref, ws_ref, bias_ref, o_ref):
    chunks = range(u_ref.shape[0] // CHUNK)
    rows = lambda c: slice(c * CHUNK, (c + 1) * CHUNK)
    for g in range(B_GROUPS):
        sl = slice(g * LANES, (g + 1) * LANES)
        normed = []
        for c in chunks:
            v = _gelu_tanh(v_ref[rows(c), sl].astype(F32))
            vc = v - jnp.mean(v, axis=-1, keepdims=True)
            vh = vc * lax.rsqrt(jnp.mean(vc * vc, axis=-1, keepdims=True) + EPS) * bn_ref[:, sl]
            normed.append(vh.astype(ws_ref.dtype))
        mixed = jnp.dot(ws_ref[g], jnp.concatenate(normed, axis=1), preferred_element_type=F32)
        for c in chunks:
            m = mixed[:, c * LANES:(c + 1) * LANES] + bias_ref[g]
            o_ref[rows(c), sl] = (_gelu_tanh(u_ref[rows(c), sl].astype(F32)) * m).astype(o_ref.dtype)


def _gmlp_call(g3, b_norm_flat, ws, bias_b, layer):
    rows = g3.shape[0]
    w = BRANCH_WIDTH
    tm = GMLP_CHUNKS_PER_STEP * CHUNK
    return pl.pallas_call(
        _gmlp_kernel,
        grid=(rows // tm,),
        in_specs=[pl.BlockSpec((tm, w), lambda i: (i, 3)),
                  pl.BlockSpec((tm, w), lambda i: (i, 4)),
                  pl.BlockSpec((None, 1, w), lambda i: (layer, 0, 0)),
                  pl.BlockSpec((None, B_GROUPS, CHUNK, CHUNK), lambda i: (layer, 0, 0, 0)),
                  pl.BlockSpec((None, B_GROUPS, CHUNK, LANES), lambda i: (layer, 0, 0, 0))],
        out_specs=pl.BlockSpec((tm, w), lambda i: (i, 0)),
        out_shape=jax.ShapeDtypeStruct((rows, w), ACT_DTYPE),
        compiler_params=_params(32, 1),
        name="chunk_gmlp",
    )(g3, g3, b_norm_flat, ws, bias_b)


def _ret_kernel(qf_ref, kf_ref, vf0_ref, vf1_ref, qb_ref, kb_ref, vb0_ref, vb1_ref,
                intra_ref, qdec_ref, kdec_ref, cdec_ref,
                of_ref, ob_ref, *state_refs):
    @pl.when(pl.program_id(0) == 0)
    def _():
        for s_ref in state_refs:
            s_ref[...] = jnp.zeros(s_ref.shape, s_ref.dtype)

    dirs = ((qf_ref, kf_ref, of_ref), (qb_ref, kb_ref, ob_ref))
    v_halves = ((vf0_ref, vf1_ref), (vb0_ref, vb1_ref))
    chains = [(d, h) for d in range(2) for h in range(C_HEADS)]
    head = lambda h: slice(h * HEAD_DIM, (h + 1) * HEAD_DIM)
    half = C_HEADS // 2
    value = lambda d, h: v_halves[d][h // half][:, (h % half) * HEAD_DIM:(h % half + 1) * HEAD_DIM]

    scores = []
    for d, h in chains:
        q_ref, k_ref, _ = dirs[d]
        q = q_ref[:, head(h)]
        a = lax.dot_general(q, k_ref[:, head(h)], (((1,), (1,)), ((), ())), preferred_element_type=F32)
        qd = (q.astype(F32) * qdec_ref[d, h]).astype(q.dtype)
        scores.append(jnp.concatenate([(a * intra_ref[d, h]).astype(q.dtype), qd], axis=1))
    for (d, h), lhs in zip(chains, scores):
        o_ref = dirs[d][2]
        v = value(d, h)
        rhs = jnp.concatenate([v, state_refs[d * C_HEADS + h][...].astype(v.dtype)], axis=0)
        o_ref[:, head(h)] = jnp.dot(lhs, rhs, preferred_element_type=F32).astype(o_ref.dtype)
    for d, h in chains:
        k_ref = dirs[d][1]
        s_ref = state_refs[d * C_HEADS + h]
        k = k_ref[:, head(h)]
        kd = (k.astype(F32) * kdec_ref[d, h]).astype(k.dtype)
        upd = lax.dot_general(kd, value(d, h), (((0,), (0,)), ((), ())), preferred_element_type=F32)
        s_ref[...] = s_ref[...] * cdec_ref[d, h] + upd


def _ret_call(qr, kr, g12, tables, layer, ctx_rows):
    rows = qr.shape[0]
    w = BRANCH_WIDTH
    nc = rows // CHUNK
    cc = ctx_rows // CHUNK

    def bwd(s):
        return jnp.where(s < cc, cc - 1 - s, nc - 1 + cc - s)

    fwd = lambda s: s
    qk = lambda f: pl.BlockSpec((CHUNK, w), lambda s: (f(s), 0))
    vh = lambda f, c: pl.BlockSpec((CHUNK, w // 2), lambda s: (f(s), c))
    tab = pl.BlockSpec((None, 2, C_HEADS, CHUNK, LANES), lambda s: (layer, 0, 0, 0, 0))
    return pl.pallas_call(
        _ret_kernel,
        grid=(nc,),
        in_specs=[qk(fwd), qk(fwd), vh(fwd, 3), vh(fwd, 4), qk(bwd), qk(bwd), vh(bwd, 3), vh(bwd, 4),
                  tab, tab, tab, tab],
        out_specs=[pl.BlockSpec((CHUNK, w), lambda s: (s, 0)),
                   pl.BlockSpec((CHUNK, w), lambda s: (bwd(s), 0))],
        out_shape=[jax.ShapeDtypeStruct((rows, w), ACT_DTYPE), jax.ShapeDtypeStruct((rows, w), ACT_DTYPE)],
        scratch_shapes=[pltpu.VMEM((HEAD_DIM, HEAD_DIM), F32) for _ in range(2 * C_HEADS)],
        compiler_params=_params(32, 1),
        name="retention",
    )(qr, kr, g12, g12, qr, kr, g12, g12, *tables)


def _merge_kernel(attn_ref, gm_ref, of_ref, ob_ref, rg_ref, gate_ref, cn_ref, wb_ref, o_ref):
    d = o_ref.shape[1]
    ret_parts = []
    for h in range(C_HEADS):
        sl = slice(h * HEAD_DIM, (h + 1) * HEAD_DIM)
        o = of_ref[:, sl].astype(F32) + ob_ref[:, sl].astype(F32)
        oc = o - jnp.mean(o, axis=-1, keepdims=True)
        y = oc * lax.rsqrt(jnp.mean(oc * oc, axis=-1, keepdims=True) + EPS) * cn_ref[:, sl]
        ret_parts.append((_silu(rg_ref[:, sl].astype(F32)) * y).astype(wb_ref.dtype))
    ret = jnp.concatenate(ret_parts, axis=1)
    branches = (attn_ref[...].astype(wb_ref.dtype), gm_ref[...].astype(wb_ref.dtype), ret)
    acc = None
    for b in range(N_BRANCH):
        proj = jnp.dot(branches[b], wb_ref[b], preferred_element_type=F32)
        term = _sigmoid(gate_ref[:, b * d:(b + 1) * d].astype(F32)) * proj
        acc = term if acc is None else acc + term
    o_ref[...] = acc.astype(o_ref.dtype)


def _merge_call(attn, gm, o_f, o_b, g3, g4, c_norm_flat, wb, layer):
    rows = attn.shape[0]
    w = BRANCH_WIDTH
    d = wb.shape[-1]
    tm = 256
    row = lambda c: pl.BlockSpec((tm, w), lambda i: (i, c))
    return pl.pallas_call(
        _merge_kernel,
        grid=(rows // tm,),
        in_specs=[row(0), row(0), row(0), row(0), row(2),
                  pl.BlockSpec((tm, N_BRANCH * d), lambda i: (i, 0)),
                  pl.BlockSpec((None, 1, w), lambda i: (layer, 0, 0)),
                  pl.BlockSpec((None, N_BRANCH, w, d), lambda i: (layer, 0, 0, 0))],
        out_specs=pl.BlockSpec((tm, d), lambda i: (i, 0)),
        out_shape=jax.ShapeDtypeStruct((rows, d), ACT_DTYPE),
        compiler_params=_params(52, 1),
        name="branch_merge",
    )(attn, gm, o_f, o_b, g3, g4, c_norm_flat, wb)


def _route_kernel(m_ref, wo_ref, z_ref, g_ref, mod_ref, whi_ref, wlo_ref, rb_ref,
                  znew_ref, h_ref, info_ref, cnt_ref, carry_ref, logits_ref):
    i = pl.program_id(0)

    @pl.when(i == 0)
    def _():
        carry_ref[...] = jnp.zeros(carry_ref.shape, carry_ref.dtype)
        logits_ref[...] = jnp.zeros(logits_ref.shape, logits_ref.dtype)

    prev_logits = logits_ref[(i + 1) % 2]
    y = jnp.dot(m_ref[...], wo_ref[...], preferred_element_type=F32)
    _route_stage(prev_logits, jnp.where(i >= 1, 1.0, 0.0), info_ref, cnt_ref, carry_ref)
    z = z_ref[...] + mod_ref[2:3, :] * y
    znew_ref[...] = z
    h = _norm_mod(z, g_ref[...], mod_ref[...], 3, 4)
    h_ref[...] = _pack_halves(h)
    h_hi = h.astype(whi_ref.dtype)
    h_lo = (h - h_hi.astype(F32)).astype(whi_ref.dtype)
    logits_ref[i % 2] = (jnp.dot(h_hi, whi_ref[...], preferred_element_type=F32)
                         + jnp.dot(h_hi, wlo_ref[...], preferred_element_type=F32)
                         + jnp.dot(h_lo, whi_ref[...], preferred_element_type=F32)) + rb_ref[...]


def _route_stage(logits, count_gate, info_ref, cnt_ref, carry_ref):
    tm = logits.shape[0]
    lane = lax.broadcasted_iota(jnp.int32, logits.shape, 1).astype(F32)
    first = lambda hit: jnp.min(jnp.where(hit, lane, 4.0 * LANES), axis=-1, keepdims=True)

    is_g = lane < N_GROUPS
    gl = jnp.where(is_g, logits, NEG_INF)
    gmax = jnp.max(gl, axis=-1, keepdims=True)
    g_sel = first(gl == gmax)
    g_w = 1.0 / jnp.sum(jnp.where(is_g, jnp.exp(gl - gmax), 0.0), axis=-1, keepdims=True)

    e_id = lane - N_GROUPS
    in_group = (e_id >= g_sel * EXPERTS_PER_GROUP) & (e_id < (g_sel + 1.0) * EXPERTS_PER_GROUP)
    el = jnp.where(in_group, logits, NEG_INF)
    m1 = jnp.max(el, axis=-1, keepdims=True)
    i1 = first(el == m1)
    el2 = jnp.where(lane == i1, NEG_INF, el)
    m2 = jnp.max(el2, axis=-1, keepdims=True)
    i2 = first(el2 == m2)
    r = jnp.exp(m2 - m1)
    w1 = g_w / (1.0 + r)
    w2 = g_w * r / (1.0 + r)
    e1 = i1 - N_GROUPS
    e2 = i2 - N_GROUPS

    hot1 = lane == e1
    hot2 = lane == e2
    hot = jnp.where(hot1 | hot2, 1.0, 0.0)
    rr = lax.broadcasted_iota(jnp.int32, (tm, tm), 0)
    cc = lax.broadcasted_iota(jnp.int32, (tm, tm), 1)
    tri = jnp.where(cc < rr, 1.0, 0.0).astype(MXU_DTYPE)
    before = jnp.dot(tri, hot.astype(MXU_DTYPE), preferred_element_type=F32) + carry_ref[0:1, :]
    rank1 = jnp.sum(jnp.where(hot1, before, 0.0), axis=-1, keepdims=True)
    rank2 = jnp.sum(jnp.where(hot2, before, 0.0), axis=-1, keepdims=True)
    carry_ref[0:1, :] = carry_ref[0:1, :] + count_gate * jnp.sum(hot, axis=0, keepdims=True)
    cnt_ref[...] = carry_ref[...]

    info = jnp.where(lane == 0, e1, 0.0)
    info = jnp.where(lane == 1, e2, info)
    info = jnp.where(lane == 2, w1, info)
    info = jnp.where(lane == 3, w2, info)
    info = jnp.where(lane == 4, rank1, info)
    info = jnp.where(lane == 5, rank2, info)
    info_ref[...] = info


def _route_call(merged, wo, z, norm_w, modsel, w_hi, w_lo, rbias, layer, ctx_rows):
    rows, d = z.shape
    tm = 256
    ctx_tiles = ctx_rows // tm
    n_tiles = rows // tm
    cur = lambda i: jnp.minimum(i, n_tiles - 1)
    prev = lambda i: jnp.maximum(i - 1, 0)
    return pl.pallas_call(
        _route_kernel,
        grid=(n_tiles + 1,),
        in_specs=[pl.BlockSpec((tm, d), lambda i: (cur(i), 0)),
                  pl.BlockSpec((None, d, d), lambda i: (layer, 0, 0)),
                  pl.BlockSpec((tm, d), lambda i: (cur(i), 0)),
                  pl.BlockSpec((None, 1, d), lambda i: (layer, 0, 0)),
                  pl.BlockSpec((None, None, MOD_ROWS, d),
                               lambda i: (layer, jnp.where(cur(i) >= ctx_tiles, 1, 0), 0, 0)),
                  pl.BlockSpec((None, d, LANES), lambda i: (layer, 0, 0)),
                  pl.BlockSpec((None, d, LANES), lambda i: (layer, 0, 0)),
                  pl.BlockSpec((None, 1, LANES), lambda i: (layer, 0, 0))],
        out_specs=[pl.BlockSpec((tm, d), lambda i: (cur(i), 0)),
                   pl.BlockSpec((tm, d // 2), lambda i: (cur(i), 0)),
                   pl.BlockSpec((tm, LANES), lambda i: (prev(i), 0)),
                   pl.BlockSpec((8, LANES), lambda i: (0, 0))],
        out_shape=[jax.ShapeDtypeStruct((rows, d), F32),
                   jax.ShapeDtypeStruct((rows, d // 2), PACK_DTYPE),
                   jax.ShapeDtypeStruct((rows, LANES), F32),
                   jax.ShapeDtypeStruct((8, LANES), F32)],
        scratch_shapes=[pltpu.VMEM((8, LANES), F32), pltpu.VMEM((2, tm, LANES), F32)],
        compiler_params=_params(48, 1),
        name="outproj_route",
    )(merged, wo, z, norm_w, modsel, w_hi, w_lo, rbias)


def _dispatch_kernel(pad_end_ref, padded_ref, n_used_ref, dest_ref, h_ref, xs_ref, zero_ref, sem, zero_sem):
    tm = h_ref.shape[0]
    n_blocks = xs_ref.shape[0] // MOE_BLOCK

    @pl.when(pl.program_id(0) == 0)
    def _():
        zero_ref[...] = jnp.zeros(zero_ref.shape, zero_ref.dtype)

        def block_copy(first):
            first = pl.multiple_of(first, MOE_BLOCK)
            return pltpu.make_async_copy(zero_ref, xs_ref.at[pl.ds(first, MOE_BLOCK)], zero_sem)

        def start_unused(b, carry):
            block_copy(b * MOE_BLOCK).start()
            return carry

        def wait_unused(b, carry):
            block_copy(b * MOE_BLOCK).wait()
            return carry

        for e in range(N_EXPERTS):
            @pl.when(padded_ref[e] > 0)
            def _(e=e):
                block_copy(pad_end_ref[e] - MOE_BLOCK).start()
        lax.fori_loop(n_used_ref[0], n_blocks, start_unused, 0)
        for e in range(N_EXPERTS):
            @pl.when(padded_ref[e] > 0)
            def _(e=e):
                block_copy(pad_end_ref[e] - MOE_BLOCK).wait()
        lax.fori_loop(n_used_ref[0], n_blocks, wait_unused, 0)

    def row_copy(t, slot):
        return pltpu.make_async_copy(h_ref.at[pl.ds(t, 1)], xs_ref.at[pl.ds(slot, 1)], sem)

    def start(t, carry):
        row_copy(t, dest_ref[0, 0, 2 * t]).start()
        row_copy(t, dest_ref[0, 0, 2 * t + 1]).start()
        return carry

    lax.fori_loop(0, tm, start, 0, unroll=16)
    all_rows = pltpu.make_async_copy(h_ref, xs_ref.at[pl.ds(0, tm)], sem)
    all_rows.wait()
    all_rows.wait()


def _dispatch_call(pad_end, padded, n_used, dest3, h, n_slots):
    rows, d = h.shape
    tm = dest3.shape[2] // 2
    grid_spec = pltpu.PrefetchScalarGridSpec(
        num_scalar_prefetch=3,
        grid=(rows // tm,),
        in_specs=[pl.BlockSpec((1, 1, 2 * tm), lambda i, pe, pd, nu: (i, 0, 0), memory_space=pltpu.SMEM),
                  pl.BlockSpec((tm, d), lambda i, pe, pd, nu: (i, 0))],
        out_specs=pl.BlockSpec(memory_space=pl.ANY),
        scratch_shapes=[pltpu.VMEM((MOE_BLOCK, d), h.dtype),
                        pltpu.SemaphoreType.DMA(()), pltpu.SemaphoreType.DMA(())])
    return pl.pallas_call(
        _dispatch_kernel,
        grid_spec=grid_spec,
        out_shape=jax.ShapeDtypeStruct((n_slots, d), h.dtype),
        compiler_params=_params(32, 1),
        name="moe_dispatch",
    )(pad_end, padded, n_used, dest3, h)


def _expert_kernel(be_ref, nu_ref, first_ref, slot_ref, next_ref, xs_ref, w1_hbm, w2_hbm, ys_ref,
                   w1f_ref, w2f_ref, w1bf_ref, w2bf_ref, sem, *, layer):
    i = pl.program_id(0)
    used = i < nu_ref[0]

    def fetch(e, s):
        return (pltpu.make_async_copy(w1_hbm.at[layer, e], w1f_ref.at[s], sem.at[s]),
                pltpu.make_async_copy(w2_hbm.at[layer, e], w2f_ref.at[s], sem.at[s]))

    @pl.when(i == 0)
    def _():
        for cp in fetch(be_ref[0], slot_ref[0]):
            cp.start()

    @pl.when(jnp.logical_and(used, first_ref[i] == 1))
    def _():
        s = slot_ref[i]
        for cp in fetch(be_ref[i], s):
            cp.wait()

        @pl.when(next_ref[i] >= 0)
        def _():
            for cp in fetch(next_ref[i], 1 - s):
                cp.start()

        w1bf_ref[...] = w1f_ref[s].astype(w1bf_ref.dtype)
        w2bf_ref[...] = w2f_ref[s].astype(w2bf_ref.dtype)

    @pl.when(used)
    def _():
        de = w2bf_ref.shape[0]
        x = jnp.concatenate(_unpack_halves(xs_ref[...]), axis=1).astype(w1bf_ref.dtype)
        hcat = jnp.dot(x, w1bf_ref[...], preferred_element_type=F32)
        act = _silu(hcat[:, :de]) * hcat[:, de:]
        ys_ref[...] = _pack_halves(jnp.dot(act.astype(w2bf_ref.dtype), w2bf_ref[...], preferred_element_type=F32))

    @pl.when(jnp.logical_not(used))
    def _():
        ys_ref[...] = jnp.zeros(ys_ref.shape, ys_ref.dtype)


def _expert_call(plan, xs, w_e1, w_e2, layer):
    slots, dp = xs.shape
    d = 2 * dp
    de = w_e2.shape[2]
    nb = slots // MOE_BLOCK
    blk = lambda i, be, nu, *_: (jnp.minimum(i, nu[0] - 1), 0)
    grid_spec = pltpu.PrefetchScalarGridSpec(
        num_scalar_prefetch=5,
        grid=(nb,),
        in_specs=[pl.BlockSpec((MOE_BLOCK, dp), blk),
                  pl.BlockSpec(memory_space=pl.ANY),
                  pl.BlockSpec(memory_space=pl.ANY)],
        out_specs=pl.BlockSpec((MOE_BLOCK, dp), lambda i, *_: (i, 0)),
        scratch_shapes=[pltpu.VMEM((2, d, 2 * de), w_e1.dtype), pltpu.VMEM((2, de, d), w_e2.dtype),
                        pltpu.VMEM((d, 2 * de), MXU_DTYPE), pltpu.VMEM((de, d), MXU_DTYPE),
                        pltpu.SemaphoreType.DMA((2,))])
    return pl.pallas_call(
        functools.partial(_expert_kernel, layer=layer),
        grid_spec=grid_spec,
        out_shape=jax.ShapeDtypeStruct((slots, dp), PACK_DTYPE),
        compiler_params=_params(48, 1),
        name="moe_experts",
    )(plan["block_e"], plan["n_used"], plan["first"], plan["slot"], plan["next_e"], xs, w_e1, w_e2)


def _combine_kernel(dest_ref, dest_next_ref, ys_ref, info_ref, z_ref, mod_ref, *rest, emit_next):
    if emit_next:
        gn_ref, modn_ref, o_ref, hn_ref, buf_ref, sem = rest
    else:
        o_ref, buf_ref, sem = rest
    i = pl.program_id(0)
    tm = z_ref.shape[0]
    slot = i % 2

    def issue(d_ref, s):
        for t in range(tm):
            for k in range(2):
                pltpu.make_async_copy(ys_ref.at[pl.ds(d_ref[0, 0, 2 * t + k], 1)],
                                      buf_ref.at[s, k, pl.ds(t, 1)], sem.at[s]).start()

    @pl.when(i == 0)
    def _():
        issue(dest_ref, 0)

    @pl.when(i + 1 < pl.num_programs(0))
    def _():
        issue(dest_next_ref, 1 - slot)

    for k in range(2):
        pltpu.make_async_copy(ys_ref.at[pl.ds(0, tm)], buf_ref.at[slot, k], sem.at[slot]).wait()
    info = info_ref[...]
    lane = lax.broadcasted_iota(jnp.int32, info.shape, 1)
    w1 = jnp.sum(jnp.where(lane == 2, info, 0.0), axis=-1, keepdims=True)
    w2 = jnp.sum(jnp.where(lane == 3, info, 0.0), axis=-1, keepdims=True)
    lo1, hi1 = _unpack_halves(buf_ref[slot, 0])
    lo2, hi2 = _unpack_halves(buf_ref[slot, 1])
    y = jnp.concatenate([lo1 * w1 + lo2 * w2, hi1 * w1 + hi2 * w2], axis=1)
    z = z_ref[...] + mod_ref[5:6, :] * y
    o_ref[...] = z
    if emit_next:
        hn_ref[...] = _norm_mod(z, gn_ref[...], modn_ref[...], 0, 1).astype(hn_ref.dtype)


def _combine_call(dest3, ys, info, z, modsel, layer, ctx_rows, next_norm_w=None):
    rows, d = z.shape
    tm = dest3.shape[2] // 2
    ctx_tiles = ctx_rows // tm
    n_tiles = rows // tm
    emit_next = next_norm_w is not None
    mod_spec = lambda l: pl.BlockSpec((None, None, MOD_ROWS, d),
                                      lambda i: (l, jnp.where(i >= ctx_tiles, 1, 0), 0, 0))
    row_spec = pl.BlockSpec((tm, d), lambda i: (i, 0))
    in_specs = [pl.BlockSpec((1, 1, 2 * tm), lambda i: (i, 0, 0), memory_space=pltpu.SMEM),
                pl.BlockSpec((1, 1, 2 * tm), lambda i: (jnp.minimum(i + 1, n_tiles - 1), 0, 0),
                             memory_space=pltpu.SMEM),
                pl.BlockSpec(memory_space=pl.ANY),
                pl.BlockSpec((tm, LANES), lambda i: (i, 0)),
                row_spec,
                mod_spec(layer)]
    args = [dest3, dest3, ys, info, z, modsel]
    if emit_next:
        out_specs = [row_spec]
        out_shape = [jax.ShapeDtypeStruct((rows, d), F32)]
    else:
        out_specs = [pl.BlockSpec((tm, d), lambda i: (jnp.maximum(i - ctx_tiles, 0), 0))]
        out_shape = [jax.ShapeDtypeStruct((rows - ctx_rows, d), F32)]
    if emit_next:
        in_specs += [pl.BlockSpec((None, 1, d), lambda i: (layer + 1, 0, 0)), mod_spec(layer + 1)]
        args += [next_norm_w, modsel]
        out_specs.append(row_spec)
        out_shape.append(jax.ShapeDtypeStruct((rows, d), ACT_DTYPE))
    return pl.pallas_call(
        functools.partial(_combine_kernel, emit_next=emit_next),
        grid=(n_tiles,),
        in_specs=in_specs,
        out_specs=out_specs,
        out_shape=out_shape,
        scratch_shapes=[pltpu.VMEM((2, 2, tm, ys.shape[1]), ys.dtype), pltpu.SemaphoreType.DMA((2,))],
        compiler_params=_params(40, 1),
        name="moe_combine",
    )(*args)


def _rope_tables(n, ctx_rows):
    rows = n // GRID_W
    row = jnp.repeat(jnp.arange(rows, dtype=F32), GRID_W)
    col = jnp.tile(jnp.arange(GRID_W, dtype=F32), rows)
    nq = HEAD_DIM // 4
    inv = ROPE_BASE ** (-jnp.arange(nq, dtype=F32) / nq)
    ar, ac = row[:, None] * inv, col[:, None] * inv
    cos = jnp.concatenate([jnp.cos(ar), jnp.cos(ar), jnp.cos(ac), jnp.cos(ac)], axis=1)
    sin = jnp.concatenate([-jnp.sin(ar), jnp.sin(ar), -jnp.sin(ac), jnp.sin(ac)], axis=1)
    cos = jnp.concatenate([jnp.ones((ctx_rows, HEAD_DIM), F32), cos], axis=0)
    sin = jnp.concatenate([jnp.zeros((ctx_rows, HEAD_DIM), F32), sin], axis=0)
    return cos, sin


def _retention_tables(c_decay_fwd, c_decay_bwd):
    lg_f = jax.nn.log_sigmoid(c_decay_fwd.astype(F32))[:, :, None, None]
    lg_b = jax.nn.log_sigmoid(c_decay_bwd.astype(F32))[:, :, None, None]
    idx = jnp.arange(CHUNK, dtype=F32)
    diff = idx[:, None] - idx[None, :]
    ones = jnp.ones((CHUNK, CHUNK), F32)
    t_col = idx[:, None] * ones
    intra_f = jnp.where(diff >= 0, jnp.exp(lg_f * jnp.maximum(diff, 0.0)), 0.0)
    intra_b = jnp.where(diff <= 0, jnp.exp(lg_b * jnp.maximum(-diff, 0.0)), 0.0)
    qdec_f = jnp.exp(lg_f * (t_col + 1.0))
    qdec_b = jnp.exp(lg_b * (CHUNK - t_col))
    kdec_f = jnp.exp(lg_f * (CHUNK - 1.0 - t_col))
    kdec_b = jnp.exp(lg_b * t_col)
    cdec_f = jnp.exp(lg_f * CHUNK) * ones
    cdec_b = jnp.exp(lg_b * CHUNK) * ones
    pair = lambda a, b: jnp.stack([a, b], axis=1)
    return (pair(intra_f, intra_b), pair(qdec_f, qdec_b), pair(kdec_f, kdec_b), pair(cdec_f, cdec_b))


def _moe_plan(info, counts_row, n_slots_blocks):
    e = info[:, 0:2].astype(jnp.int32)
    rank = info[:, 4:6].astype(jnp.int32)
    counts = counts_row[:N_EXPERTS].astype(jnp.int32)
    padded = (counts + MOE_BLOCK - 1) // MOE_BLOCK * MOE_BLOCK
    pad_end = jnp.cumsum(padded)
    pad_start = pad_end - padded
    hit = e[:, :, None] == jnp.arange(N_EXPERTS, dtype=jnp.int32)
    dest = jnp.sum(jnp.where(hit, pad_start, 0), axis=-1) + rank
    n_used = pad_end[-1] // MOE_BLOCK
    blocks = jnp.arange(n_slots_blocks, dtype=jnp.int32)
    first_slot = jnp.minimum(blocks, n_used - 1) * MOE_BLOCK
    block_e = jnp.sum((pad_end[None, :] <= first_slot[:, None]).astype(jnp.int32), axis=1)
    ids = jnp.arange(N_EXPERTS, dtype=jnp.int32)
    nonempty = counts > 0
    slot_of = (jnp.cumsum(nonempty.astype(jnp.int32)) - 1) % 2
    later = nonempty[None, :] & (ids[None, :] > ids[:, None])
    next_of = jnp.min(jnp.where(later, ids[None, :], N_EXPERTS), axis=1)
    next_of = jnp.where(next_of == N_EXPERTS, -1, next_of)
    pick = lambda table: jnp.sum(jnp.where(block_e[:, None] == ids[None, :], table[None, :], 0), axis=1)
    prev_e = jnp.concatenate([jnp.full((1,), -1, jnp.int32), block_e[:-1]])
    first = ((block_e != prev_e) & (blocks < n_used)).astype(jnp.int32)
    i32 = lambda a: a.astype(jnp.int32)
    return dict(dest=dest, block_e=i32(block_e), n_used=i32(n_used.reshape(1)), pad_end=i32(pad_end),
                padded=i32(padded), first=first, slot=i32(pick(slot_of)), next_e=i32(pick(next_of)))


def kernel(x, c, ctx, c_ctx, norm_mix, norm_ffn, w_ada, b_ada, w_in, a_q_norm, a_k_norm, a_sink,
           b_norm, b_spatial, b_spatial_bias, c_decay_fwd, c_decay_bwd, c_norm, w_branch, w_out,
           w_router_group, b_router_group, w_router_expert, b_router_expert, w_expert_in, w_expert_out):
    batch, n, d = x.shape
    ctx_rows = ctx.shape[1]
    depth = w_in.shape[0]
    assert batch == 1 and ctx_rows % 256 == 0 and n % 256 == 0
    rows = ctx_rows + n

    cond = jnp.stack([c[0], c_ctx], axis=0)
    cond_b = jnp.broadcast_to(cond[:, :, None], (2, d, LANES))
    mod = _ada_call(cond_b, w_ada, b_ada)[:, :2].reshape(depth, 2, N_MOD, d)
    modsel = jnp.pad(mod[:, ::-1], ((0, 0), (0, 0), (0, MOD_ROWS - N_MOD), (0, 0)))

    cos_t, sin_t = _rope_tables(n, ctx_rows)
    ret_tables = _retention_tables(c_decay_fwd, c_decay_bwd)
    sink_col = jnp.broadcast_to(a_sink.astype(F32).reshape(depth, A_KV_HEADS, A_GROUP, 1, 1),
                                (depth, A_KV_HEADS, A_GROUP, A_BLOCK, 1)).reshape(depth, A_KV_HEADS, A_GROUP * A_BLOCK, 1)
    ws = b_spatial.astype(MXU_DTYPE)
    bias_b = jnp.broadcast_to(b_spatial_bias.astype(F32)[:, :, :, None], (depth, B_GROUPS, CHUNK, LANES))
    wb = w_branch.astype(MXU_DTYPE)
    wo = w_out.astype(MXU_DTYPE)
    w_r = jnp.concatenate([w_router_group, w_router_expert], axis=-1).astype(F32)
    w_r = jnp.pad(w_r, ((0, 0), (0, 0), (0, LANES - w_r.shape[-1])))
    w_r_hi = w_r.astype(MXU_DTYPE)
    w_r_lo = (w_r - w_r_hi.astype(F32)).astype(MXU_DTYPE)
    b_r = jnp.concatenate([b_router_group, b_router_expert], axis=-1).astype(F32)
    b_r = jnp.pad(b_r, ((0, 0), (0, LANES - b_r.shape[-1]))).reshape(depth, 1, LANES)
    norm_mix3 = norm_mix.reshape(depth, 1, d)
    norm_ffn3 = norm_ffn.reshape(depth, 1, d)
    lane_ids = jnp.arange(HEAD_DIM)
    partner = jnp.where((lane_ids // 32) % 2 == 0, lane_ids + 32, lane_ids - 32)
    perm = (lane_ids[:, None] == partner[None, :]).astype(ACT_DTYPE)
    a_q_norm3 = jnp.stack([a_q_norm, a_q_norm[:, partner]], axis=1).astype(F32)
    a_k_norm3 = jnp.stack([a_k_norm, a_k_norm[:, partner]], axis=1).astype(F32)
    b_norm3 = b_norm.reshape(depth, 1, BRANCH_WIDTH)
    c_norm3 = c_norm.reshape(depth, 1, BRANCH_WIDTH)

    kvw = A_KV_HEADS * HEAD_DIM
    w = BRANCH_WIDTH
    n_assign = rows * 2
    n_slot_blocks = -(-(n_assign + N_EXPERTS * (MOE_BLOCK - 1)) // MOE_BLOCK)
    tok_tile = 256
    disp_tile = _row_tile(rows, (1056, 768, 256))

    z, h = _first_norm_call(ctx[0], x[0], norm_mix3, modsel)
    for l in range(depth):
        g12 = _proj_call(h, w_in, l, 0, 2 * kvw + 2 * w)
        g3 = _proj_call(h, w_in, l, 2 * kvw + 2 * w, 5 * w)
        g4 = _proj_call(h, w_in, l, 2 * kvw + 7 * w, N_BRANCH * d)
        kn, qn, kr, qr = _prep_call(g12, g3, cos_t, sin_t, a_q_norm3, a_k_norm3, perm, l)
        attn = _attn_call(qn, kn, g12, sink_col, l, ctx_rows)
        gm = _gmlp_call(g3, b_norm3, ws, bias_b, l)
        o_f, o_b = _ret_call(qr, kr, g12, ret_tables, l, ctx_rows)
        merged = _merge_call(attn, gm, o_f, o_b, g3, g4, c_norm3, wb, l)
        z, h2, info, counts = _route_call(merged, wo, z, norm_ffn3, modsel, w_r_hi, w_r_lo, b_r, l, ctx_rows)
        plan = _moe_plan(info, counts[0], n_slot_blocks)
        dest3 = plan["dest"].reshape(rows // tok_tile, 1, 2 * tok_tile)
        dest3_disp = plan["dest"].reshape(rows // disp_tile, 1, 2 * disp_tile)
        xs = _dispatch_call(plan["pad_end"], plan["padded"], plan["n_used"], dest3_disp, h2,
                            n_slot_blocks * MOE_BLOCK)
        ys = _expert_call(plan, xs, w_expert_in, w_expert_out, l)
        if l + 1 < depth:
            z, h = _combine_call(dest3, ys, info, z, modsel, l, ctx_rows, next_norm_w=norm_mix3)
        else:
            (z_latent,) = _combine_call(dest3, ys, info, z, modsel, l, ctx_rows)
    return z_latent[None]
```

```python
import functools
import math

import jax
import jax.numpy as jnp
from jax import lax
from jax.experimental import pallas as pl
from jax.experimental.pallas import tpu as pltpu

F32 = jnp.float32
MXU_DTYPE = jnp.bfloat16
ACT_DTYPE = jnp.bfloat16

LANES = 128
HEAD_DIM = 128
GRID_W = 64
ROPE_BASE = 10000.0
EPS = 1e-6
NEG_INF = -1e30
A_Q_HEADS = 8
A_KV_HEADS = 2
A_GROUP = A_Q_HEADS // A_KV_HEADS
A_BLOCK = 128
B_GROUPS = 8
C_HEADS = 8
CHUNK = 128
N_GROUPS = 4
EXPERTS_PER_GROUP = 8
N_EXPERTS = N_GROUPS * EXPERTS_PER_GROUP
D_EXPERT = 512
BRANCH_WIDTH = 1024
N_BRANCH = 3
MOE_BLOCK = 320
N_MOD = 6
MOD_ROWS = 8
MIB = 1024 * 1024


def _params(vmem_mib, n_grid, **kw):
    return pltpu.CompilerParams(dimension_semantics=("arbitrary",) * n_grid,
                                vmem_limit_bytes=vmem_mib * MIB, **kw)


def _sigmoid(x):
    return 0.5 + 0.5 * jnp.tanh(0.5 * x)


def _silu(x):
    return x * _sigmoid(x)


PACK_DTYPE = jnp.uint32


def _pack_halves(x):
    n = x.shape[1] // 2
    rounded = lambda v: lax.bitcast_convert_type(v.astype(jnp.bfloat16).astype(F32), PACK_DTYPE)
    return rounded(x[:, n:]) | (rounded(x[:, :n]) >> 16)


def _unpack_halves(p):
    lo = lax.bitcast_convert_type(p << 16, F32)
    hi = lax.bitcast_convert_type(p & jnp.asarray(0xFFFF0000, PACK_DTYPE), F32)
    return lo, hi


def _gelu_tanh(x):
    return 0.5 * x * (1.0 + jnp.tanh(math.sqrt(2.0 / math.pi) * (x + 0.044715 * (x * x * x))))


def _ada_kernel(c_ref, w_ref, b_ref, o_ref):
    tn = w_ref.shape[1]
    s0 = _silu(c_ref[0])
    s1 = _silu(c_ref[1])
    o_ref[...] = jnp.zeros(o_ref.shape, o_ref.dtype)
    for j in range(tn // LANES):
        sl = slice(j * LANES, (j + 1) * LANES)
        wj = w_ref[:, sl]
        o_ref[0:1, sl] = jnp.sum(wj * s0, axis=0, keepdims=True) + b_ref[:, sl]
        o_ref[1:2, sl] = jnp.sum(wj * s1, axis=0, keepdims=True) + b_ref[:, sl]


def _ada_call(cond_b, w_ada, b_ada):
    depth, k, n = w_ada.shape
    tn = 2048
    return pl.pallas_call(
        _ada_kernel,
        grid=(depth, n // tn),
        in_specs=[pl.BlockSpec((2, k, LANES), lambda l, j: (0, 0, 0)),
                  pl.BlockSpec((None, k, tn), lambda l, j: (l, 0, j)),
                  pl.BlockSpec((None, 1, tn), lambda l, j: (l, 0, j))],
        out_specs=pl.BlockSpec((None, 8, tn), lambda l, j: (l, 0, j)),
        out_shape=jax.ShapeDtypeStruct((depth, 8, n), F32),
        compiler_params=_params(48, 2),
        name="adaln",
    )(cond_b, w_ada, b_ada.reshape(depth, 1, n))


def _norm_mod(z, g, mod, shift_row, scale_row):
    r = lax.rsqrt(jnp.mean(z * z, axis=-1, keepdims=True) + EPS)
    return (z * r * g) * (1.0 + mod[scale_row:scale_row + 1, :]) + mod[shift_row:shift_row + 1, :]


def _first_norm_kernel(ctx_ref, x_ref, g_ref, mod_ref, z_ref, h_ref, *, ctx_tiles):
    def emit(src_ref):
        z = src_ref[...]
        z_ref[...] = z
        h_ref[...] = _norm_mod(z, g_ref[...], mod_ref[...], 0, 1).astype(h_ref.dtype)

    @pl.when(pl.program_id(0) < ctx_tiles)
    def _():
        emit(ctx_ref)

    @pl.when(pl.program_id(0) >= ctx_tiles)
    def _():
        emit(x_ref)


def _first_norm_call(ctx2, x2, norm_w, modsel):
    ctx_rows, d = ctx2.shape
    rows = ctx_rows + x2.shape[0]
    tm = 256
    ctx_tiles = ctx_rows // tm
    row_spec = pl.BlockSpec((tm, d), lambda i: (i, 0))
    return pl.pallas_call(
        functools.partial(_first_norm_kernel, ctx_tiles=ctx_tiles),
        grid=(rows // tm,),
        in_specs=[pl.BlockSpec((tm, d), lambda i: (jnp.minimum(i, ctx_tiles - 1), 0)),
                  pl.BlockSpec((tm, d), lambda i: (jnp.maximum(i - ctx_tiles, 0), 0)),
                  pl.BlockSpec((None, 1, d), lambda i: (0, 0, 0)),
                  pl.BlockSpec((None, None, MOD_ROWS, d),
                               lambda i: (0, jnp.where(i >= ctx_tiles, 1, 0), 0, 0))],
        out_specs=[row_spec, row_spec],
        out_shape=[jax.ShapeDtypeStruct((rows, d), F32), jax.ShapeDtypeStruct((rows, d), ACT_DTYPE)],
        compiler_params=_params(32, 1),
        name="stack_norm_mod",
    )(ctx2, x2, norm_w, modsel)


def _proj_kernel(h_ref, w_ref, o_ref, wbf_ref):
    @pl.when(pl.program_id(1) == 0)
    def _():
        wbf_ref[...] = w_ref[...].astype(wbf_ref.dtype)

    o_ref[...] = jnp.dot(h_ref[...], wbf_ref[...], preferred_element_type=F32).astype(o_ref.dtype)


def _row_tile(rows, pref):
    for t in pref:
        if rows % t == 0:
            return t
    raise ValueError(f"no row tile for {rows}")


def _proj_call(h, w, layer, col_off, ncols):
    rows, k = h.shape
    tn = next(t for t in (1536, 1280, 1024, 512) if col_off % t == 0 and ncols % t == 0)
    vmem_mib = 52
    need = lambda t: (2 * 4 + 2) * k * tn + 2 * 2 * t * k + (2 * 2 + 4) * t * tn
    tm = next(t for t in (1056, 768, 512, 256) if rows % t == 0 and need(t) <= (vmem_mib - 4) * MIB)
    off = col_off // tn
    return pl.pallas_call(
        _proj_kernel,
        grid=(ncols // tn, rows // tm),
        in_specs=[pl.BlockSpec((tm, k), lambda j, i: (i, 0)),
                  pl.BlockSpec((None, k, tn), lambda j, i: (layer, 0, off + j))],
        out_specs=pl.BlockSpec((tm, tn), lambda j, i: (i, j)),
        out_shape=jax.ShapeDtypeStruct((rows, ncols), ACT_DTYPE),
        scratch_shapes=[pltpu.VMEM((k, tn), MXU_DTYPE)],
        compiler_params=_params(vmem_mib, 2),
        name="proj_in",
    )(h, w)


def _prep_kernel(ak_ref, ck0_ref, ck1_ref, aq_ref, cq_ref, cos_ref, sin_ref, qn_ref, kn_ref, perm_ref,
                 okn_ref, oqn_ref, okr_ref, oqr_ref):
    cos = cos_ref[...]
    sin = sin_ref[...]
    perm = perm_ref[...]
    scale = HEAD_DIM ** -0.5

    def swap(x):
        return jnp.dot(x, perm, preferred_element_type=F32)

    def norm_rope(x, g_ref, out_scale):
        xf = x.astype(F32)
        r = lax.rsqrt(jnp.mean(xf * xf, axis=-1, keepdims=True) + EPS) * out_scale
        return (xf * (g_ref[0:1, :] * cos) + swap(x) * (g_ref[1:2, :] * sin)) * r

    for h in range(A_KV_HEADS):
        sl = slice(h * HEAD_DIM, (h + 1) * HEAD_DIM)
        okn_ref[:, sl] = norm_rope(ak_ref[:, sl], kn_ref, 1.0).astype(okn_ref.dtype)
    for h in range(A_Q_HEADS):
        sl = slice(h * HEAD_DIM, (h + 1) * HEAD_DIM)
        oqn_ref[:, sl] = norm_rope(aq_ref[:, sl], qn_ref, scale).astype(oqn_ref.dtype)
    for h in range(C_HEADS):
        sl = slice(h * HEAD_DIM, (h + 1) * HEAD_DIM)
        half = C_HEADS // 2
        k = (ck0_ref if h < half else ck1_ref)[:, (h % half) * HEAD_DIM:(h % half + 1) * HEAD_DIM]
        q = cq_ref[:, sl]
        okr_ref[:, sl] = ((k.astype(F32) * cos + swap(k) * sin) * scale).astype(okr_ref.dtype)
        oqr_ref[:, sl] = (q.astype(F32) * cos + swap(q) * sin).astype(oqr_ref.dtype)


def _prep_call(g12, g3, cos_t, sin_t, a_q_norm, a_k_norm, perm, layer):
    rows = g12.shape[0]
    tm = _row_tile(rows, (768, 512, 256))
    w = BRANCH_WIDTH
    kvw = A_KV_HEADS * HEAD_DIM
    return pl.pallas_call(
        _prep_kernel,
        grid=(rows // tm,),
        in_specs=[pl.BlockSpec((tm, kvw), lambda i: (i, 0)),
                  pl.BlockSpec((tm, w // 2), lambda i: (i, 1)),
                  pl.BlockSpec((tm, w // 2), lambda i: (i, 2)),
                  pl.BlockSpec((tm, w), lambda i: (i, 0)),
                  pl.BlockSpec((tm, w), lambda i: (i, 1)),
                  pl.BlockSpec((tm, LANES), lambda i: (i, 0)),
                  pl.BlockSpec((tm, LANES), lambda i: (i, 0)),
                  pl.BlockSpec((None, 2, HEAD_DIM), lambda i: (layer, 0, 0)),
                  pl.BlockSpec((None, 2, HEAD_DIM), lambda i: (layer, 0, 0)),
                  pl.BlockSpec((HEAD_DIM, HEAD_DIM), lambda i: (0, 0))],
        out_specs=[pl.BlockSpec((tm, kvw), lambda i: (i, 0)),
                   pl.BlockSpec((tm, w), lambda i: (i, 0)),
                   pl.BlockSpec((tm, w), lambda i: (i, 0)),
                   pl.BlockSpec((tm, w), lambda i: (i, 0))],
        out_shape=[jax.ShapeDtypeStruct((rows, kvw), ACT_DTYPE),
                   jax.ShapeDtypeStruct((rows, w), ACT_DTYPE),
                   jax.ShapeDtypeStruct((rows, w), ACT_DTYPE),
                   jax.ShapeDtypeStruct((rows, w), ACT_DTYPE)],
        compiler_params=_params(40, 1),
        name="qk_prep",
    )(g12, g12, g12, g3, g3, cos_t, sin_t, a_q_norm, a_k_norm, perm)


def _attn_kernel(q_ref, kp_ref, km_ref, kn_ref, kc_ref, vp_ref, vm_ref, vn_ref, vc_ref, sink_ref,
                 o_ref, *, ctx_blocks, n_blocks):
    blk = A_BLOCK
    n_keys = 3 * blk + kc_ref.shape[0]
    row = lax.broadcasted_iota(jnp.int32, (A_GROUP * blk, n_keys), 0) % blk
    col = lax.broadcasted_iota(jnp.int32, (A_GROUP * blk, n_keys), 1)
    in_window = (row >= jnp.maximum(col - 2 * blk, 0)) & (row <= jnp.where(col < blk, col, blk - 1))
    is_ctx_key = col >= 3 * blk

    def valid_mask(rb):
        is_lat = rb >= ctx_blocks
        c_lo = jnp.where(rb >= ctx_blocks + 1, 0, blk)
        c_hi = jnp.where(rb <= n_blocks - 2, 3 * blk, 2 * blk)
        c_lo = jnp.where(is_lat, c_lo, 3 * blk)
        c_hi = jnp.where(is_lat, c_hi, 0)
        return ((col >= c_lo) & (col < c_hi) & in_window) | is_ctx_key

    hs = lambda hk: slice(hk * HEAD_DIM, (hk + 1) * HEAD_DIM)
    qs = lambda hk, g: slice((hk * A_GROUP + g) * HEAD_DIM, (hk * A_GROUP + g + 1) * HEAD_DIM)
    lo_rows, hi_rows = slice(0, blk), slice(blk, 2 * blk)

    def band(j, prev_ref, mid_ref, next_ref, ctx_ref, hk):
        parts = ((prev_ref[:, hs(hk)], mid_ref[lo_rows, hs(hk)], mid_ref[hi_rows, hs(hk)]) if j == 0 else
                 (mid_ref[lo_rows, hs(hk)], mid_ref[hi_rows, hs(hk)], next_ref[:, hs(hk)]))
        return jnp.concatenate(parts + (ctx_ref[:, hs(hk)],), axis=0)

    chains = [(j, hk) for j in range(2) for hk in range(A_KV_HEADS)]
    masks = [valid_mask(2 * pl.program_id(0) + j) for j in range(2)]
    scores = []
    for j, hk in chains:
        q = jnp.concatenate([q_ref[j * blk:(j + 1) * blk, qs(hk, g)] for g in range(A_GROUP)], axis=0)
        k = band(j, kp_ref, km_ref, kn_ref, kc_ref, hk)
        scores.append(lax.dot_general(q, k, (((1,), (1,)), ((), ())), preferred_element_type=F32))
    probs, denoms = [], []
    for (j, hk), s in zip(chains, scores):
        s = jnp.where(masks[j], s, NEG_INF)
        sink = sink_ref[hk]
        m = jnp.maximum(jnp.max(s, axis=-1, keepdims=True), sink)
        p = jnp.exp(s - m)
        denoms.append(jnp.sum(p, axis=-1, keepdims=True) + jnp.exp(sink - m))
        probs.append(p.astype(vm_ref.dtype))
    for (j, hk), p, den in zip(chains, probs, denoms):
        v = band(j, vp_ref, vm_ref, vn_ref, vc_ref, hk)
        o = jnp.dot(p, v, preferred_element_type=F32) / den
        for g in range(A_GROUP):
            o_ref[j * blk:(j + 1) * blk, qs(hk, g)] = o[g * blk:(g + 1) * blk].astype(o_ref.dtype)


def _attn_call(qn, kn, g12, sink_col, layer, ctx_rows):
    rows = qn.shape[0]
    blk = A_BLOCK
    nb = rows // blk
    cb = ctx_rows // blk
    assert nb % 2 == 0 and cb % 2 == 0
    qw = A_Q_HEADS * HEAD_DIM
    kvw = A_KV_HEADS * HEAD_DIM
    before = lambda i: jnp.maximum(2 * i - 1, 0)
    after = lambda i: jnp.minimum(2 * i + 2, nb - 1)
    edge = lambda f, c: pl.BlockSpec((blk, kvw), lambda i: (f(i), c))
    pair = lambda c: pl.BlockSpec((2 * blk, kvw), lambda i: (i, c))
    return pl.pallas_call(
        functools.partial(_attn_kernel, ctx_blocks=cb, n_blocks=nb),
        grid=(nb // 2,),
        in_specs=[pl.BlockSpec((2 * blk, qw), lambda i: (i, 0)),
                  edge(before, 0), pair(0), edge(after, 0),
                  pl.BlockSpec((ctx_rows, kvw), lambda i: (0, 0)),
                  edge(before, 1), pair(1), edge(after, 1),
                  pl.BlockSpec((ctx_rows, kvw), lambda i: (0, 1)),
                  pl.BlockSpec((None, A_KV_HEADS, A_GROUP * blk, 1), lambda i: (layer, 0, 0, 0))],
        out_specs=pl.BlockSpec((2 * blk, qw), lambda i: (i, 0)),
        out_shape=jax.ShapeDtypeStruct((rows, qw), ACT_DTYPE),
        compiler_params=_params(40, 1),
        name="window_attn",
    )(qn, kn, kn, kn, kn, g12, g12, g12, g12, sink_col)


GMLP_CHUNKS_PER_STEP = 2


def _gmlp_kernel(u_ref, v_ref, bn_ref, ws_ref, bias_ref, o_ref):
    chunks = range(u_ref.shape[0] // CHUNK)
    rows = lambda c: slice(c * CHUNK, (c + 1) * CHUNK)
    for g in range(B_GROUPS):
        sl = slice(g * LANES, (g + 1) * LANES)
        normed = []
        for c in chunks:
            v = _gelu_tanh(v_ref[rows(c), sl].astype(F32))
            vc = v - jnp.mean(v, axis=-1, keepdims=True)
            vh = vc * lax.rsqrt(jnp.mean(vc * vc, axis=-1, keepdims=True) + EPS) * bn_ref[:, sl]
            normed.append(vh.astype(ws_ref.dtype))
        mixed = jnp.dot(ws_ref[g], jnp.concatenate(normed, axis=1), preferred_element_type=F32)
        for c in chunks:
            m = mixed[:, c * LANES:(c + 1) * LANES] + bias_ref[g]
            o_ref[rows(c), sl] = (_gelu_tanh(u_ref[rows(c), sl].astype(F32)) * m).astype(o_ref.dtype)


def _gmlp_call(g3, b_norm_flat, ws, bias_b, layer):
    rows = g3.shape[0]
    w = BRANCH_WIDTH
    tm = GMLP_CHUNKS_PER_STEP * CHUNK
    return pl.pallas_call(
        _gmlp_kernel,
        grid=(rows // tm,),
        in_specs=[pl.BlockSpec((tm, w), lambda i: (i, 3)),
                  pl.BlockSpec((tm, w), lambda i: (i, 4)),
                  pl.BlockSpec((None, 1, w), lambda i: (layer, 0, 0)),
                  pl.BlockSpec((None, B_GROUPS, CHUNK, CHUNK), lambda i: (layer, 0, 0, 0)),
                  pl.BlockSpec((None, B_GROUPS, CHUNK, LANES), lambda i: (layer, 0, 0, 0))],
        out_specs=pl.BlockSpec((tm, w), lambda i: (i, 0)),
        out_shape=jax.ShapeDtypeStruct((rows, w), ACT_DTYPE),
        compiler_params=_params(32, 1),
        name="chunk_gmlp",
    )(g3, g3, b_norm_flat, ws, bias_b)


def _ret_kernel(qf_ref, kf_ref, vf0_ref, vf1_ref, qb_ref, kb_ref, vb0_ref, vb1_ref,
                intra_ref, qdec_ref, kdec_ref, cdec_ref,
                of_ref, ob_ref, *state_refs):
    @pl.when(pl.program_id(0) == 0)
    def _():
        for s_ref in state_refs:
            s_ref[...] = jnp.zeros(s_ref.shape, s_ref.dtype)

    dirs = ((qf_ref, kf_ref, of_ref), (qb_ref, kb_ref, ob_ref))
    v_halves = ((vf0_ref, vf1_ref), (vb0_ref, vb1_ref))
    chains = [(d, h) for d in range(2) for h in range(C_HEADS)]
    head = lambda h: slice(h * HEAD_DIM, (h + 1) * HEAD_DIM)
    half = C_HEADS // 2
    value = lambda d, h: v_halves[d][h // half][:, (h % half) * HEAD_DIM:(h % half + 1) * HEAD_DIM]

    scores = []
    for d, h in chains:
        q_ref, k_ref, _ = dirs[d]
        q = q_ref[:, head(h)]
        a = lax.dot_general(q, k_ref[:, head(h)], (((1,), (1,)), ((), ())), preferred_element_type=F32)
        qd = (q.astype(F32) * qdec_ref[d, h]).astype(q.dtype)
        scores.append(jnp.concatenate([(a * intra_ref[d, h]).astype(q.dtype), qd], axis=1))
    for (d, h), lhs in zip(chains, scores):
        o_ref = dirs[d][2]
        v = value(d, h)
        rhs = jnp.concatenate([v, state_refs[d * C_HEADS + h][...].astype(v.dtype)], axis=0)
        o_ref[:, head(h)] = jnp.dot(lhs, rhs, preferred_element_type=F32).astype(o_ref.dtype)
    for d, h in chains:
        k_ref = dirs[d][1]
        s_ref = state_refs[d * C_HEADS + h]
        k = k_ref[:, head(h)]
        kd = (k.astype(F32) * kdec_ref[d, h]).astype(k.dtype)
        upd = lax.dot_general(kd, value(d, h), (((0,), (0,)), ((), ())), preferred_element_type=F32)
        s_ref[...] = s_ref[...] * cdec_ref[d, h] + upd


def _ret_call(qr, kr, g12, tables, layer, ctx_rows):
    rows = qr.shape[0]
    w = BRANCH_WIDTH
    nc = rows // CHUNK
    cc = ctx_rows // CHUNK

    def bwd(s):
        return jnp.where(s < cc, cc - 1 - s, nc - 1 + cc - s)

    fwd = lambda s: s
    qk = lambda f: pl.BlockSpec((CHUNK, w), lambda s: (f(s), 0))
    vh = lambda f, c: pl.BlockSpec((CHUNK, w // 2), lambda s: (f(s), c))
    tab = pl.BlockSpec((None, 2, C_HEADS, CHUNK, LANES), lambda s: (layer, 0, 0, 0, 0))
    return pl.pallas_call(
        _ret_kernel,
        grid=(nc,),
        in_specs=[qk(fwd), qk(fwd), vh(fwd, 3), vh(fwd, 4), qk(bwd), qk(bwd), vh(bwd, 3), vh(bwd, 4),
                  tab, tab, tab, tab],
        out_specs=[pl.BlockSpec((CHUNK, w), lambda s: (s, 0)),
                   pl.BlockSpec((CHUNK, w), lambda s: (bwd(s), 0))],
        out_shape=[jax.ShapeDtypeStruct((rows, w), ACT_DTYPE), jax.ShapeDtypeStruct((rows, w), ACT_DTYPE)],
        scratch_shapes=[pltpu.VMEM((HEAD_DIM, HEAD_DIM), F32) for _ in range(2 * C_HEADS)],
        compiler_params=_params(32, 1),
        name="retention",
    )(qr, kr, g12, g12, qr, kr, g12, g12, *tables)


def _merge_kernel(attn_ref, gm_ref, of_ref, ob_ref, rg_ref, gate_ref, cn_ref, wb_ref, o_ref):
    d = o_ref.shape[1]
    ret_parts = []
    for h in range(C_HEADS):
        sl = slice(h * HEAD_DIM, (h + 1) * HEAD_DIM)
        o = of_ref[:, sl].astype(F32) + ob_ref[:, sl].astype(F32)
        oc = o - jnp.mean(o, axis=-1, keepdims=True)
        y = oc * lax.rsqrt(jnp.mean(oc * oc, axis=-1, keepdims=True) + EPS) * cn_ref[:, sl]
        ret_parts.append((_silu(rg_ref[:, sl].astype(F32)) * y).astype(wb_ref.dtype))
    ret = jnp.concatenate(ret_parts, axis=1)
    branches = (attn_ref[...].astype(wb_ref.dtype), gm_ref[...].astype(wb_ref.dtype), ret)
    acc = None
    for b in range(N_BRANCH):
        proj = jnp.dot(branches[b], wb_ref[b], preferred_element_type=F32)
        term = _sigmoid(gate_ref[:, b * d:(b + 1) * d].astype(F32)) * proj
        acc = term if acc is None else acc + term
    o_ref[...] = acc.astype(o_ref.dtype)


def _merge_call(attn, gm, o_f, o_b, g3, g4, c_norm_flat, wb, layer):
    rows = attn.shape[0]
    w = BRANCH_WIDTH
    d = wb.shape[-1]
    tm = 256
    row = lambda c: pl.BlockSpec((tm, w), lambda i: (i, c))
    return pl.pallas_call(
        _merge_kernel,
        grid=(rows // tm,),
        in_specs=[row(0), row(0), row(0), row(0), row(2),
                  pl.BlockSpec((tm, N_BRANCH * d), lambda i: (i, 0)),
                  pl.BlockSpec((None, 1, w), lambda i: (layer, 0, 0)),
                  pl.BlockSpec((None, N_BRANCH, w, d), lambda i: (layer, 0, 0, 0))],
        out_specs=pl.BlockSpec((tm, d), lambda i: (i, 0)),
        out_shape=jax.ShapeDtypeStruct((rows, d), ACT_DTYPE),
        compiler_params=_params(52, 1),
        name="branch_merge",
    )(attn, gm, o_f, o_b, g3, g4, c_norm_flat, wb)


def _route_kernel(m_ref, wo_ref, z_ref, g_ref, mod_ref, whi_ref, wlo_ref, rb_ref,
                  znew_ref, h_ref, info_ref, cnt_ref, carry_ref, logits_ref, wobf_ref):
    i = pl.program_id(0)

    @pl.when(i == 0)
    def _():
        carry_ref[...] = jnp.zeros(carry_ref.shape, carry_ref.dtype)
        logits_ref[...] = jnp.zeros(logits_ref.shape, logits_ref.dtype)
        wobf_ref[...] = wo_ref[...].astype(wobf_ref.dtype)

    prev_logits = logits_ref[(i + 1) % 2]
    y = jnp.dot(m_ref[...], wobf_ref[...], preferred_element_type=F32)
    _route_stage(prev_logits, jnp.where(i >= 1, 1.0, 0.0), info_ref, cnt_ref, carry_ref)
    z = z_ref[...] + mod_ref[2:3, :] * y
    znew_ref[...] = z
    h = _norm_mod(z, g_ref[...], mod_ref[...], 3, 4)
    h_ref[...] = _pack_halves(h)
    h_hi = h.astype(whi_ref.dtype)
    h_lo = (h - h_hi.astype(F32)).astype(whi_ref.dtype)
    logits_ref[i % 2] = (jnp.dot(h_hi, whi_ref[...], preferred_element_type=F32)
                         + jnp.dot(h_hi, wlo_ref[...], preferred_element_type=F32)
                         + jnp.dot(h_lo, whi_ref[...], preferred_element_type=F32)) + rb_ref[...]


def _route_stage(logits, count_gate, info_ref, cnt_ref, carry_ref):
    tm = logits.shape[0]
    lane = lax.broadcasted_iota(jnp.int32, logits.shape, 1).astype(F32)
    first = lambda hit: jnp.min(jnp.where(hit, lane, 4.0 * LANES), axis=-1, keepdims=True)

    is_g = lane < N_GROUPS
    gl = jnp.where(is_g, logits, NEG_INF)
    gmax = jnp.max(gl, axis=-1, keepdims=True)
    g_sel = first(gl == gmax)
    g_w = 1.0 / jnp.sum(jnp.where(is_g, jnp.exp(gl - gmax), 0.0), axis=-1, keepdims=True)

    e_id = lane - N_GROUPS
    in_group = (e_id >= g_sel * EXPERTS_PER_GROUP) & (e_id < (g_sel + 1.0) * EXPERTS_PER_GROUP)
    el = jnp.where(in_group, logits, NEG_INF)
    m1 = jnp.max(el, axis=-1, keepdims=True)
    i1 = first(el == m1)
    el2 = jnp.where(lane == i1, NEG_INF, el)
    m2 = jnp.max(el2, axis=-1, keepdims=True)
    i2 = first(el2 == m2)
    r = jnp.exp(m2 - m1)
    w1 = g_w / (1.0 + r)
    w2 = g_w * r / (1.0 + r)
    e1 = i1 - N_GROUPS
    e2 = i2 - N_GROUPS

    hot1 = lane == e1
    hot2 = lane == e2
    hot = jnp.where(hot1 | hot2, 1.0, 0.0)
    rr = lax.broadcasted_iota(jnp.int32, (tm, tm), 0)
    cc = lax.broadcasted_iota(jnp.int32, (tm, tm), 1)
    tri = jnp.where(cc < rr, 1.0, 0.0).astype(MXU_DTYPE)
    before = jnp.dot(tri, hot.astype(MXU_DTYPE), preferred_element_type=F32) + carry_ref[0:1, :]
    rank1 = jnp.sum(jnp.where(hot1, before, 0.0), axis=-1, keepdims=True)
    rank2 = jnp.sum(jnp.where(hot2, before, 0.0), axis=-1, keepdims=True)
    carry_ref[0:1, :] = carry_ref[0:1, :] + count_gate * jnp.sum(hot, axis=0, keepdims=True)
    cnt_ref[...] = carry_ref[...]

    info = jnp.where(lane == 0, e1, 0.0)
    info = jnp.where(lane == 1, e2, info)
    info = jnp.where(lane == 2, w1, info)
    info = jnp.where(lane == 3, w2, info)
    info = jnp.where(lane == 4, rank1, info)
    info = jnp.where(lane == 5, rank2, info)
    info_ref[...] = info


def _route_call(merged, wo, z, norm_w, modsel, w_hi, w_lo, rbias, layer, ctx_rows):
    rows, d = z.shape
    tm = 256
    ctx_tiles = ctx_rows // tm
    n_tiles = rows // tm
    cur = lambda i: jnp.minimum(i, n_tiles - 1)
    prev = lambda i: jnp.maximum(i - 1, 0)
    return pl.pallas_call(
        _route_kernel,
        grid=(n_tiles + 1,),
        in_specs=[pl.BlockSpec((tm, d), lambda i: (cur(i), 0)),
                  pl.BlockSpec((None, d, d), lambda i: (layer, 0, 0), pipeline_mode=pl.Buffered(1)),
                  pl.BlockSpec((tm, d), lambda i: (cur(i), 0)),
                  pl.BlockSpec((None, 1, d), lambda i: (layer, 0, 0)),
                  pl.BlockSpec((None, None, MOD_ROWS, d),
                               lambda i: (layer, jnp.where(cur(i) >= ctx_tiles, 1, 0), 0, 0)),
                  pl.BlockSpec((None, d, LANES), lambda i: (layer, 0, 0)),
                  pl.BlockSpec((None, d, LANES), lambda i: (layer, 0, 0)),
                  pl.BlockSpec((None, 1, LANES), lambda i: (layer, 0, 0))],
        out_specs=[pl.BlockSpec((tm, d), lambda i: (cur(i), 0)),
                   pl.BlockSpec((tm, d // 2), lambda i: (cur(i), 0)),
                   pl.BlockSpec((tm, LANES), lambda i: (prev(i), 0)),
                   pl.BlockSpec((8, LANES), lambda i: (0, 0))],
        out_shape=[jax.ShapeDtypeStruct((rows, d), F32),
                   jax.ShapeDtypeStruct((rows, d // 2), PACK_DTYPE),
                   jax.ShapeDtypeStruct((rows, LANES), F32),
                   jax.ShapeDtypeStruct((8, LANES), F32)],
        scratch_shapes=[pltpu.VMEM((8, LANES), F32), pltpu.VMEM((2, tm, LANES), F32),
                        pltpu.VMEM((d, d), MXU_DTYPE)],
        compiler_params=_params(54, 1),
        name="outproj_route",
    )(merged, wo, z, norm_w, modsel, w_hi, w_lo, rbias)


def _dispatch_kernel(pad_end_ref, padded_ref, n_used_ref, dest_ref, h_ref, xs_ref, zero_ref, sem, zero_sem):
    tm = h_ref.shape[0]
    n_blocks = xs_ref.shape[0] // MOE_BLOCK

    @pl.when(pl.program_id(0) == 0)
    def _():
        zero_ref[...] = jnp.zeros(zero_ref.shape, zero_ref.dtype)

        def block_copy(first):
            first = pl.multiple_of(first, MOE_BLOCK)
            return pltpu.make_async_copy(zero_ref, xs_ref.at[pl.ds(first, MOE_BLOCK)], zero_sem)

        def start_unused(b, carry):
            block_copy(b * MOE_BLOCK).start()
            return carry

        def wait_unused(b, carry):
            block_copy(b * MOE_BLOCK).wait()
            return carry

        for e in range(N_EXPERTS):
            @pl.when(padded_ref[e] > 0)
            def _(e=e):
                block_copy(pad_end_ref[e] - MOE_BLOCK).start()
        lax.fori_loop(n_used_ref[0], n_blocks, start_unused, 0)
        for e in range(N_EXPERTS):
            @pl.when(padded_ref[e] > 0)
            def _(e=e):
                block_copy(pad_end_ref[e] - MOE_BLOCK).wait()
        lax.fori_loop(n_used_ref[0], n_blocks, wait_unused, 0)

    def row_copy(t, slot):
        return pltpu.make_async_copy(h_ref.at[pl.ds(t, 1)], xs_ref.at[pl.ds(slot, 1)], sem)

    def start(t, carry):
        row_copy(t, dest_ref[0, 0, 2 * t]).start()
        row_copy(t, dest_ref[0, 0, 2 * t + 1]).start()
        return carry

    lax.fori_loop(0, tm, start, 0, unroll=16)
    all_rows = pltpu.make_async_copy(h_ref, xs_ref.at[pl.ds(0, tm)], sem)
    all_rows.wait()
    all_rows.wait()


def _dispatch_call(pad_end, padded, n_used, dest3, h, n_slots):
    rows, d = h.shape
    tm = dest3.shape[2] // 2
    grid_spec = pltpu.PrefetchScalarGridSpec(
        num_scalar_prefetch=3,
        grid=(rows // tm,),
        in_specs=[pl.BlockSpec((1, 1, 2 * tm), lambda i, pe, pd, nu: (i, 0, 0), memory_space=pltpu.SMEM),
                  pl.BlockSpec((tm, d), lambda i, pe, pd, nu: (i, 0))],
        out_specs=pl.BlockSpec(memory_space=pl.ANY),
        scratch_shapes=[pltpu.VMEM((MOE_BLOCK, d), h.dtype),
                        pltpu.SemaphoreType.DMA(()), pltpu.SemaphoreType.DMA(())])
    return pl.pallas_call(
        _dispatch_kernel,
        grid_spec=grid_spec,
        out_shape=jax.ShapeDtypeStruct((n_slots, d), h.dtype),
        compiler_params=_params(32, 1),
        name="moe_dispatch",
    )(pad_end, padded, n_used, dest3, h)


def _expert_kernel(be_ref, nu_ref, first_ref, slot_ref, next_ref, xs_ref, w1_hbm, w2_hbm, ys_ref,
                   w1f_ref, w2f_ref, w1bf_ref, w2bf_ref, sem, *, layer):
    i = pl.program_id(0)
    used = i < nu_ref[0]

    def fetch(e, s):
        return (pltpu.make_async_copy(w1_hbm.at[layer, e], w1f_ref.at[s], sem.at[s]),
                pltpu.make_async_copy(w2_hbm.at[layer, e], w2f_ref.at[s], sem.at[s]))

    @pl.when(i == 0)
    def _():
        for cp in fetch(be_ref[0], slot_ref[0]):
            cp.start()

    @pl.when(jnp.logical_and(used, first_ref[i] == 1))
    def _():
        s = slot_ref[i]
        for cp in fetch(be_ref[i], s):
            cp.wait()

        @pl.when(next_ref[i] >= 0)
        def _():
            for cp in fetch(next_ref[i], 1 - s):
                cp.start()

        w1bf_ref[...] = w1f_ref[s].astype(w1bf_ref.dtype)
        w2bf_ref[...] = w2f_ref[s].astype(w2bf_ref.dtype)

    @pl.when(used)
    def _():
        de = w2bf_ref.shape[0]
        x = jnp.concatenate(_unpack_halves(xs_ref[...]), axis=1).astype(w1bf_ref.dtype)
        hcat = jnp.dot(x, w1bf_ref[...], preferred_element_type=F32)
        act = _silu(hcat[:, :de]) * hcat[:, de:]
        ys_ref[...] = _pack_halves(jnp.dot(act.astype(w2bf_ref.dtype), w2bf_ref[...], preferred_element_type=F32))

    @pl.when(jnp.logical_not(used))
    def _():
        ys_ref[...] = jnp.zeros(ys_ref.shape, ys_ref.dtype)


def _expert_call(plan, xs, w_e1, w_e2, layer):
    slots, dp = xs.shape
    d = 2 * dp
    de = w_e2.shape[2]
    nb = slots // MOE_BLOCK
    blk = lambda i, be, nu, *_: (jnp.minimum(i, nu[0] - 1), 0)
    grid_spec = pltpu.PrefetchScalarGridSpec(
        num_scalar_prefetch=5,
        grid=(nb,),
        in_specs=[pl.BlockSpec((MOE_BLOCK, dp), blk),
                  pl.BlockSpec(memory_space=pl.ANY),
                  pl.BlockSpec(memory_space=pl.ANY)],
        out_specs=pl.BlockSpec((MOE_BLOCK, dp), lambda i, *_: (i, 0)),
        scratch_shapes=[pltpu.VMEM((2, d, 2 * de), w_e1.dtype), pltpu.VMEM((2, de, d), w_e2.dtype),
                        pltpu.VMEM((d, 2 * de), MXU_DTYPE), pltpu.VMEM((de, d), MXU_DTYPE),
                        pltpu.SemaphoreType.DMA((2,))])
    return pl.pallas_call(
        functools.partial(_expert_kernel, layer=layer),
        grid_spec=grid_spec,
        out_shape=jax.ShapeDtypeStruct((slots, dp), PACK_DTYPE),
        compiler_params=_params(48, 1),
        name="moe_experts",
    )(plan["block_e"], plan["n_used"], plan["first"], plan["slot"], plan["next_e"], xs, w_e1, w_e2)


def _combine_kernel(dest_ref, dest_next_ref, ys_ref, info_ref, z_ref, mod_ref, *rest, emit_next):
    if emit_next:
        gn_ref, modn_ref, o_ref, hn_ref, buf_ref, sem = rest
    else:
        o_ref, buf_ref, sem = rest
    i = pl.program_id(0)
    tm = z_ref.shape[0]
    slot = i % 2

    def issue(d_ref, s):
        for t in range(tm):
            for k in range(2):
                pltpu.make_async_copy(ys_ref.at[pl.ds(d_ref[0, 0, 2 * t + k], 1)],
                                      buf_ref.at[s, k, pl.ds(t, 1)], sem.at[s]).start()

    @pl.when(i == 0)
    def _():
        issue(dest_ref, 0)

    @pl.when(i + 1 < pl.num_programs(0))
    def _():
        issue(dest_next_ref, 1 - slot)

    for k in range(2):
        pltpu.make_async_copy(ys_ref.at[pl.ds(0, tm)], buf_ref.at[slot, k], sem.at[slot]).wait()
    info = info_ref[...]
    lane = lax.broadcasted_iota(jnp.int32, info.shape, 1)
    w1 = jnp.sum(jnp.where(lane == 2, info, 0.0), axis=-1, keepdims=True)
    w2 = jnp.sum(jnp.where(lane == 3, info, 0.0), axis=-1, keepdims=True)
    lo1, hi1 = _unpack_halves(buf_ref[slot, 0])
    lo2, hi2 = _unpack_halves(buf_ref[slot, 1])
    y = jnp.concatenate([lo1 * w1 + lo2 * w2, hi1 * w1 + hi2 * w2], axis=1)
    z = z_ref[...] + mod_ref[5:6, :] * y
    o_ref[...] = z
    if emit_next:
        hn_ref[...] = _norm_mod(z, gn_ref[...], modn_ref[...], 0, 1).astype(hn_ref.dtype)


def _combine_call(dest3, ys, info, z, modsel, layer, ctx_rows, next_norm_w=None):
    rows, d = z.shape
    tm = dest3.shape[2] // 2
    ctx_tiles = ctx_rows // tm
    n_tiles = rows // tm
    emit_next = next_norm_w is not None
    mod_spec = lambda l: pl.BlockSpec((None, None, MOD_ROWS, d),
                                      lambda i: (l, jnp.where(i >= ctx_tiles, 1, 0), 0, 0))
    row_spec = pl.BlockSpec((tm, d), lambda i: (i, 0))
    in_specs = [pl.BlockSpec((1, 1, 2 * tm), lambda i: (i, 0, 0), memory_space=pltpu.SMEM),
                pl.BlockSpec((1, 1, 2 * tm), lambda i: (jnp.minimum(i + 1, n_tiles - 1), 0, 0),
                             memory_space=pltpu.SMEM),
                pl.BlockSpec(memory_space=pl.ANY),
                pl.BlockSpec((tm, LANES), lambda i: (i, 0)),
                row_spec,
                mod_spec(layer)]
    args = [dest3, dest3, ys, info, z, modsel]
    if emit_next:
        out_specs = [row_spec]
        out_shape = [jax.ShapeDtypeStruct((rows, d), F32)]
    else:
        out_specs = [pl.BlockSpec((tm, d), lambda i: (jnp.maximum(i - ctx_tiles, 0), 0))]
        out_shape = [jax.ShapeDtypeStruct((rows - ctx_rows, d), F32)]
    if emit_next:
        in_specs += [pl.BlockSpec((None, 1, d), lambda i: (layer + 1, 0, 0)), mod_spec(layer + 1)]
        args += [next_norm_w, modsel]
        out_specs.append(row_spec)
        out_shape.append(jax.ShapeDtypeStruct((rows, d), ACT_DTYPE))
    return pl.pallas_call(
        functools.partial(_combine_kernel, emit_next=emit_next),
        grid=(n_tiles,),
        in_specs=in_specs,
        out_specs=out_specs,
        out_shape=out_shape,
        scratch_shapes=[pltpu.VMEM((2, 2, tm, ys.shape[1]), ys.dtype), pltpu.SemaphoreType.DMA((2,))],
        compiler_params=_params(40, 1),
        name="moe_combine",
    )(*args)


def _rope_tables(n, ctx_rows):
    rows = n // GRID_W
    row = jnp.repeat(jnp.arange(rows, dtype=F32), GRID_W)
    col = jnp.tile(jnp.arange(GRID_W, dtype=F32), rows)
    nq = HEAD_DIM // 4
    inv = ROPE_BASE ** (-jnp.arange(nq, dtype=F32) / nq)
    ar, ac = row[:, None] * inv, col[:, None] * inv
    cos = jnp.concatenate([jnp.cos(ar), jnp.cos(ar), jnp.cos(ac), jnp.cos(ac)], axis=1)
    sin = jnp.concatenate([-jnp.sin(ar), jnp.sin(ar), -jnp.sin(ac), jnp.sin(ac)], axis=1)
    cos = jnp.concatenate([jnp.ones((ctx_rows, HEAD_DIM), F32), cos], axis=0)
    sin = jnp.concatenate([jnp.zeros((ctx_rows, HEAD_DIM), F32), sin], axis=0)
    return cos, sin


def _retention_tables(c_decay_fwd, c_decay_bwd):
    lg_f = jax.nn.log_sigmoid(c_decay_fwd.astype(F32))[:, :, None, None]
    lg_b = jax.nn.log_sigmoid(c_decay_bwd.astype(F32))[:, :, None, None]
    idx = jnp.arange(CHUNK, dtype=F32)
    diff = idx[:, None] - idx[None, :]
    ones = jnp.ones((CHUNK, CHUNK), F32)
    t_col = idx[:, None] * ones
    intra_f = jnp.where(diff >= 0, jnp.exp(lg_f * jnp.maximum(diff, 0.0)), 0.0)
    intra_b = jnp.where(diff <= 0, jnp.exp(lg_b * jnp.maximum(-diff, 0.0)), 0.0)
    qdec_f = jnp.exp(lg_f * (t_col + 1.0))
    qdec_b = jnp.exp(lg_b * (CHUNK - t_col))
    kdec_f = jnp.exp(lg_f * (CHUNK - 1.0 - t_col))
    kdec_b = jnp.exp(lg_b * t_col)
    cdec_f = jnp.exp(lg_f * CHUNK) * ones
    cdec_b = jnp.exp(lg_b * CHUNK) * ones
    pair = lambda a, b: jnp.stack([a, b], axis=1)
    return (pair(intra_f, intra_b), pair(qdec_f, qdec_b), pair(kdec_f, kdec_b), pair(cdec_f, cdec_b))


def _moe_plan(info, counts_row, n_slots_blocks):
    e = info[:, 0:2].astype(jnp.int32)
    rank = info[:, 4:6].astype(jnp.int32)
    counts = counts_row[:N_EXPERTS].astype(jnp.int32)
    padded = (counts + MOE_BLOCK - 1) // MOE_BLOCK * MOE_BLOCK
    pad_end = jnp.cumsum(padded)
    pad_start = pad_end - padded
    hit = e[:, :, None] == jnp.arange(N_EXPERTS, dtype=jnp.int32)
    dest = jnp.sum(jnp.where(hit, pad_start, 0), axis=-1) + rank
    n_used = pad_end[-1] // MOE_BLOCK
    blocks = jnp.arange(n_slots_blocks, dtype=jnp.int32)
    first_slot = jnp.minimum(blocks, n_used - 1) * MOE_BLOCK
    block_e = jnp.sum((pad_end[None, :] <= first_slot[:, None]).astype(jnp.int32), axis=1)
    ids = jnp.arange(N_EXPERTS, dtype=jnp.int32)
    nonempty = counts > 0
    slot_of = (jnp.cumsum(nonempty.astype(jnp.int32)) - 1) % 2
    later = nonempty[None, :] & (ids[None, :] > ids[:, None])
    next_of = jnp.min(jnp.where(later, ids[None, :], N_EXPERTS), axis=1)
    next_of = jnp.where(next_of == N_EXPERTS, -1, next_of)
    pick = lambda table: jnp.sum(jnp.where(block_e[:, None] == ids[None, :], table[None, :], 0), axis=1)
    prev_e = jnp.concatenate([jnp.full((1,), -1, jnp.int32), block_e[:-1]])
    first = ((block_e != prev_e) & (blocks < n_used)).astype(jnp.int32)
    i32 = lambda a: a.astype(jnp.int32)
    return dict(dest=dest, block_e=i32(block_e), n_used=i32(n_used.reshape(1)), pad_end=i32(pad_end),
                padded=i32(padded), first=first, slot=i32(pick(slot_of)), next_e=i32(pick(next_of)))


def kernel(x, c, ctx, c_ctx, norm_mix, norm_ffn, w_ada, b_ada, w_in, a_q_norm, a_k_norm, a_sink,
           b_norm, b_spatial, b_spatial_bias, c_decay_fwd, c_decay_bwd, c_norm, w_branch, w_out,
           w_router_group, b_router_group, w_router_expert, b_router_expert, w_expert_in, w_expert_out):
    batch, n, d = x.shape
    ctx_rows = ctx.shape[1]
    depth = w_in.shape[0]
    assert batch == 1 and ctx_rows % 256 == 0 and n % 256 == 0
    rows = ctx_rows + n

    cond = jnp.stack([c[0], c_ctx], axis=0)
    cond_b = jnp.broadcast_to(cond[:, :, None], (2, d, LANES))
    mod = _ada_call(cond_b, w_ada, b_ada)[:, :2].reshape(depth, 2, N_MOD, d)
    modsel = jnp.pad(mod[:, ::-1], ((0, 0), (0, 0), (0, MOD_ROWS - N_MOD), (0, 0)))

    cos_t, sin_t = _rope_tables(n, ctx_rows)
    ret_tables = _retention_tables(c_decay_fwd, c_decay_bwd)
    sink_col = jnp.broadcast_to(a_sink.astype(F32).reshape(depth, A_KV_HEADS, A_GROUP, 1, 1),
                                (depth, A_KV_HEADS, A_GROUP, A_BLOCK, 1)).reshape(depth, A_KV_HEADS, A_GROUP * A_BLOCK, 1)
    ws = b_spatial.astype(MXU_DTYPE)
    bias_b = jnp.broadcast_to(b_spatial_bias.astype(F32)[:, :, :, None], (depth, B_GROUPS, CHUNK, LANES))
    wb = w_branch.astype(MXU_DTYPE)
    wo = w_out
    w_r = jnp.concatenate([w_router_group, w_router_expert], axis=-1).astype(F32)
    w_r = jnp.pad(w_r, ((0, 0), (0, 0), (0, LANES - w_r.shape[-1])))
    w_r_hi = w_r.astype(MXU_DTYPE)
    w_r_lo = (w_r - w_r_hi.astype(F32)).astype(MXU_DTYPE)
    b_r = jnp.concatenate([b_router_group, b_router_expert], axis=-1).astype(F32)
    b_r = jnp.pad(b_r, ((0, 0), (0, LANES - b_r.shape[-1]))).reshape(depth, 1, LANES)
    norm_mix3 = norm_mix.reshape(depth, 1, d)
    norm_ffn3 = norm_ffn.reshape(depth, 1, d)
    lane_ids = jnp.arange(HEAD_DIM)
    partner = jnp.where((lane_ids // 32) % 2 == 0, lane_ids + 32, lane_ids - 32)
    perm = (lane_ids[:, None] == partner[None, :]).astype(ACT_DTYPE)
    a_q_norm3 = jnp.stack([a_q_norm, a_q_norm[:, partner]], axis=1).astype(F32)
    a_k_norm3 = jnp.stack([a_k_norm, a_k_norm[:, partner]], axis=1).astype(F32)
    b_norm3 = b_norm.reshape(depth, 1, BRANCH_WIDTH)
    c_norm3 = c_norm.reshape(depth, 1, BRANCH_WIDTH)

    kvw = A_KV_HEADS * HEAD_DIM
    w = BRANCH_WIDTH
    n_assign = rows * 2
    n_slot_blocks = -(-(n_assign + N_EXPERTS * (MOE_BLOCK - 1)) // MOE_BLOCK)
    tok_tile = 256
    disp_tile = _row_tile(rows, (1056, 768, 256))

    z, h = _first_norm_call(ctx[0], x[0], norm_mix3, modsel)
    for l in range(depth):
        g12 = _proj_call(h, w_in, l, 0, 2 * kvw + 2 * w)
        g3 = _proj_call(h, w_in, l, 2 * kvw + 2 * w, 5 * w)
        g4 = _proj_call(h, w_in, l, 2 * kvw + 7 * w, N_BRANCH * d)
        kn, qn, kr, qr = _prep_call(g12, g3, cos_t, sin_t, a_q_norm3, a_k_norm3, perm, l)
        attn = _attn_call(qn, kn, g12, sink_col, l, ctx_rows)
        gm = _gmlp_call(g3, b_norm3, ws, bias_b, l)
        o_f, o_b = _ret_call(qr, kr, g12, ret_tables, l, ctx_rows)
        merged = _merge_call(attn, gm, o_f, o_b, g3, g4, c_norm3, wb, l)
        z, h2, info, counts = _route_call(merged, wo, z, norm_ffn3, modsel, w_r_hi, w_r_lo, b_r, l, ctx_rows)
        plan = _moe_plan(info, counts[0], n_slot_blocks)
        dest3 = plan["dest"].reshape(rows // tok_tile, 1, 2 * tok_tile)
        dest3_disp = plan["dest"].reshape(rows // disp_tile, 1, 2 * disp_tile)
        xs = _dispatch_call(plan["pad_end"], plan["padded"], plan["n_used"], dest3_disp, h2,
                            n_slot_blocks * MOE_BLOCK)
        ys = _expert_call(plan, xs, w_expert_in, w_expert_out, l)
        if l + 1 < depth:
            z, h = _combine_call(dest3, ys, info, z, modsel, l, ctx_rows, next_norm_w=norm_mix3)
        else:
            (z_latent,) = _combine_call(dest3, ys, info, z, modsel, l, ctx_rows)
    return z_latent[None]
```

```python
import functools
import math

import jax
import jax.numpy as jnp
from jax import lax
from jax.experimental import pallas as pl
from jax.experimental.pallas import tpu as pltpu

F32 = jnp.float32
MXU_DTYPE = jnp.bfloat16
ACT_DTYPE = jnp.bfloat16

LANES = 128
HEAD_DIM = 128
GRID_W = 64
ROPE_BASE = 10000.0
EPS = 1e-6
NEG_INF = -1e30
A_Q_HEADS = 8
A_KV_HEADS = 2
A_GROUP = A_Q_HEADS // A_KV_HEADS
A_BLOCK = 128
B_GROUPS = 8
C_HEADS = 8
CHUNK = 128
N_GROUPS = 4
EXPERTS_PER_GROUP = 8
N_EXPERTS = N_GROUPS * EXPERTS_PER_GROUP
D_EXPERT = 512
BRANCH_WIDTH = 1024
N_BRANCH = 3
MOE_BLOCK = 320
N_MOD = 6
MOD_ROWS = 8
MIB = 1024 * 1024


def _params(vmem_mib, n_grid, **kw):
    return pltpu.CompilerParams(dimension_semantics=("arbitrary",) * n_grid,
                                vmem_limit_bytes=vmem_mib * MIB, **kw)


def _sigmoid(x):
    return 0.5 + 0.5 * jnp.tanh(0.5 * x)


def _silu(x):
    return x * _sigmoid(x)


PACK_DTYPE = jnp.uint32


def _pack_halves(x):
    n = x.shape[1] // 2
    rounded = lambda v: lax.bitcast_convert_type(v.astype(jnp.bfloat16).astype(F32), PACK_DTYPE)
    return rounded(x[:, n:]) | (rounded(x[:, :n]) >> 16)


def _unpack_halves(p):
    lo = lax.bitcast_convert_type(p << 16, F32)
    hi = lax.bitcast_convert_type(p & jnp.asarray(0xFFFF0000, PACK_DTYPE), F32)
    return lo, hi


def _gelu_tanh(x):
    return 0.5 * x * (1.0 + jnp.tanh(math.sqrt(2.0 / math.pi) * (x + 0.044715 * (x * x * x))))


def _ada_kernel(c_ref, w_ref, b_ref, o_ref):
    tn = w_ref.shape[1]
    s0 = _silu(c_ref[0])
    s1 = _silu(c_ref[1])
    o_ref[...] = jnp.zeros(o_ref.shape, o_ref.dtype)
    for j in range(tn // LANES):
        sl = slice(j * LANES, (j + 1) * LANES)
        wj = w_ref[:, sl]
        o_ref[0:1, sl] = jnp.sum(wj * s0, axis=0, keepdims=True) + b_ref[:, sl]
        o_ref[1:2, sl] = jnp.sum(wj * s1, axis=0, keepdims=True) + b_ref[:, sl]


def _ada_call(cond_b, w_ada, b_ada):
    depth, k, n = w_ada.shape
    tn = 2048
    return pl.pallas_call(
        _ada_kernel,
        grid=(depth, n // tn),
        in_specs=[pl.BlockSpec((2, k, LANES), lambda l, j: (0, 0, 0)),
                  pl.BlockSpec((None, k, tn), lambda l, j: (l, 0, j)),
                  pl.BlockSpec((None, 1, tn), lambda l, j: (l, 0, j))],
        out_specs=pl.BlockSpec((None, 8, tn), lambda l, j: (l, 0, j)),
        out_shape=jax.ShapeDtypeStruct((depth, 8, n), F32),
        compiler_params=_params(48, 2),
        name="adaln",
    )(cond_b, w_ada, b_ada.reshape(depth, 1, n))


def _norm_mod(z, g, mod, shift_row, scale_row):
    r = lax.rsqrt(jnp.mean(z * z, axis=-1, keepdims=True) + EPS)
    return (z * r * g) * (1.0 + mod[scale_row:scale_row + 1, :]) + mod[shift_row:shift_row + 1, :]


def _first_norm_kernel(ctx_ref, x_ref, g_ref, mod_ref, z_ref, h_ref, *, ctx_tiles):
    def emit(src_ref):
        z = src_ref[...]
        z_ref[...] = z
        h_ref[...] = _norm_mod(z, g_ref[...], mod_ref[...], 0, 1).astype(h_ref.dtype)

    @pl.when(pl.program_id(0) < ctx_tiles)
    def _():
        emit(ctx_ref)

    @pl.when(pl.program_id(0) >= ctx_tiles)
    def _():
        emit(x_ref)


def _first_norm_call(ctx2, x2, norm_w, modsel):
    ctx_rows, d = ctx2.shape
    rows = ctx_rows + x2.shape[0]
    tm = 256
    ctx_tiles = ctx_rows // tm
    row_spec = pl.BlockSpec((tm, d), lambda i: (i, 0))
    return pl.pallas_call(
        functools.partial(_first_norm_kernel, ctx_tiles=ctx_tiles),
        grid=(rows // tm,),
        in_specs=[pl.BlockSpec((tm, d), lambda i: (jnp.minimum(i, ctx_tiles - 1), 0)),
                  pl.BlockSpec((tm, d), lambda i: (jnp.maximum(i - ctx_tiles, 0), 0)),
                  pl.BlockSpec((None, 1, d), lambda i: (0, 0, 0)),
                  pl.BlockSpec((None, None, MOD_ROWS, d),
                               lambda i: (0, jnp.where(i >= ctx_tiles, 1, 0), 0, 0))],
        out_specs=[row_spec, row_spec],
        out_shape=[jax.ShapeDtypeStruct((rows, d), F32), jax.ShapeDtypeStruct((rows, d), ACT_DTYPE)],
        compiler_params=_params(32, 1),
        name="stack_norm_mod",
    )(ctx2, x2, norm_w, modsel)


def _proj_kernel(h_ref, w_ref, o_ref, wbf_ref):
    @pl.when(pl.program_id(1) == 0)
    def _():
        wbf_ref[...] = w_ref[...].astype(wbf_ref.dtype)

    o_ref[...] = jnp.dot(h_ref[...], wbf_ref[...], preferred_element_type=F32).astype(o_ref.dtype)


def _row_tile(rows, pref):
    for t in pref:
        if rows % t == 0:
            return t
    raise ValueError(f"no row tile for {rows}")


def _proj_call(h, w, layer, col_off, ncols):
    rows, k = h.shape
    tn = next(t for t in (1536, 1280, 1024, 512) if col_off % t == 0 and ncols % t == 0)
    vmem_mib = 52
    need = lambda t: (2 * 4 + 2) * k * tn + 2 * 2 * t * k + (2 * 2 + 4) * t * tn
    tm = next(t for t in (1056, 768, 512, 256) if rows % t == 0 and need(t) <= (vmem_mib - 4) * MIB)
    off = col_off // tn
    return pl.pallas_call(
        _proj_kernel,
        grid=(ncols // tn, rows // tm),
        in_specs=[pl.BlockSpec((tm, k), lambda j, i: (i, 0)),
                  pl.BlockSpec((None, k, tn), lambda j, i: (layer, 0, off + j))],
        out_specs=pl.BlockSpec((tm, tn), lambda j, i: (i, j)),
        out_shape=jax.ShapeDtypeStruct((rows, ncols), ACT_DTYPE),
        scratch_shapes=[pltpu.VMEM((k, tn), MXU_DTYPE)],
        compiler_params=_params(vmem_mib, 2),
        name="proj_in",
    )(h, w)


def _prep_kernel(ak_ref, ck0_ref, ck1_ref, aq_ref, cq_ref, cos_ref, sin_ref, qn_ref, kn_ref, perm_ref,
                 okn_ref, oqn_ref, okr_ref, oqr_ref):
    cos = cos_ref[...]
    sin = sin_ref[...]
    perm = perm_ref[...]
    scale = HEAD_DIM ** -0.5

    def swap(x):
        return jnp.dot(x, perm, preferred_element_type=F32)

    def norm_rope(x, g_ref, out_scale):
        xf = x.astype(F32)
        r = lax.rsqrt(jnp.mean(xf * xf, axis=-1, keepdims=True) + EPS) * out_scale
        return (xf * (g_ref[0:1, :] * cos) + swap(x) * (g_ref[1:2, :] * sin)) * r

    for h in range(A_KV_HEADS):
        sl = slice(h * HEAD_DIM, (h + 1) * HEAD_DIM)
        okn_ref[:, sl] = norm_rope(ak_ref[:, sl], kn_ref, 1.0).astype(okn_ref.dtype)
    for h in range(A_Q_HEADS):
        sl = slice(h * HEAD_DIM, (h + 1) * HEAD_DIM)
        oqn_ref[:, sl] = norm_rope(aq_ref[:, sl], qn_ref, scale).astype(oqn_ref.dtype)
    for h in range(C_HEADS):
        sl = slice(h * HEAD_DIM, (h + 1) * HEAD_DIM)
        half = C_HEADS // 2
        k = (ck0_ref if h < half else ck1_ref)[:, (h % half) * HEAD_DIM:(h % half + 1) * HEAD_DIM]
        q = cq_ref[:, sl]
        okr_ref[:, sl] = ((k.astype(F32) * cos + swap(k) * sin) * scale).astype(okr_ref.dtype)
        oqr_ref[:, sl] = (q.astype(F32) * cos + swap(q) * sin).astype(oqr_ref.dtype)


def _prep_call(g12, g3, cos_t, sin_t, a_q_norm, a_k_norm, perm, layer):
    rows = g12.shape[0]
    tm = _row_tile(rows, (768, 512, 256))
    w = BRANCH_WIDTH
    kvw = A_KV_HEADS * HEAD_DIM
    return pl.pallas_call(
        _prep_kernel,
        grid=(rows // tm,),
        in_specs=[pl.BlockSpec((tm, kvw), lambda i: (i, 0)),
                  pl.BlockSpec((tm, w // 2), lambda i: (i, 1)),
                  pl.BlockSpec((tm, w // 2), lambda i: (i, 2)),
                  pl.BlockSpec((tm, w), lambda i: (i, 0)),
                  pl.BlockSpec((tm, w), lambda i: (i, 1)),
                  pl.BlockSpec((tm, LANES), lambda i: (i, 0)),
                  pl.BlockSpec((tm, LANES), lambda i: (i, 0)),
                  pl.BlockSpec((None, 2, HEAD_DIM), lambda i: (layer, 0, 0)),
                  pl.BlockSpec((None, 2, HEAD_DIM), lambda i: (layer, 0, 0)),
                  pl.BlockSpec((HEAD_DIM, HEAD_DIM), lambda i: (0, 0))],
        out_specs=[pl.BlockSpec((tm, kvw), lambda i: (i, 0)),
                   pl.BlockSpec((tm, w), lambda i: (i, 0)),
                   pl.BlockSpec((tm, w), lambda i: (i, 0)),
                   pl.BlockSpec((tm, w), lambda i: (i, 0))],
        out_shape=[jax.ShapeDtypeStruct((rows, kvw), ACT_DTYPE),
                   jax.ShapeDtypeStruct((rows, w), ACT_DTYPE),
                   jax.ShapeDtypeStruct((rows, w), ACT_DTYPE),
                   jax.ShapeDtypeStruct((rows, w), ACT_DTYPE)],
        compiler_params=_params(40, 1),
        name="qk_prep",
    )(g12, g12, g12, g3, g3, cos_t, sin_t, a_q_norm, a_k_norm, perm)


def _attn_kernel(q_ref, kp_ref, km_ref, kn_ref, kc_ref, vp_ref, vm_ref, vn_ref, vc_ref, sink_ref,
                 o_ref, *, ctx_blocks, n_blocks):
    blk = A_BLOCK
    n_keys = 3 * blk + kc_ref.shape[0]
    row = lax.broadcasted_iota(jnp.int32, (A_GROUP * blk, n_keys), 0) % blk
    col = lax.broadcasted_iota(jnp.int32, (A_GROUP * blk, n_keys), 1)
    in_window = (row >= jnp.maximum(col - 2 * blk, 0)) & (row <= jnp.where(col < blk, col, blk - 1))
    is_ctx_key = col >= 3 * blk

    def valid_mask(rb):
        is_lat = rb >= ctx_blocks
        c_lo = jnp.where(rb >= ctx_blocks + 1, 0, blk)
        c_hi = jnp.where(rb <= n_blocks - 2, 3 * blk, 2 * blk)
        c_lo = jnp.where(is_lat, c_lo, 3 * blk)
        c_hi = jnp.where(is_lat, c_hi, 0)
        return ((col >= c_lo) & (col < c_hi) & in_window) | is_ctx_key

    hs = lambda hk: slice(hk * HEAD_DIM, (hk + 1) * HEAD_DIM)
    qs = lambda hk, g: slice((hk * A_GROUP + g) * HEAD_DIM, (hk * A_GROUP + g + 1) * HEAD_DIM)
    lo_rows, hi_rows = slice(0, blk), slice(blk, 2 * blk)

    def band(j, prev_ref, mid_ref, next_ref, ctx_ref, hk):
        parts = ((prev_ref[:, hs(hk)], mid_ref[lo_rows, hs(hk)], mid_ref[hi_rows, hs(hk)]) if j == 0 else
                 (mid_ref[lo_rows, hs(hk)], mid_ref[hi_rows, hs(hk)], next_ref[:, hs(hk)]))
        return jnp.concatenate(parts + (ctx_ref[:, hs(hk)],), axis=0)

    chains = [(j, hk) for j in range(2) for hk in range(A_KV_HEADS)]
    masks = [valid_mask(2 * pl.program_id(0) + j) for j in range(2)]
    scores = []
    for j, hk in chains:
        q = jnp.concatenate([q_ref[j * blk:(j + 1) * blk, qs(hk, g)] for g in range(A_GROUP)], axis=0)
        k = band(j, kp_ref, km_ref, kn_ref, kc_ref, hk)
        scores.append(lax.dot_general(q, k, (((1,), (1,)), ((), ())), preferred_element_type=F32))
    probs, denoms = [], []
    for (j, hk), s in zip(chains, scores):
        s = jnp.where(masks[j], s, NEG_INF)
        sink = sink_ref[hk]
        m = jnp.maximum(jnp.max(s, axis=-1, keepdims=True), sink)
        p = jnp.exp(s - m)
        denoms.append(jnp.sum(p, axis=-1, keepdims=True) + jnp.exp(sink - m))
        probs.append(p.astype(vm_ref.dtype))
    for (j, hk), p, den in zip(chains, probs, denoms):
        v = band(j, vp_ref, vm_ref, vn_ref, vc_ref, hk)
        o = jnp.dot(p, v, preferred_element_type=F32) / den
        for g in range(A_GROUP):
            o_ref[j * blk:(j + 1) * blk, qs(hk, g)] = o[g * blk:(g + 1) * blk].astype(o_ref.dtype)


def _attn_call(qn, kn, g12, sink_col, layer, ctx_rows):
    rows = qn.shape[0]
    blk = A_BLOCK
    nb = rows // blk
    cb = ctx_rows // blk
    assert nb % 2 == 0 and cb % 2 == 0
    qw = A_Q_HEADS * HEAD_DIM
    kvw = A_KV_HEADS * HEAD_DIM
    before = lambda i: jnp.maximum(2 * i - 1, 0)
    after = lambda i: jnp.minimum(2 * i + 2, nb - 1)
    edge = lambda f, c: pl.BlockSpec((blk, kvw), lambda i: (f(i), c))
    pair = lambda c: pl.BlockSpec((2 * blk, kvw), lambda i: (i, c))
    return pl.pallas_call(
        functools.partial(_attn_kernel, ctx_blocks=cb, n_blocks=nb),
        grid=(nb // 2,),
        in_specs=[pl.BlockSpec((2 * blk, qw), lambda i: (i, 0)),
                  edge(before, 0), pair(0), edge(after, 0),
                  pl.BlockSpec((ctx_rows, kvw), lambda i: (0, 0)),
                  edge(before, 1), pair(1), edge(after, 1),
                  pl.BlockSpec((ctx_rows, kvw), lambda i: (0, 1)),
                  pl.BlockSpec((None, A_KV_HEADS, A_GROUP * blk, 1), lambda i: (layer, 0, 0, 0))],
        out_specs=pl.BlockSpec((2 * blk, qw), lambda i: (i, 0)),
        out_shape=jax.ShapeDtypeStruct((rows, qw), ACT_DTYPE),
        compiler_params=_params(40, 1),
        name="window_attn",
    )(qn, kn, kn, kn, kn, g12, g12, g12, g12, sink_col)


GMLP_CHUNKS_PER_STEP = 2


def _gmlp_kernel(u_ref, v_ref, bn_ref, ws_ref, bias_ref, o_ref):
    chunks = range(u_ref.shape[0] // CHUNK)
    rows = lambda c: slice(c * CHUNK, (c + 1) * CHUNK)
    for g in range(B_GROUPS):
        sl = slice(g * LANES, (g + 1) * LANES)
        normed = []
        for c in chunks:
            v = _gelu_tanh(v_ref[rows(c), sl].astype(F32))
            vc = v - jnp.mean(v, axis=-1, keepdims=True)
            vh = vc * lax.rsqrt(jnp.mean(vc * vc, axis=-1, keepdims=True) + EPS) * bn_ref[:, sl]
            normed.append(vh.astype(ws_ref.dtype))
        mixed = jnp.dot(ws_ref[g], jnp.concatenate(normed, axis=1), preferred_element_type=F32)
        for c in chunks:
            m = mixed[:, c * LANES:(c + 1) * LANES] + bias_ref[g]
            o_ref[rows(c), sl] = (_gelu_tanh(u_ref[rows(c), sl].astype(F32)) * m).astype(o_ref.dtype)


def _gmlp_call(g3, b_norm_flat, ws, bias_b, layer):
    rows = g3.shape[0]
    w = BRANCH_WIDTH
    tm = GMLP_CHUNKS_PER_STEP * CHUNK
    return pl.pallas_call(
        _gmlp_kernel,
        grid=(rows // tm,),
        in_specs=[pl.BlockSpec((tm, w), lambda i: (i, 3)),
                  pl.BlockSpec((tm, w), lambda i: (i, 4)),
                  pl.BlockSpec((None, 1, w), lambda i: (layer, 0, 0)),
                  pl.BlockSpec((None, B_GROUPS, CHUNK, CHUNK), lambda i: (layer, 0, 0, 0)),
                  pl.BlockSpec((None, B_GROUPS, CHUNK, LANES), lambda i: (layer, 0, 0, 0))],
        out_specs=pl.BlockSpec((tm, w), lambda i: (i, 0)),
        out_shape=jax.ShapeDtypeStruct((rows, w), ACT_DTYPE),
        compiler_params=_params(32, 1),
        name="chunk_gmlp",
    )(g3, g3, b_norm_flat, ws, bias_b)


def _ret_kernel(qf_ref, kf_ref, vf0_ref, vf1_ref, qb_ref, kb_ref, vb0_ref, vb1_ref,
                intra_ref, qdec_ref, kdec_ref, cdec_ref,
                of_ref, ob_ref, *state_refs):
    @pl.when(pl.program_id(0) == 0)
    def _():
        for s_ref in state_refs:
            s_ref[...] = jnp.zeros(s_ref.shape, s_ref.dtype)

    dirs = ((qf_ref, kf_ref, of_ref), (qb_ref, kb_ref, ob_ref))
    v_halves = ((vf0_ref, vf1_ref), (vb0_ref, vb1_ref))
    lower, upper = slice(0, CHUNK), slice(CHUNK, 2 * CHUNK)
    scan_rows = ((lower, upper), (upper, lower))
    chains = [(d, h) for d in range(2) for h in range(C_HEADS)]
    head = lambda h: slice(h * HEAD_DIM, (h + 1) * HEAD_DIM)
    half = C_HEADS // 2
    value = lambda d, c, h: v_halves[d][h // half][scan_rows[d][c], (h % half) * HEAD_DIM:(h % half + 1) * HEAD_DIM]

    scores = {}
    for c in range(2):
        for d, h in chains:
            q_ref, k_ref, _ = dirs[d]
            q = q_ref[scan_rows[d][c], head(h)]
            k = k_ref[scan_rows[d][c], head(h)]
            a = lax.dot_general(q, k, (((1,), (1,)), ((), ())), preferred_element_type=F32)
            qd = (q.astype(F32) * qdec_ref[d, h]).astype(q.dtype)
            scores[c, d, h] = jnp.concatenate([(a * intra_ref[d, h]).astype(q.dtype), qd], axis=1)
    for c in range(2):
        for d, h in chains:
            o_ref = dirs[d][2]
            v = value(d, c, h)
            rhs = jnp.concatenate([v, state_refs[d * C_HEADS + h][...].astype(v.dtype)], axis=0)
            o_ref[scan_rows[d][c], head(h)] = jnp.dot(scores[c, d, h], rhs,
                                                      preferred_element_type=F32).astype(o_ref.dtype)
        for d, h in chains:
            k_ref = dirs[d][1]
            s_ref = state_refs[d * C_HEADS + h]
            k = k_ref[scan_rows[d][c], head(h)]
            kd = (k.astype(F32) * kdec_ref[d, h]).astype(k.dtype)
            upd = lax.dot_general(kd, value(d, c, h), (((0,), (0,)), ((), ())), preferred_element_type=F32)
            s_ref[...] = s_ref[...] * cdec_ref[d, h] + upd


def _ret_call(qr, kr, g12, tables, layer, ctx_rows):
    rows = qr.shape[0]
    w = BRANCH_WIDTH
    nc = rows // CHUNK
    cc = ctx_rows // CHUNK

    assert nc % 2 == 0 and cc % 2 == 0
    n_pairs, ctx_pairs = nc // 2, cc // 2
    pair_rows = 2 * CHUNK

    def bwd(s):
        return jnp.where(s < ctx_pairs, ctx_pairs - 1 - s, n_pairs - 1 + ctx_pairs - s)

    fwd = lambda s: s
    qk = lambda f: pl.BlockSpec((pair_rows, w), lambda s: (f(s), 0))
    vh = lambda f, c: pl.BlockSpec((pair_rows, w // 2), lambda s: (f(s), c))
    tab = pl.BlockSpec((None, 2, C_HEADS, CHUNK, LANES), lambda s: (layer, 0, 0, 0, 0))
    return pl.pallas_call(
        _ret_kernel,
        grid=(n_pairs,),
        in_specs=[qk(fwd), qk(fwd), vh(fwd, 3), vh(fwd, 4), qk(bwd), qk(bwd), vh(bwd, 3), vh(bwd, 4),
                  tab, tab, tab, tab],
        out_specs=[pl.BlockSpec((pair_rows, w), lambda s: (s, 0)),
                   pl.BlockSpec((pair_rows, w), lambda s: (bwd(s), 0))],
        out_shape=[jax.ShapeDtypeStruct((rows, w), ACT_DTYPE), jax.ShapeDtypeStruct((rows, w), ACT_DTYPE)],
        scratch_shapes=[pltpu.VMEM((HEAD_DIM, HEAD_DIM), F32) for _ in range(2 * C_HEADS)],
        compiler_params=_params(32, 1),
        name="retention",
    )(qr, kr, g12, g12, qr, kr, g12, g12, *tables)


def _merge_kernel(attn_ref, gm_ref, of_ref, ob_ref, rg_ref, gate_ref, cn_ref, wb_ref, o_ref):
    d = o_ref.shape[1]
    ret_parts = []
    for h in range(C_HEADS):
        sl = slice(h * HEAD_DIM, (h + 1) * HEAD_DIM)
        o = of_ref[:, sl].astype(F32) + ob_ref[:, sl].astype(F32)
        oc = o - jnp.mean(o, axis=-1, keepdims=True)
        y = oc * lax.rsqrt(jnp.mean(oc * oc, axis=-1, keepdims=True) + EPS) * cn_ref[:, sl]
        ret_parts.append((_silu(rg_ref[:, sl].astype(F32)) * y).astype(wb_ref.dtype))
    ret = jnp.concatenate(ret_parts, axis=1)
    branches = (attn_ref[...].astype(wb_ref.dtype), gm_ref[...].astype(wb_ref.dtype), ret)
    acc = None
    for b in range(N_BRANCH):
        proj = jnp.dot(branches[b], wb_ref[b], preferred_element_type=F32)
        term = _sigmoid(gate_ref[:, b * d:(b + 1) * d].astype(F32)) * proj
        acc = term if acc is None else acc + term
    o_ref[...] = acc.astype(o_ref.dtype)


def _merge_call(attn, gm, o_f, o_b, g3, g4, c_norm_flat, wb, layer):
    rows = attn.shape[0]
    w = BRANCH_WIDTH
    d = wb.shape[-1]
    tm = 256
    row = lambda c: pl.BlockSpec((tm, w), lambda i: (i, c))
    return pl.pallas_call(
        _merge_kernel,
        grid=(rows // tm,),
        in_specs=[row(0), row(0), row(0), row(0), row(2),
                  pl.BlockSpec((tm, N_BRANCH * d), lambda i: (i, 0)),
                  pl.BlockSpec((None, 1, w), lambda i: (layer, 0, 0)),
                  pl.BlockSpec((None, N_BRANCH, w, d), lambda i: (layer, 0, 0, 0))],
        out_specs=pl.BlockSpec((tm, d), lambda i: (i, 0)),
        out_shape=jax.ShapeDtypeStruct((rows, d), ACT_DTYPE),
        compiler_params=_params(52, 1),
        name="branch_merge",
    )(attn, gm, o_f, o_b, g3, g4, c_norm_flat, wb)


def _route_kernel(m_ref, wo_ref, z_ref, g_ref, mod_ref, whi_ref, wlo_ref, rb_ref,
                  znew_ref, h_ref, info_ref, cnt_ref, carry_ref, logits_ref, wobf_ref):
    i = pl.program_id(0)

    @pl.when(i == 0)
    def _():
        carry_ref[...] = jnp.zeros(carry_ref.shape, carry_ref.dtype)
        logits_ref[...] = jnp.zeros(logits_ref.shape, logits_ref.dtype)
        wobf_ref[...] = wo_ref[...].astype(wobf_ref.dtype)

    prev_logits = logits_ref[(i + 1) % 2]
    y = jnp.dot(m_ref[...], wobf_ref[...], preferred_element_type=F32)
    _route_stage(prev_logits, jnp.where(i >= 1, 1.0, 0.0), info_ref, cnt_ref, carry_ref)
    z = z_ref[...] + mod_ref[2:3, :] * y
    znew_ref[...] = z
    h = _norm_mod(z, g_ref[...], mod_ref[...], 3, 4)
    h_ref[...] = _pack_halves(h)
    h_hi = h.astype(whi_ref.dtype)
    h_lo = (h - h_hi.astype(F32)).astype(whi_ref.dtype)
    logits_ref[i % 2] = (jnp.dot(h_hi, whi_ref[...], preferred_element_type=F32)
                         + jnp.dot(h_hi, wlo_ref[...], preferred_element_type=F32)
                         + jnp.dot(h_lo, whi_ref[...], preferred_element_type=F32)) + rb_ref[...]


def _route_stage(logits, count_gate, info_ref, cnt_ref, carry_ref):
    tm = logits.shape[0]
    lane = lax.broadcasted_iota(jnp.int32, logits.shape, 1).astype(F32)
    first = lambda hit: jnp.min(jnp.where(hit, lane, 4.0 * LANES), axis=-1, keepdims=True)

    is_g = lane < N_GROUPS
    gl = jnp.where(is_g, logits, NEG_INF)
    gmax = jnp.max(gl, axis=-1, keepdims=True)
    g_sel = first(gl == gmax)
    g_w = 1.0 / jnp.sum(jnp.where(is_g, jnp.exp(gl - gmax), 0.0), axis=-1, keepdims=True)

    e_id = lane - N_GROUPS
    in_group = (e_id >= g_sel * EXPERTS_PER_GROUP) & (e_id < (g_sel + 1.0) * EXPERTS_PER_GROUP)
    el = jnp.where(in_group, logits, NEG_INF)
    m1 = jnp.max(el, axis=-1, keepdims=True)
    i1 = first(el == m1)
    el2 = jnp.where(lane == i1, NEG_INF, el)
    m2 = jnp.max(el2, axis=-1, keepdims=True)
    i2 = first(el2 == m2)
    r = jnp.exp(m2 - m1)
    w1 = g_w / (1.0 + r)
    w2 = g_w * r / (1.0 + r)
    e1 = i1 - N_GROUPS
    e2 = i2 - N_GROUPS

    hot1 = lane == e1
    hot2 = lane == e2
    hot = jnp.where(hot1 | hot2, 1.0, 0.0)
    rr = lax.broadcasted_iota(jnp.int32, (tm, tm), 0)
    cc = lax.broadcasted_iota(jnp.int32, (tm, tm), 1)
    tri = jnp.where(cc < rr, 1.0, 0.0).astype(MXU_DTYPE)
    before = jnp.dot(tri, hot.astype(MXU_DTYPE), preferred_element_type=F32) + carry_ref[0:1, :]
    rank1 = jnp.sum(jnp.where(hot1, before, 0.0), axis=-1, keepdims=True)
    rank2 = jnp.sum(jnp.where(hot2, before, 0.0), axis=-1, keepdims=True)
    carry_ref[0:1, :] = carry_ref[0:1, :] + count_gate * jnp.sum(hot, axis=0, keepdims=True)
    cnt_ref[...] = carry_ref[...]

    info = jnp.where(lane == 0, e1, 0.0)
    info = jnp.where(lane == 1, e2, info)
    info = jnp.where(lane == 2, w1, info)
    info = jnp.where(lane == 3, w2, info)
    info = jnp.where(lane == 4, rank1, info)
    info = jnp.where(lane == 5, rank2, info)
    info_ref[...] = info


def _route_call(merged, wo, z, norm_w, modsel, w_hi, w_lo, rbias, layer, ctx_rows):
    rows, d = z.shape
    tm = 256
    ctx_tiles = ctx_rows // tm
    n_tiles = rows // tm
    cur = lambda i: jnp.minimum(i, n_tiles - 1)
    prev = lambda i: jnp.maximum(i - 1, 0)
    return pl.pallas_call(
        _route_kernel,
        grid=(n_tiles + 1,),
        in_specs=[pl.BlockSpec((tm, d), lambda i: (cur(i), 0)),
                  pl.BlockSpec((None, d, d), lambda i: (layer, 0, 0), pipeline_mode=pl.Buffered(1)),
                  pl.BlockSpec((tm, d), lambda i: (cur(i), 0)),
                  pl.BlockSpec((None, 1, d), lambda i: (layer, 0, 0)),
                  pl.BlockSpec((None, None, MOD_ROWS, d),
                               lambda i: (layer, jnp.where(cur(i) >= ctx_tiles, 1, 0), 0, 0)),
                  pl.BlockSpec((None, d, LANES), lambda i: (layer, 0, 0)),
                  pl.BlockSpec((None, d, LANES), lambda i: (layer, 0, 0)),
                  pl.BlockSpec((None, 1, LANES), lambda i: (layer, 0, 0))],
        out_specs=[pl.BlockSpec((tm, d), lambda i: (cur(i), 0)),
                   pl.BlockSpec((tm, d // 2), lambda i: (cur(i), 0)),
                   pl.BlockSpec((tm, LANES), lambda i: (prev(i), 0)),
                   pl.BlockSpec((8, LANES), lambda i: (0, 0))],
        out_shape=[jax.ShapeDtypeStruct((rows, d), F32),
                   jax.ShapeDtypeStruct((rows, d // 2), PACK_DTYPE),
                   jax.ShapeDtypeStruct((rows, LANES), F32),
                   jax.ShapeDtypeStruct((8, LANES), F32)],
        scratch_shapes=[pltpu.VMEM((8, LANES), F32), pltpu.VMEM((2, tm, LANES), F32),
                        pltpu.VMEM((d, d), MXU_DTYPE)],
        compiler_params=_params(54, 1),
        name="outproj_route",
    )(merged, wo, z, norm_w, modsel, w_hi, w_lo, rbias)


def _dispatch_kernel(pad_end_ref, padded_ref, n_used_ref, dest_ref, h_ref, xs_ref, zero_ref, sem, zero_sem):
    tm = h_ref.shape[0]
    n_blocks = xs_ref.shape[0] // MOE_BLOCK

    @pl.when(pl.program_id(0) == 0)
    def _():
        zero_ref[...] = jnp.zeros(zero_ref.shape, zero_ref.dtype)

        def block_copy(first):
            first = pl.multiple_of(first, MOE_BLOCK)
            return pltpu.make_async_copy(zero_ref, xs_ref.at[pl.ds(first, MOE_BLOCK)], zero_sem)

        def start_unused(b, carry):
            block_copy(b * MOE_BLOCK).start()
            return carry

        def wait_unused(b, carry):
            block_copy(b * MOE_BLOCK).wait()
            return carry

        for e in range(N_EXPERTS):
            @pl.when(padded_ref[e] > 0)
            def _(e=e):
                block_copy(pad_end_ref[e] - MOE_BLOCK).start()
        lax.fori_loop(n_used_ref[0], n_blocks, start_unused, 0)
        for e in range(N_EXPERTS):
            @pl.when(padded_ref[e] > 0)
            def _(e=e):
                block_copy(pad_end_ref[e] - MOE_BLOCK).wait()
        lax.fori_loop(n_used_ref[0], n_blocks, wait_unused, 0)

    def row_copy(t, slot):
        return pltpu.make_async_copy(h_ref.at[pl.ds(t, 1)], xs_ref.at[pl.ds(slot, 1)], sem)

    def start(t, carry):
        row_copy(t, dest_ref[0, 0, 2 * t]).start()
        row_copy(t, dest_ref[0, 0, 2 * t + 1]).start()
        return carry

    lax.fori_loop(0, tm, start, 0, unroll=16)
    all_rows = pltpu.make_async_copy(h_ref, xs_ref.at[pl.ds(0, tm)], sem)
    all_rows.wait()
    all_rows.wait()


def _dispatch_call(pad_end, padded, n_used, dest3, h, n_slots):
    rows, d = h.shape
    tm = dest3.shape[2] // 2
    grid_spec = pltpu.PrefetchScalarGridSpec(
        num_scalar_prefetch=3,
        grid=(rows // tm,),
        in_specs=[pl.BlockSpec((1, 1, 2 * tm), lambda i, pe, pd, nu: (i, 0, 0), memory_space=pltpu.SMEM),
                  pl.BlockSpec((tm, d), lambda i, pe, pd, nu: (i, 0))],
        out_specs=pl.BlockSpec(memory_space=pl.ANY),
        scratch_shapes=[pltpu.VMEM((MOE_BLOCK, d), h.dtype),
                        pltpu.SemaphoreType.DMA(()), pltpu.SemaphoreType.DMA(())])
    return pl.pallas_call(
        _dispatch_kernel,
        grid_spec=grid_spec,
        out_shape=jax.ShapeDtypeStruct((n_slots, d), h.dtype),
        compiler_params=_params(32, 1),
        name="moe_dispatch",
    )(pad_end, padded, n_used, dest3, h)


def _expert_kernel(be_ref, nu_ref, first_ref, slot_ref, next_ref, xs_ref, w1_hbm, w2_hbm, ys_ref,
                   w1f_ref, w2f_ref, w1bf_ref, w2bf_ref, sem, *, layer):
    i = pl.program_id(0)
    used = i < nu_ref[0]

    def fetch(e, s):
        return (pltpu.make_async_copy(w1_hbm.at[layer, e], w1f_ref.at[s], sem.at[s]),
                pltpu.make_async_copy(w2_hbm.at[layer, e], w2f_ref.at[s], sem.at[s]))

    @pl.when(i == 0)
    def _():
        for cp in fetch(be_ref[0], slot_ref[0]):
            cp.start()

    @pl.when(jnp.logical_and(used, first_ref[i] == 1))
    def _():
        s = slot_ref[i]
        for cp in fetch(be_ref[i], s):
            cp.wait()

        @pl.when(next_ref[i] >= 0)
        def _():
            for cp in fetch(next_ref[i], 1 - s):
                cp.start()

        w1bf_ref[...] = w1f_ref[s].astype(w1bf_ref.dtype)
        w2bf_ref[...] = w2f_ref[s].astype(w2bf_ref.dtype)

    @pl.when(used)
    def _():
        de = w2bf_ref.shape[0]
        x = jnp.concatenate(_unpack_halves(xs_ref[...]), axis=1).astype(w1bf_ref.dtype)
        hcat = jnp.dot(x, w1bf_ref[...], preferred_element_type=F32)
        act = _silu(hcat[:, :de]) * hcat[:, de:]
        ys_ref[...] = _pack_halves(jnp.dot(act.astype(w2bf_ref.dtype), w2bf_ref[...], preferred_element_type=F32))

    @pl.when(jnp.logical_not(used))
    def _():
        ys_ref[...] = jnp.zeros(ys_ref.shape, ys_ref.dtype)


def _expert_call(plan, xs, w_e1, w_e2, layer):
    slots, dp = xs.shape
    d = 2 * dp
    de = w_e2.shape[2]
    nb = slots // MOE_BLOCK
    blk = lambda i, be, nu, *_: (jnp.minimum(i, nu[0] - 1), 0)
    grid_spec = pltpu.PrefetchScalarGridSpec(
        num_scalar_prefetch=5,
        grid=(nb,),
        in_specs=[pl.BlockSpec((MOE_BLOCK, dp), blk),
                  pl.BlockSpec(memory_space=pl.ANY),
                  pl.BlockSpec(memory_space=pl.ANY)],
        out_specs=pl.BlockSpec((MOE_BLOCK, dp), lambda i, *_: (i, 0)),
        scratch_shapes=[pltpu.VMEM((2, d, 2 * de), w_e1.dtype), pltpu.VMEM((2, de, d), w_e2.dtype),
                        pltpu.VMEM((d, 2 * de), MXU_DTYPE), pltpu.VMEM((de, d), MXU_DTYPE),
                        pltpu.SemaphoreType.DMA((2,))])
    return pl.pallas_call(
        functools.partial(_expert_kernel, layer=layer),
        grid_spec=grid_spec,
        out_shape=jax.ShapeDtypeStruct((slots, dp), PACK_DTYPE),
        compiler_params=_params(48, 1),
        name="moe_experts",
    )(plan["block_e"], plan["n_used"], plan["first"], plan["slot"], plan["next_e"], xs, w_e1, w_e2)


def _combine_kernel(dest_ref, dest_next_ref, ys_ref, info_ref, z_ref, mod_ref, *rest, emit_next):
    if emit_next:
        gn_ref, modn_ref, o_ref, hn_ref, buf_ref, sem = rest
    else:
        o_ref, buf_ref, sem = rest
    i = pl.program_id(0)
    tm = z_ref.shape[0]
    slot = i % 2

    def issue(d_ref, s):
        for t in range(tm):
            for k in range(2):
                pltpu.make_async_copy(ys_ref.at[pl.ds(d_ref[0, 0, 2 * t + k], 1)],
                                      buf_ref.at[s, k, pl.ds(t, 1)], sem.at[s]).start()

    @pl.when(i == 0)
    def _():
        issue(dest_ref, 0)

    @pl.when(i + 1 < pl.num_programs(0))
    def _():
        issue(dest_next_ref, 1 - slot)

    for k in range(2):
        pltpu.make_async_copy(ys_ref.at[pl.ds(0, tm)], buf_ref.at[slot, k], sem.at[slot]).wait()
    info = info_ref[...]
    lane = lax.broadcasted_iota(jnp.int32, info.shape, 1)
    w1 = jnp.sum(jnp.where(lane == 2, info, 0.0), axis=-1, keepdims=True)
    w2 = jnp.sum(jnp.where(lane == 3, info, 0.0), axis=-1, keepdims=True)
    lo1, hi1 = _unpack_halves(buf_ref[slot, 0])
    lo2, hi2 = _unpack_halves(buf_ref[slot, 1])
    y = jnp.concatenate([lo1 * w1 + lo2 * w2, hi1 * w1 + hi2 * w2], axis=1)
    z = z_ref[...] + mod_ref[5:6, :] * y
    o_ref[...] = z
    if emit_next:
        hn_ref[...] = _norm_mod(z, gn_ref[...], modn_ref[...], 0, 1).astype(hn_ref.dtype)


def _combine_call(dest3, ys, info, z, modsel, layer, ctx_rows, next_norm_w=None):
    rows, d = z.shape
    tm = dest3.shape[2] // 2
    ctx_tiles = ctx_rows // tm
    n_tiles = rows // tm
    emit_next = next_norm_w is not None
    mod_spec = lambda l: pl.BlockSpec((None, None, MOD_ROWS, d),
                                      lambda i: (l, jnp.where(i >= ctx_tiles, 1, 0), 0, 0))
    row_spec = pl.BlockSpec((tm, d), lambda i: (i, 0))
    in_specs = [pl.BlockSpec((1, 1, 2 * tm), lambda i: (i, 0, 0), memory_space=pltpu.SMEM),
                pl.BlockSpec((1, 1, 2 * tm), lambda i: (jnp.minimum(i + 1, n_tiles - 1), 0, 0),
                             memory_space=pltpu.SMEM),
                pl.BlockSpec(memory_space=pl.ANY),
                pl.BlockSpec((tm, LANES), lambda i: (i, 0)),
                row_spec,
                mod_spec(layer)]
    args = [dest3, dest3, ys, info, z, modsel]
    if emit_next:
        out_specs = [row_spec]
        out_shape = [jax.ShapeDtypeStruct((rows, d), F32)]
    else:
        out_specs = [pl.BlockSpec((tm, d), lambda i: (jnp.maximum(i - ctx_tiles, 0), 0))]
        out_shape = [jax.ShapeDtypeStruct((rows - ctx_rows, d), F32)]
    if emit_next:
        in_specs += [pl.BlockSpec((None, 1, d), lambda i: (layer + 1, 0, 0)), mod_spec(layer + 1)]
        args += [next_norm_w, modsel]
        out_specs.append(row_spec)
        out_shape.append(jax.ShapeDtypeStruct((rows, d), ACT_DTYPE))
    return pl.pallas_call(
        functools.partial(_combine_kernel, emit_next=emit_next),
        grid=(n_tiles,),
        in_specs=in_specs,
        out_specs=out_specs,
        out_shape=out_shape,
        scratch_shapes=[pltpu.VMEM((2, 2, tm, ys.shape[1]), ys.dtype), pltpu.SemaphoreType.DMA((2,))],
        compiler_params=_params(40, 1),
        name="moe_combine",
    )(*args)


def _rope_tables(n, ctx_rows):
    rows = n // GRID_W
    row = jnp.repeat(jnp.arange(rows, dtype=F32), GRID_W)
    col = jnp.tile(jnp.arange(GRID_W, dtype=F32), rows)
    nq = HEAD_DIM // 4
    inv = ROPE_BASE ** (-jnp.arange(nq, dtype=F32) / nq)
    ar, ac = row[:, None] * inv, col[:, None] * inv
    cos = jnp.concatenate([jnp.cos(ar), jnp.cos(ar), jnp.cos(ac), jnp.cos(ac)], axis=1)
    sin = jnp.concatenate([-jnp.sin(ar), jnp.sin(ar), -jnp.sin(ac), jnp.sin(ac)], axis=1)
    cos = jnp.concatenate([jnp.ones((ctx_rows, HEAD_DIM), F32), cos], axis=0)
    sin = jnp.concatenate([jnp.zeros((ctx_rows, HEAD_DIM), F32), sin], axis=0)
    return cos, sin


def _retention_tables(c_decay_fwd, c_decay_bwd):
    lg_f = jax.nn.log_sigmoid(c_decay_fwd.astype(F32))[:, :, None, None]
    lg_b = jax.nn.log_sigmoid(c_decay_bwd.astype(F32))[:, :, None, None]
    idx = jnp.arange(CHUNK, dtype=F32)
    diff = idx[:, None] - idx[None, :]
    ones = jnp.ones((CHUNK, CHUNK), F32)
    t_col = idx[:, None] * ones
    intra_f = jnp.where(diff >= 0, jnp.exp(lg_f * jnp.maximum(diff, 0.0)), 0.0)
    intra_b = jnp.where(diff <= 0, jnp.exp(lg_b * jnp.maximum(-diff, 0.0)), 0.0)
    qdec_f = jnp.exp(lg_f * (t_col + 1.0))
    qdec_b = jnp.exp(lg_b * (CHUNK - t_col))
    kdec_f = jnp.exp(lg_f * (CHUNK - 1.0 - t_col))
    kdec_b = jnp.exp(lg_b * t_col)
    cdec_f = jnp.exp(lg_f * CHUNK) * ones
    cdec_b = jnp.exp(lg_b * CHUNK) * ones
    pair = lambda a, b: jnp.stack([a, b], axis=1)
    return (pair(intra_f, intra_b), pair(qdec_f, qdec_b), pair(kdec_f, kdec_b), pair(cdec_f, cdec_b))


def _moe_plan(info, counts_row, n_slots_blocks):
    e = info[:, 0:2].astype(jnp.int32)
    rank = info[:, 4:6].astype(jnp.int32)
    counts = counts_row[:N_EXPERTS].astype(jnp.int32)
    padded = (counts + MOE_BLOCK - 1) // MOE_BLOCK * MOE_BLOCK
    pad_end = jnp.cumsum(padded)
    pad_start = pad_end - padded
    hit = e[:, :, None] == jnp.arange(N_EXPERTS, dtype=jnp.int32)
    dest = jnp.sum(jnp.where(hit, pad_start, 0), axis=-1) + rank
    n_used = pad_end[-1] // MOE_BLOCK
    blocks = jnp.arange(n_slots_blocks, dtype=jnp.int32)
    first_slot = jnp.minimum(blocks, n_used - 1) * MOE_BLOCK
    block_e = jnp.sum((pad_end[None, :] <= first_slot[:, None]).astype(jnp.int32), axis=1)
    ids = jnp.arange(N_EXPERTS, dtype=jnp.int32)
    nonempty = counts > 0
    slot_of = (jnp.cumsum(nonempty.astype(jnp.int32)) - 1) % 2
    later = nonempty[None, :] & (ids[None, :] > ids[:, None])
    next_of = jnp.min(jnp.where(later, ids[None, :], N_EXPERTS), axis=1)
    next_of = jnp.where(next_of == N_EXPERTS, -1, next_of)
    pick = lambda table: jnp.sum(jnp.where(block_e[:, None] == ids[None, :], table[None, :], 0), axis=1)
    prev_e = jnp.concatenate([jnp.full((1,), -1, jnp.int32), block_e[:-1]])
    first = ((block_e != prev_e) & (blocks < n_used)).astype(jnp.int32)
    i32 = lambda a: a.astype(jnp.int32)
    return dict(dest=dest, block_e=i32(block_e), n_used=i32(n_used.reshape(1)), pad_end=i32(pad_end),
                padded=i32(padded), first=first, slot=i32(pick(slot_of)), next_e=i32(pick(next_of)))


def kernel(x, c, ctx, c_ctx, norm_mix, norm_ffn, w_ada, b_ada, w_in, a_q_norm, a_k_norm, a_sink,
           b_norm, b_spatial, b_spatial_bias, c_decay_fwd, c_decay_bwd, c_norm, w_branch, w_out,
           w_router_group, b_router_group, w_router_expert, b_router_expert, w_expert_in, w_expert_out):
    batch, n, d = x.shape
    ctx_rows = ctx.shape[1]
    depth = w_in.shape[0]
    assert batch == 1 and ctx_rows % 256 == 0 and n % 256 == 0
    rows = ctx_rows + n

    cond = jnp.stack([c[0], c_ctx], axis=0)
    cond_b = jnp.broadcast_to(cond[:, :, None], (2, d, LANES))
    mod = _ada_call(cond_b, w_ada, b_ada)[:, :2].reshape(depth, 2, N_MOD, d)
    modsel = jnp.pad(mod[:, ::-1], ((0, 0), (0, 0), (0, MOD_ROWS - N_MOD), (0, 0)))

    cos_t, sin_t = _rope_tables(n, ctx_rows)
    ret_tables = _retention_tables(c_decay_fwd, c_decay_bwd)
    sink_col = jnp.broadcast_to(a_sink.astype(F32).reshape(depth, A_KV_HEADS, A_GROUP, 1, 1),
                                (depth, A_KV_HEADS, A_GROUP, A_BLOCK, 1)).reshape(depth, A_KV_HEADS, A_GROUP * A_BLOCK, 1)
    ws = b_spatial.astype(MXU_DTYPE)
    bias_b = jnp.broadcast_to(b_spatial_bias.astype(F32)[:, :, :, None], (depth, B_GROUPS, CHUNK, LANES))
    wb = w_branch.astype(MXU_DTYPE)
    wo = w_out
    w_r = jnp.concatenate([w_router_group, w_router_expert], axis=-1).astype(F32)
    w_r = jnp.pad(w_r, ((0, 0), (0, 0), (0, LANES - w_r.shape[-1])))
    w_r_hi = w_r.astype(MXU_DTYPE)
    w_r_lo = (w_r - w_r_hi.astype(F32)).astype(MXU_DTYPE)
    b_r = jnp.concatenate([b_router_group, b_router_expert], axis=-1).astype(F32)
    b_r = jnp.pad(b_r, ((0, 0), (0, LANES - b_r.shape[-1]))).reshape(depth, 1, LANES)
    norm_mix3 = norm_mix.reshape(depth, 1, d)
    norm_ffn3 = norm_ffn.reshape(depth, 1, d)
    lane_ids = jnp.arange(HEAD_DIM)
    partner = jnp.where((lane_ids // 32) % 2 == 0, lane_ids + 32, lane_ids - 32)
    perm = (lane_ids[:, None] == partner[None, :]).astype(ACT_DTYPE)
    a_q_norm3 = jnp.stack([a_q_norm, a_q_norm[:, partner]], axis=1).astype(F32)
    a_k_norm3 = jnp.stack([a_k_norm, a_k_norm[:, partner]], axis=1).astype(F32)
    b_norm3 = b_norm.reshape(depth, 1, BRANCH_WIDTH)
    c_norm3 = c_norm.reshape(depth, 1, BRANCH_WIDTH)

    kvw = A_KV_HEADS * HEAD_DIM
    w = BRANCH_WIDTH
    n_assign = rows * 2
    n_slot_blocks = -(-(n_assign + N_EXPERTS * (MOE_BLOCK - 1)) // MOE_BLOCK)
    tok_tile = 256
    disp_tile = _row_tile(rows, (1056, 768, 256))

    z, h = _first_norm_call(ctx[0], x[0], norm_mix3, modsel)
    for l in range(depth):
        g12 = _proj_call(h, w_in, l, 0, 2 * kvw + 2 * w)
        g3 = _proj_call(h, w_in, l, 2 * kvw + 2 * w, 5 * w)
        g4 = _proj_call(h, w_in, l, 2 * kvw + 7 * w, N_BRANCH * d)
        kn, qn, kr, qr = _prep_call(g12, g3, cos_t, sin_t, a_q_norm3, a_k_norm3, perm, l)
        attn = _attn_call(qn, kn, g12, sink_col, l, ctx_rows)
        gm = _gmlp_call(g3, b_norm3, ws, bias_b, l)
        o_f, o_b = _ret_call(qr, kr, g12, ret_tables, l, ctx_rows)
        merged = _merge_call(attn, gm, o_f, o_b, g3, g4, c_norm3, wb, l)
        z, h2, info, counts = _route_call(merged, wo, z, norm_ffn3, modsel, w_r_hi, w_r_lo, b_r, l, ctx_rows)
        plan = _moe_plan(info, counts[0], n_slot_blocks)
        dest3 = plan["dest"].reshape(rows // tok_tile, 1, 2 * tok_tile)
        dest3_disp = plan["dest"].reshape(rows // disp_tile, 1, 2 * disp_tile)
        xs = _dispatch_call(plan["pad_end"], plan["padded"], plan["n_used"], dest3_disp, h2,
                            n_slot_blocks * MOE_BLOCK)
        ys = _expert_call(plan, xs, w_expert_in, w_expert_out, l)
        if l + 1 < depth:
            z, h = _combine_call(dest3, ys, info, z, modsel, l, ctx_rows, next_norm_w=norm_mix3)
        else:
            (z_latent,) = _combine_call(dest3, ys, info, z, modsel, l, ctx_rows)
    return z_latent[None]
```

```python
import functools
import math

import jax
import jax.numpy as jnp
from jax import lax
from jax.experimental import pallas as pl
from jax.experimental.pallas import tpu as pltpu

F32 = jnp.float32
MXU_DTYPE = jnp.bfloat16
ACT_DTYPE = jnp.bfloat16

LANES = 128
HEAD_DIM = 128
GRID_W = 64
ROPE_BASE = 10000.0
EPS = 1e-6
NEG_INF = -1e30
A_Q_HEADS = 8
A_KV_HEADS = 2
A_GROUP = A_Q_HEADS // A_KV_HEADS
A_BLOCK = 128
B_GROUPS = 8
C_HEADS = 8
CHUNK = 128
N_GROUPS = 4
EXPERTS_PER_GROUP = 8
N_EXPERTS = N_GROUPS * EXPERTS_PER_GROUP
D_EXPERT = 512
BRANCH_WIDTH = 1024
N_BRANCH = 3
MOE_BLOCK = 320
N_MOD = 6
MOD_ROWS = 8
MIB = 1024 * 1024


def _params(vmem_mib, n_grid, **kw):
    return pltpu.CompilerParams(dimension_semantics=("arbitrary",) * n_grid,
                                vmem_limit_bytes=vmem_mib * MIB, **kw)


def _sigmoid(x):
    return 0.5 + 0.5 * jnp.tanh(0.5 * x)


def _silu(x):
    return x * _sigmoid(x)


PACK_DTYPE = jnp.uint32


def _pack_halves(x):
    n = x.shape[1] // 2
    rounded = lambda v: lax.bitcast_convert_type(v.astype(jnp.bfloat16).astype(F32), PACK_DTYPE)
    return rounded(x[:, n:]) | (rounded(x[:, :n]) >> 16)


def _unpack_halves(p):
    lo = lax.bitcast_convert_type(p << 16, F32)
    hi = lax.bitcast_convert_type(p & jnp.asarray(0xFFFF0000, PACK_DTYPE), F32)
    return lo, hi


def _gelu_tanh(x):
    return 0.5 * x * (1.0 + jnp.tanh(math.sqrt(2.0 / math.pi) * (x + 0.044715 * (x * x * x))))


def _ada_kernel(c_ref, w_ref, b_ref, o_ref):
    tn = w_ref.shape[1]
    s0 = _silu(c_ref[0])
    s1 = _silu(c_ref[1])
    o_ref[...] = jnp.zeros(o_ref.shape, o_ref.dtype)
    for j in range(tn // LANES):
        sl = slice(j * LANES, (j + 1) * LANES)
        wj = w_ref[:, sl]
        o_ref[0:1, sl] = jnp.sum(wj * s0, axis=0, keepdims=True) + b_ref[:, sl]
        o_ref[1:2, sl] = jnp.sum(wj * s1, axis=0, keepdims=True) + b_ref[:, sl]


def _ada_call(cond_b, w_ada, b_ada):
    depth, k, n = w_ada.shape
    tn = 2048
    return pl.pallas_call(
        _ada_kernel,
        grid=(depth, n // tn),
        in_specs=[pl.BlockSpec((2, k, LANES), lambda l, j: (0, 0, 0)),
                  pl.BlockSpec((None, k, tn), lambda l, j: (l, 0, j)),
                  pl.BlockSpec((None, 1, tn), lambda l, j: (l, 0, j))],
        out_specs=pl.BlockSpec((None, 8, tn), lambda l, j: (l, 0, j)),
        out_shape=jax.ShapeDtypeStruct((depth, 8, n), F32),
        compiler_params=_params(48, 2),
        name="adaln",
    )(cond_b, w_ada, b_ada.reshape(depth, 1, n))


def _norm_mod(z, g, mod, shift_row, scale_row):
    r = lax.rsqrt(jnp.mean(z * z, axis=-1, keepdims=True) + EPS)
    return (z * r * g) * (1.0 + mod[scale_row:scale_row + 1, :]) + mod[shift_row:shift_row + 1, :]


def _first_norm_kernel(ctx_ref, x_ref, g_ref, mod_ref, z_ref, h_ref, *, ctx_tiles):
    def emit(src_ref):
        z = src_ref[...]
        z_ref[...] = z
        h_ref[...] = _norm_mod(z, g_ref[...], mod_ref[...], 0, 1).astype(h_ref.dtype)

    @pl.when(pl.program_id(0) < ctx_tiles)
    def _():
        emit(ctx_ref)

    @pl.when(pl.program_id(0) >= ctx_tiles)
    def _():
        emit(x_ref)


def _first_norm_call(ctx2, x2, norm_w, modsel):
    ctx_rows, d = ctx2.shape
    rows = ctx_rows + x2.shape[0]
    tm = 256
    ctx_tiles = ctx_rows // tm
    row_spec = pl.BlockSpec((tm, d), lambda i: (i, 0))
    return pl.pallas_call(
        functools.partial(_first_norm_kernel, ctx_tiles=ctx_tiles),
        grid=(rows // tm,),
        in_specs=[pl.BlockSpec((tm, d), lambda i: (jnp.minimum(i, ctx_tiles - 1), 0)),
                  pl.BlockSpec((tm, d), lambda i: (jnp.maximum(i - ctx_tiles, 0), 0)),
                  pl.BlockSpec((None, 1, d), lambda i: (0, 0, 0)),
                  pl.BlockSpec((None, None, MOD_ROWS, d),
                               lambda i: (0, jnp.where(i >= ctx_tiles, 1, 0), 0, 0))],
        out_specs=[row_spec, row_spec],
        out_shape=[jax.ShapeDtypeStruct((rows, d), F32), jax.ShapeDtypeStruct((rows, d), ACT_DTYPE)],
        compiler_params=_params(32, 1),
        name="stack_norm_mod",
    )(ctx2, x2, norm_w, modsel)


def _proj_kernel(h_ref, w_ref, o_ref, wbf_ref):
    @pl.when(pl.program_id(1) == 0)
    def _():
        wbf_ref[...] = w_ref[...].astype(wbf_ref.dtype)

    o_ref[...] = jnp.dot(h_ref[...], wbf_ref[...], preferred_element_type=F32).astype(o_ref.dtype)


def _row_tile(rows, pref):
    for t in pref:
        if rows % t == 0:
            return t
    raise ValueError(f"no row tile for {rows}")


def _proj_call(h, w, layer, col_off, ncols):
    rows, k = h.shape
    tn = next(t for t in (1536, 1280, 1024, 512) if col_off % t == 0 and ncols % t == 0)
    vmem_mib = 52
    need = lambda t: (2 * 4 + 2) * k * tn + 2 * 2 * t * k + (2 * 2 + 4) * t * tn
    tm = next(t for t in (1056, 768, 512, 256) if rows % t == 0 and need(t) <= (vmem_mib - 4) * MIB)
    off = col_off // tn
    return pl.pallas_call(
        _proj_kernel,
        grid=(ncols // tn, rows // tm),
        in_specs=[pl.BlockSpec((tm, k), lambda j, i: (i, 0)),
                  pl.BlockSpec((None, k, tn), lambda j, i: (layer, 0, off + j))],
        out_specs=pl.BlockSpec((tm, tn), lambda j, i: (i, j)),
        out_shape=jax.ShapeDtypeStruct((rows, ncols), ACT_DTYPE),
        scratch_shapes=[pltpu.VMEM((k, tn), MXU_DTYPE)],
        compiler_params=_params(vmem_mib, 2),
        name="proj_in",
    )(h, w)


def _prep_kernel(ak_ref, ck0_ref, ck1_ref, aq_ref, cq_ref, cos_ref, sin_ref, qn_ref, kn_ref, perm_ref,
                 okn_ref, oqn_ref, okr_ref, oqr_ref):
    cos = cos_ref[...]
    sin = sin_ref[...]
    perm = perm_ref[...]
    scale = HEAD_DIM ** -0.5

    def swap(x):
        return jnp.dot(x, perm, preferred_element_type=F32)

    def norm_rope(x, g_ref, out_scale):
        xf = x.astype(F32)
        r = lax.rsqrt(jnp.mean(xf * xf, axis=-1, keepdims=True) + EPS) * out_scale
        return (xf * (g_ref[0:1, :] * cos) + swap(x) * (g_ref[1:2, :] * sin)) * r

    for h in range(A_KV_HEADS):
        sl = slice(h * HEAD_DIM, (h + 1) * HEAD_DIM)
        okn_ref[:, sl] = norm_rope(ak_ref[:, sl], kn_ref, 1.0).astype(okn_ref.dtype)
    for h in range(A_Q_HEADS):
        sl = slice(h * HEAD_DIM, (h + 1) * HEAD_DIM)
        oqn_ref[:, sl] = norm_rope(aq_ref[:, sl], qn_ref, scale).astype(oqn_ref.dtype)
    for h in range(C_HEADS):
        sl = slice(h * HEAD_DIM, (h + 1) * HEAD_DIM)
        half = C_HEADS // 2
        k = (ck0_ref if h < half else ck1_ref)[:, (h % half) * HEAD_DIM:(h % half + 1) * HEAD_DIM]
        q = cq_ref[:, sl]
        okr_ref[:, sl] = ((k.astype(F32) * cos + swap(k) * sin) * scale).astype(okr_ref.dtype)
        oqr_ref[:, sl] = (q.astype(F32) * cos + swap(q) * sin).astype(oqr_ref.dtype)


def _prep_call(g12, g3, cos_t, sin_t, a_q_norm, a_k_norm, perm, layer):
    rows = g12.shape[0]
    tm = _row_tile(rows, (768, 512, 256))
    w = BRANCH_WIDTH
    kvw = A_KV_HEADS * HEAD_DIM
    return pl.pallas_call(
        _prep_kernel,
        grid=(rows // tm,),
        in_specs=[pl.BlockSpec((tm, kvw), lambda i: (i, 0)),
                  pl.BlockSpec((tm, w // 2), lambda i: (i, 1)),
                  pl.BlockSpec((tm, w // 2), lambda i: (i, 2)),
                  pl.BlockSpec((tm, w), lambda i: (i, 0)),
                  pl.BlockSpec((tm, w), lambda i: (i, 1)),
                  pl.BlockSpec((tm, LANES), lambda i: (i, 0)),
                  pl.BlockSpec((tm, LANES), lambda i: (i, 0)),
                  pl.BlockSpec((None, 2, HEAD_DIM), lambda i: (layer, 0, 0)),
                  pl.BlockSpec((None, 2, HEAD_DIM), lambda i: (layer, 0, 0)),
                  pl.BlockSpec((HEAD_DIM, HEAD_DIM), lambda i: (0, 0))],
        out_specs=[pl.BlockSpec((tm, kvw), lambda i: (i, 0)),
                   pl.BlockSpec((tm, w), lambda i: (i, 0)),
                   pl.BlockSpec((tm, w), lambda i: (i, 0)),
                   pl.BlockSpec((tm, w), lambda i: (i, 0))],
        out_shape=[jax.ShapeDtypeStruct((rows, kvw), ACT_DTYPE),
                   jax.ShapeDtypeStruct((rows, w), ACT_DTYPE),
                   jax.ShapeDtypeStruct((rows, w), ACT_DTYPE),
                   jax.ShapeDtypeStruct((rows, w), ACT_DTYPE)],
        compiler_params=_params(40, 1),
        name="qk_prep",
    )(g12, g12, g12, g3, g3, cos_t, sin_t, a_q_norm, a_k_norm, perm)


def _attn_kernel(q_ref, kp_ref, km_ref, kn_ref, kc_ref, vp_ref, vm_ref, vn_ref, vc_ref, sink_ref,
                 o_ref, *, ctx_blocks, n_blocks):
    blk = A_BLOCK
    n_keys = 3 * blk + kc_ref.shape[0]
    row = lax.broadcasted_iota(jnp.int32, (A_GROUP * blk, n_keys), 0) % blk
    col = lax.broadcasted_iota(jnp.int32, (A_GROUP * blk, n_keys), 1)
    in_window = (row >= jnp.maximum(col - 2 * blk, 0)) & (row <= jnp.where(col < blk, col, blk - 1))
    is_ctx_key = col >= 3 * blk

    def valid_mask(rb):
        is_lat = rb >= ctx_blocks
        c_lo = jnp.where(rb >= ctx_blocks + 1, 0, blk)
        c_hi = jnp.where(rb <= n_blocks - 2, 3 * blk, 2 * blk)
        c_lo = jnp.where(is_lat, c_lo, 3 * blk)
        c_hi = jnp.where(is_lat, c_hi, 0)
        return ((col >= c_lo) & (col < c_hi) & in_window) | is_ctx_key

    hs = lambda hk: slice(hk * HEAD_DIM, (hk + 1) * HEAD_DIM)
    qs = lambda hk, g: slice((hk * A_GROUP + g) * HEAD_DIM, (hk * A_GROUP + g + 1) * HEAD_DIM)
    lo_rows, hi_rows = slice(0, blk), slice(blk, 2 * blk)

    def band(j, prev_ref, mid_ref, next_ref, ctx_ref, hk):
        parts = ((prev_ref[:, hs(hk)], mid_ref[lo_rows, hs(hk)], mid_ref[hi_rows, hs(hk)]) if j == 0 else
                 (mid_ref[lo_rows, hs(hk)], mid_ref[hi_rows, hs(hk)], next_ref[:, hs(hk)]))
        return jnp.concatenate(parts + (ctx_ref[:, hs(hk)],), axis=0)

    chains = [(j, hk) for j in range(2) for hk in range(A_KV_HEADS)]
    masks = [valid_mask(2 * pl.program_id(0) + j) for j in range(2)]
    scores = []
    for j, hk in chains:
        q = jnp.concatenate([q_ref[j * blk:(j + 1) * blk, qs(hk, g)] for g in range(A_GROUP)], axis=0)
        k = band(j, kp_ref, km_ref, kn_ref, kc_ref, hk)
        scores.append(lax.dot_general(q, k, (((1,), (1,)), ((), ())), preferred_element_type=F32))
    probs, denoms = [], []
    for (j, hk), s in zip(chains, scores):
        s = jnp.where(masks[j], s, NEG_INF)
        sink = sink_ref[hk]
        m = jnp.maximum(jnp.max(s, axis=-1, keepdims=True), sink)
        p = jnp.exp(s - m)
        denoms.append(jnp.sum(p, axis=-1, keepdims=True) + jnp.exp(sink - m))
        probs.append(p.astype(vm_ref.dtype))
    for (j, hk), p, den in zip(chains, probs, denoms):
        v = band(j, vp_ref, vm_ref, vn_ref, vc_ref, hk)
        o = jnp.dot(p, v, preferred_element_type=F32) / den
        for g in range(A_GROUP):
            o_ref[j * blk:(j + 1) * blk, qs(hk, g)] = o[g * blk:(g + 1) * blk].astype(o_ref.dtype)


def _attn_call(qn, kn, g12, sink_col, layer, ctx_rows):
    rows = qn.shape[0]
    blk = A_BLOCK
    nb = rows // blk
    cb = ctx_rows // blk
    assert nb % 2 == 0 and cb % 2 == 0
    qw = A_Q_HEADS * HEAD_DIM
    kvw = A_KV_HEADS * HEAD_DIM
    before = lambda i: jnp.maximum(2 * i - 1, 0)
    after = lambda i: jnp.minimum(2 * i + 2, nb - 1)
    edge = lambda f, c: pl.BlockSpec((blk, kvw), lambda i: (f(i), c))
    pair = lambda c: pl.BlockSpec((2 * blk, kvw), lambda i: (i, c))
    return pl.pallas_call(
        functools.partial(_attn_kernel, ctx_blocks=cb, n_blocks=nb),
        grid=(nb // 2,),
        in_specs=[pl.BlockSpec((2 * blk, qw), lambda i: (i, 0)),
                  edge(before, 0), pair(0), edge(after, 0),
                  pl.BlockSpec((ctx_rows, kvw), lambda i: (0, 0)),
                  edge(before, 1), pair(1), edge(after, 1),
                  pl.BlockSpec((ctx_rows, kvw), lambda i: (0, 1)),
                  pl.BlockSpec((None, A_KV_HEADS, A_GROUP * blk, 1), lambda i: (layer, 0, 0, 0))],
        out_specs=pl.BlockSpec((2 * blk, qw), lambda i: (i, 0)),
        out_shape=jax.ShapeDtypeStruct((rows, qw), ACT_DTYPE),
        compiler_params=_params(40, 1),
        name="window_attn",
    )(qn, kn, kn, kn, kn, g12, g12, g12, g12, sink_col)


GMLP_CHUNKS_PER_STEP = 2


def _gmlp_kernel(u_ref, v_ref, bn_ref, ws_ref, bias_ref, o_ref):
    chunks = range(u_ref.shape[0] // CHUNK)
    rows = lambda c: slice(c * CHUNK, (c + 1) * CHUNK)
    for g in range(B_GROUPS):
        sl = slice(g * LANES, (g + 1) * LANES)
        normed = []
        for c in chunks:
            v = _gelu_tanh(v_ref[rows(c), sl].astype(F32))
            vc = v - jnp.mean(v, axis=-1, keepdims=True)
            vh = vc * lax.rsqrt(jnp.mean(vc * vc, axis=-1, keepdims=True) + EPS) * bn_ref[:, sl]
            normed.append(vh.astype(ws_ref.dtype))
        mixed = jnp.dot(ws_ref[g], jnp.concatenate(normed, axis=1), preferred_element_type=F32)
        for c in chunks:
            m = mixed[:, c * LANES:(c + 1) * LANES] + bias_ref[g]
            o_ref[rows(c), sl] = (_gelu_tanh(u_ref[rows(c), sl].astype(F32)) * m).astype(o_ref.dtype)


def _gmlp_call(g3, b_norm_flat, ws, bias_b, layer):
    rows = g3.shape[0]
    w = BRANCH_WIDTH
    tm = GMLP_CHUNKS_PER_STEP * CHUNK
    return pl.pallas_call(
        _gmlp_kernel,
        grid=(rows // tm,),
        in_specs=[pl.BlockSpec((tm, w), lambda i: (i, 3)),
                  pl.BlockSpec((tm, w), lambda i: (i, 4)),
                  pl.BlockSpec((None, 1, w), lambda i: (layer, 0, 0)),
                  pl.BlockSpec((None, B_GROUPS, CHUNK, CHUNK), lambda i: (layer, 0, 0, 0)),
                  pl.BlockSpec((None, B_GROUPS, CHUNK, LANES), lambda i: (layer, 0, 0, 0))],
        out_specs=pl.BlockSpec((tm, w), lambda i: (i, 0)),
        out_shape=jax.ShapeDtypeStruct((rows, w), ACT_DTYPE),
        compiler_params=_params(32, 1),
        name="chunk_gmlp",
    )(g3, g3, b_norm_flat, ws, bias_b)


def _ret_kernel(qf_ref, kf_ref, vf0_ref, vf1_ref, qb_ref, kb_ref, vb0_ref, vb1_ref,
                intra_ref, qdec_ref, kdec_ref, cdec_ref,
                of_ref, ob_ref, *state_refs):
    @pl.when(pl.program_id(0) == 0)
    def _():
        for s_ref in state_refs:
            s_ref[...] = jnp.zeros(s_ref.shape, s_ref.dtype)

    dirs = ((qf_ref, kf_ref, of_ref), (qb_ref, kb_ref, ob_ref))
    v_halves = ((vf0_ref, vf1_ref), (vb0_ref, vb1_ref))
    lower, upper = slice(0, CHUNK), slice(CHUNK, 2 * CHUNK)
    scan_rows = ((lower, upper), (upper, lower))
    chains = [(d, h) for d in range(2) for h in range(C_HEADS)]
    head = lambda h: slice(h * HEAD_DIM, (h + 1) * HEAD_DIM)
    half = C_HEADS // 2
    value = lambda d, c, h: v_halves[d][h // half][scan_rows[d][c], (h % half) * HEAD_DIM:(h % half + 1) * HEAD_DIM]

    scores = {}
    for c in range(2):
        for d, h in chains:
            q_ref, k_ref, _ = dirs[d]
            q = q_ref[scan_rows[d][c], head(h)]
            k = k_ref[scan_rows[d][c], head(h)]
            a = lax.dot_general(q, k, (((1,), (1,)), ((), ())), preferred_element_type=F32)
            qd = (q.astype(F32) * qdec_ref[d, h]).astype(q.dtype)
            scores[c, d, h] = jnp.concatenate([(a * intra_ref[d, h]).astype(q.dtype), qd], axis=1)
    for c in range(2):
        for d, h in chains:
            o_ref = dirs[d][2]
            v = value(d, c, h)
            rhs = jnp.concatenate([v, state_refs[d * C_HEADS + h][...].astype(v.dtype)], axis=0)
            o_ref[scan_rows[d][c], head(h)] = jnp.dot(scores[c, d, h], rhs,
                                                      preferred_element_type=F32).astype(o_ref.dtype)
        for d, h in chains:
            k_ref = dirs[d][1]
            s_ref = state_refs[d * C_HEADS + h]
            k = k_ref[scan_rows[d][c], head(h)]
            kd = (k.astype(F32) * kdec_ref[d, h]).astype(k.dtype)
            upd = lax.dot_general(kd, value(d, c, h), (((0,), (0,)), ((), ())), preferred_element_type=F32)
            s_ref[...] = s_ref[...] * cdec_ref[d, h] + upd


def _ret_call(qr, kr, g12, tables, layer, ctx_rows):
    rows = qr.shape[0]
    w = BRANCH_WIDTH
    nc = rows // CHUNK
    cc = ctx_rows // CHUNK

    assert nc % 2 == 0 and cc % 2 == 0
    n_pairs, ctx_pairs = nc // 2, cc // 2
    pair_rows = 2 * CHUNK

    def bwd(s):
        return jnp.where(s < ctx_pairs, ctx_pairs - 1 - s, n_pairs - 1 + ctx_pairs - s)

    fwd = lambda s: s
    qk = lambda f: pl.BlockSpec((pair_rows, w), lambda s: (f(s), 0))
    vh = lambda f, c: pl.BlockSpec((pair_rows, w // 2), lambda s: (f(s), c))
    tab = pl.BlockSpec((None, 2, C_HEADS, CHUNK, LANES), lambda s: (layer, 0, 0, 0, 0))
    return pl.pallas_call(
        _ret_kernel,
        grid=(n_pairs,),
        in_specs=[qk(fwd), qk(fwd), vh(fwd, 3), vh(fwd, 4), qk(bwd), qk(bwd), vh(bwd, 3), vh(bwd, 4),
                  tab, tab, tab, tab],
        out_specs=[pl.BlockSpec((pair_rows, w), lambda s: (s, 0)),
                   pl.BlockSpec((pair_rows, w), lambda s: (bwd(s), 0))],
        out_shape=[jax.ShapeDtypeStruct((rows, w), ACT_DTYPE), jax.ShapeDtypeStruct((rows, w), ACT_DTYPE)],
        scratch_shapes=[pltpu.VMEM((HEAD_DIM, HEAD_DIM), F32) for _ in range(2 * C_HEADS)],
        compiler_params=_params(32, 1),
        name="retention",
    )(qr, kr, g12, g12, qr, kr, g12, g12, *tables)


def _merge_kernel(attn_ref, gm_ref, of_ref, ob_ref, rg_ref, gate_ref, cn_ref, wb_ref, o_ref):
    d = o_ref.shape[1]
    ret_parts = []
    for h in range(C_HEADS):
        sl = slice(h * HEAD_DIM, (h + 1) * HEAD_DIM)
        o = of_ref[:, sl].astype(F32) + ob_ref[:, sl].astype(F32)
        oc = o - jnp.mean(o, axis=-1, keepdims=True)
        y = oc * lax.rsqrt(jnp.mean(oc * oc, axis=-1, keepdims=True) + EPS) * cn_ref[:, sl]
        ret_parts.append((_silu(rg_ref[:, sl].astype(F32)) * y).astype(wb_ref.dtype))
    ret = jnp.concatenate(ret_parts, axis=1)
    branches = (attn_ref[...].astype(wb_ref.dtype), gm_ref[...].astype(wb_ref.dtype), ret)
    acc = None
    for b in range(N_BRANCH):
        proj = jnp.dot(branches[b], wb_ref[b], preferred_element_type=F32)
        term = _sigmoid(gate_ref[:, b * d:(b + 1) * d].astype(F32)) * proj
        acc = term if acc is None else acc + term
    o_ref[...] = acc.astype(o_ref.dtype)


def _merge_call(attn, gm, o_f, o_b, g3, g4, c_norm_flat, wb, layer):
    rows = attn.shape[0]
    w = BRANCH_WIDTH
    d = wb.shape[-1]
    tm = 256
    row = lambda c: pl.BlockSpec((tm, w), lambda i: (i, c))
    return pl.pallas_call(
        _merge_kernel,
        grid=(rows // tm,),
        in_specs=[row(0), row(0), row(0), row(0), row(2),
                  pl.BlockSpec((tm, N_BRANCH * d), lambda i: (i, 0)),
                  pl.BlockSpec((None, 1, w), lambda i: (layer, 0, 0)),
                  pl.BlockSpec((None, N_BRANCH, w, d), lambda i: (layer, 0, 0, 0))],
        out_specs=pl.BlockSpec((tm, d), lambda i: (i, 0)),
        out_shape=jax.ShapeDtypeStruct((rows, d), ACT_DTYPE),
        compiler_params=_params(52, 1),
        name="branch_merge",
    )(attn, gm, o_f, o_b, g3, g4, c_norm_flat, wb)


def _route_kernel(m_ref, wo_ref, z_ref, g_ref, mod_ref, whi_ref, wlo_ref, rb_ref,
                  znew_ref, h_ref, info_ref, cnt_ref, carry_ref, logits_ref, wobf_ref):
    i = pl.program_id(0)

    @pl.when(i == 0)
    def _():
        carry_ref[...] = jnp.zeros(carry_ref.shape, carry_ref.dtype)
        logits_ref[...] = jnp.zeros(logits_ref.shape, logits_ref.dtype)
        wobf_ref[...] = wo_ref[...].astype(wobf_ref.dtype)

    prev_logits = logits_ref[(i + 1) % 2]
    y = jnp.dot(m_ref[...], wobf_ref[...], preferred_element_type=F32)
    _route_stage(prev_logits, jnp.where(i >= 1, 1.0, 0.0), info_ref, cnt_ref, carry_ref)
    z = z_ref[...] + mod_ref[2:3, :] * y
    znew_ref[...] = z
    h = _norm_mod(z, g_ref[...], mod_ref[...], 3, 4)
    h_ref[...] = _pack_halves(h)
    h_hi = h.astype(whi_ref.dtype)
    h_lo = (h - h_hi.astype(F32)).astype(whi_ref.dtype)
    logits_ref[i % 2] = (jnp.dot(h_hi, whi_ref[...], preferred_element_type=F32)
                         + jnp.dot(h_hi, wlo_ref[...], preferred_element_type=F32)
                         + jnp.dot(h_lo, whi_ref[...], preferred_element_type=F32)) + rb_ref[...]


def _route_stage(logits, count_gate, info_ref, cnt_ref, carry_ref):
    tm = logits.shape[0]
    lane = lax.broadcasted_iota(jnp.int32, logits.shape, 1).astype(F32)
    first = lambda hit: jnp.min(jnp.where(hit, lane, 4.0 * LANES), axis=-1, keepdims=True)

    is_g = lane < N_GROUPS
    gl = jnp.where(is_g, logits, NEG_INF)
    gmax = jnp.max(gl, axis=-1, keepdims=True)
    g_sel = first(gl == gmax)
    g_w = 1.0 / jnp.sum(jnp.where(is_g, jnp.exp(gl - gmax), 0.0), axis=-1, keepdims=True)

    e_id = lane - N_GROUPS
    in_group = (e_id >= g_sel * EXPERTS_PER_GROUP) & (e_id < (g_sel + 1.0) * EXPERTS_PER_GROUP)
    el = jnp.where(in_group, logits, NEG_INF)
    m1 = jnp.max(el, axis=-1, keepdims=True)
    i1 = first(el == m1)
    el2 = jnp.where(lane == i1, NEG_INF, el)
    m2 = jnp.max(el2, axis=-1, keepdims=True)
    i2 = first(el2 == m2)
    r = jnp.exp(m2 - m1)
    w1 = g_w / (1.0 + r)
    w2 = g_w * r / (1.0 + r)
    e1 = i1 - N_GROUPS
    e2 = i2 - N_GROUPS

    hot1 = lane == e1
    hot2 = lane == e2
    hot = jnp.where(hot1 | hot2, 1.0, 0.0)
    rr = lax.broadcasted_iota(jnp.int32, (tm, tm), 0)
    cc = lax.broadcasted_iota(jnp.int32, (tm, tm), 1)
    tri = jnp.where(cc < rr, 1.0, 0.0).astype(MXU_DTYPE)
    before = jnp.dot(tri, hot.astype(MXU_DTYPE), preferred_element_type=F32) + carry_ref[0:1, :]
    rank1 = jnp.sum(jnp.where(hot1, before, 0.0), axis=-1, keepdims=True)
    rank2 = jnp.sum(jnp.where(hot2, before, 0.0), axis=-1, keepdims=True)
    carry_ref[0:1, :] = carry_ref[0:1, :] + count_gate * jnp.sum(hot, axis=0, keepdims=True)
    cnt_ref[...] = carry_ref[...]

    info = jnp.where(lane == 0, e1, 0.0)
    info = jnp.where(lane == 1, e2, info)
    info = jnp.where(lane == 2, w1, info)
    info = jnp.where(lane == 3, w2, info)
    info = jnp.where(lane == 4, rank1, info)
    info = jnp.where(lane == 5, rank2, info)
    info_ref[...] = info


def _route_call(merged, wo, z, norm_w, modsel, w_hi, w_lo, rbias, layer, ctx_rows):
    rows, d = z.shape
    tm = 256
    ctx_tiles = ctx_rows // tm
    n_tiles = rows // tm
    cur = lambda i: jnp.minimum(i, n_tiles - 1)
    prev = lambda i: jnp.maximum(i - 1, 0)
    return pl.pallas_call(
        _route_kernel,
        grid=(n_tiles + 1,),
        in_specs=[pl.BlockSpec((tm, d), lambda i: (cur(i), 0)),
                  pl.BlockSpec((None, d, d), lambda i: (layer, 0, 0), pipeline_mode=pl.Buffered(1)),
                  pl.BlockSpec((tm, d), lambda i: (cur(i), 0)),
                  pl.BlockSpec((None, 1, d), lambda i: (layer, 0, 0)),
                  pl.BlockSpec((None, None, MOD_ROWS, d),
                               lambda i: (layer, jnp.where(cur(i) >= ctx_tiles, 1, 0), 0, 0)),
                  pl.BlockSpec((None, d, LANES), lambda i: (layer, 0, 0)),
                  pl.BlockSpec((None, d, LANES), lambda i: (layer, 0, 0)),
                  pl.BlockSpec((None, 1, LANES), lambda i: (layer, 0, 0))],
        out_specs=[pl.BlockSpec((tm, d), lambda i: (cur(i), 0)),
                   pl.BlockSpec((tm, d // 2), lambda i: (cur(i), 0)),
                   pl.BlockSpec((tm, LANES), lambda i: (prev(i), 0)),
                   pl.BlockSpec((8, LANES), lambda i: (0, 0))],
        out_shape=[jax.ShapeDtypeStruct((rows, d), F32),
                   jax.ShapeDtypeStruct((rows, d // 2), PACK_DTYPE),
                   jax.ShapeDtypeStruct((rows, LANES), F32),
                   jax.ShapeDtypeStruct((8, LANES), F32)],
        scratch_shapes=[pltpu.VMEM((8, LANES), F32), pltpu.VMEM((2, tm, LANES), F32),
                        pltpu.VMEM((d, d), MXU_DTYPE)],
        compiler_params=_params(54, 1),
        name="outproj_route",
    )(merged, wo, z, norm_w, modsel, w_hi, w_lo, rbias)


def _dispatch_kernel(pad_end_ref, padded_ref, n_used_ref, dest_ref, h_ref, xs_ref, zero_ref, sem, zero_sem):
    tm = h_ref.shape[0]
    n_blocks = xs_ref.shape[0] // MOE_BLOCK

    @pl.when(pl.program_id(0) == 0)
    def _():
        zero_ref[...] = jnp.zeros(zero_ref.shape, zero_ref.dtype)

        def block_copy(first):
            first = pl.multiple_of(first, MOE_BLOCK)
            return pltpu.make_async_copy(zero_ref, xs_ref.at[pl.ds(first, MOE_BLOCK)], zero_sem)

        def start_unused(b, carry):
            block_copy(b * MOE_BLOCK).start()
            return carry

        def wait_unused(b, carry):
            block_copy(b * MOE_BLOCK).wait()
            return carry

        for e in range(N_EXPERTS):
            @pl.when(padded_ref[e] > 0)
            def _(e=e):
                block_copy(pad_end_ref[e] - MOE_BLOCK).start()
        lax.fori_loop(n_used_ref[0], n_blocks, start_unused, 0)
        for e in range(N_EXPERTS):
            @pl.when(padded_ref[e] > 0)
            def _(e=e):
                block_copy(pad_end_ref[e] - MOE_BLOCK).wait()
        lax.fori_loop(n_used_ref[0], n_blocks, wait_unused, 0)

    def row_copy(t, slot):
        return pltpu.make_async_copy(h_ref.at[pl.ds(t, 1)], xs_ref.at[pl.ds(slot, 1)], sem)

    def start(t, carry):
        row_copy(t, dest_ref[0, 0, 2 * t]).start(priority=0)
        row_copy(t, dest_ref[0, 0, 2 * t + 1]).start(priority=1)
        return carry

    lax.fori_loop(0, tm, start, 0, unroll=16)
    all_rows = pltpu.make_async_copy(h_ref, xs_ref.at[pl.ds(0, tm)], sem)
    all_rows.wait()
    all_rows.wait()


def _dispatch_call(pad_end, padded, n_used, dest3, h, n_slots):
    rows, d = h.shape
    tm = dest3.shape[2] // 2
    grid_spec = pltpu.PrefetchScalarGridSpec(
        num_scalar_prefetch=3,
        grid=(rows // tm,),
        in_specs=[pl.BlockSpec((1, 1, 2 * tm), lambda i, pe, pd, nu: (i, 0, 0), memory_space=pltpu.SMEM),
                  pl.BlockSpec((tm, d), lambda i, pe, pd, nu: (i, 0))],
        out_specs=pl.BlockSpec(memory_space=pl.ANY),
        scratch_shapes=[pltpu.VMEM((MOE_BLOCK, d), h.dtype),
                        pltpu.SemaphoreType.DMA(()), pltpu.SemaphoreType.DMA(())])
    return pl.pallas_call(
        _dispatch_kernel,
        grid_spec=grid_spec,
        out_shape=jax.ShapeDtypeStruct((n_slots, d), h.dtype),
        compiler_params=_params(32, 1),
        name="moe_dispatch",
    )(pad_end, padded, n_used, dest3, h)


def _expert_kernel(be_ref, nu_ref, first_ref, slot_ref, next_ref, xs_ref, w1_hbm, w2_hbm, ys_ref,
                   w1f_ref, w2f_ref, w1bf_ref, w2bf_ref, sem, *, layer):
    i = pl.program_id(0)
    used = i < nu_ref[0]

    def fetch(e, s):
        return (pltpu.make_async_copy(w1_hbm.at[layer, e], w1f_ref.at[s], sem.at[s]),
                pltpu.make_async_copy(w2_hbm.at[layer, e], w2f_ref.at[s], sem.at[s]))

    @pl.when(i == 0)
    def _():
        for cp in fetch(be_ref[0], slot_ref[0]):
            cp.start()

    @pl.when(jnp.logical_and(used, first_ref[i] == 1))
    def _():
        s = slot_ref[i]
        for cp in fetch(be_ref[i], s):
            cp.wait()

        @pl.when(next_ref[i] >= 0)
        def _():
            for cp in fetch(next_ref[i], 1 - s):
                cp.start()

        w1bf_ref[...] = w1f_ref[s].astype(w1bf_ref.dtype)
        w2bf_ref[...] = w2f_ref[s].astype(w2bf_ref.dtype)

    @pl.when(used)
    def _():
        de = w2bf_ref.shape[0]
        x = jnp.concatenate(_unpack_halves(xs_ref[...]), axis=1).astype(w1bf_ref.dtype)
        hcat = jnp.dot(x, w1bf_ref[...], preferred_element_type=F32)
        act = _silu(hcat[:, :de]) * hcat[:, de:]
        ys_ref[...] = _pack_halves(jnp.dot(act.astype(w2bf_ref.dtype), w2bf_ref[...], preferred_element_type=F32))

    @pl.when(jnp.logical_not(used))
    def _():
        ys_ref[...] = jnp.zeros(ys_ref.shape, ys_ref.dtype)


def _expert_call(plan, xs, w_e1, w_e2, layer):
    slots, dp = xs.shape
    d = 2 * dp
    de = w_e2.shape[2]
    nb = slots // MOE_BLOCK
    blk = lambda i, be, nu, *_: (jnp.minimum(i, nu[0] - 1), 0)
    grid_spec = pltpu.PrefetchScalarGridSpec(
        num_scalar_prefetch=5,
        grid=(nb,),
        in_specs=[pl.BlockSpec((MOE_BLOCK, dp), blk),
                  pl.BlockSpec(memory_space=pl.ANY),
                  pl.BlockSpec(memory_space=pl.ANY)],
        out_specs=pl.BlockSpec((MOE_BLOCK, dp), lambda i, *_: (i, 0)),
        scratch_shapes=[pltpu.VMEM((2, d, 2 * de), w_e1.dtype), pltpu.VMEM((2, de, d), w_e2.dtype),
                        pltpu.VMEM((d, 2 * de), MXU_DTYPE), pltpu.VMEM((de, d), MXU_DTYPE),
                        pltpu.SemaphoreType.DMA((2,))])
    return pl.pallas_call(
        functools.partial(_expert_kernel, layer=layer),
        grid_spec=grid_spec,
        out_shape=jax.ShapeDtypeStruct((slots, dp), PACK_DTYPE),
        compiler_params=_params(48, 1),
        name="moe_experts",
    )(plan["block_e"], plan["n_used"], plan["first"], plan["slot"], plan["next_e"], xs, w_e1, w_e2)


def _combine_kernel(dest_ref, dest_next_ref, ys_ref, info_ref, z_ref, mod_ref, *rest, emit_next):
    if emit_next:
        gn_ref, modn_ref, o_ref, hn_ref, buf_ref, sem = rest
    else:
        o_ref, buf_ref, sem = rest
    i = pl.program_id(0)
    tm = z_ref.shape[0]
    slot = i % 2

    def issue(d_ref, s):
        for t in range(tm):
            for k in range(2):
                pltpu.make_async_copy(ys_ref.at[pl.ds(d_ref[0, 0, 2 * t + k], 1)],
                                      buf_ref.at[s, k, pl.ds(t, 1)], sem.at[s]).start(priority=k)

    @pl.when(i == 0)
    def _():
        issue(dest_ref, 0)

    @pl.when(i + 1 < pl.num_programs(0))
    def _():
        issue(dest_next_ref, 1 - slot)

    for k in range(2):
        pltpu.make_async_copy(ys_ref.at[pl.ds(0, tm)], buf_ref.at[slot, k], sem.at[slot]).wait()
    info = info_ref[...]
    lane = lax.broadcasted_iota(jnp.int32, info.shape, 1)
    w1 = jnp.sum(jnp.where(lane == 2, info, 0.0), axis=-1, keepdims=True)
    w2 = jnp.sum(jnp.where(lane == 3, info, 0.0), axis=-1, keepdims=True)
    lo1, hi1 = _unpack_halves(buf_ref[slot, 0])
    lo2, hi2 = _unpack_halves(buf_ref[slot, 1])
    y = jnp.concatenate([lo1 * w1 + lo2 * w2, hi1 * w1 + hi2 * w2], axis=1)
    z = z_ref[...] + mod_ref[5:6, :] * y
    o_ref[...] = z
    if emit_next:
        hn_ref[...] = _norm_mod(z, gn_ref[...], modn_ref[...], 0, 1).astype(hn_ref.dtype)


def _combine_call(dest3, ys, info, z, modsel, layer, ctx_rows, next_norm_w=None):
    rows, d = z.shape
    tm = dest3.shape[2] // 2
    ctx_tiles = ctx_rows // tm
    n_tiles = rows // tm
    emit_next = next_norm_w is not None
    mod_spec = lambda l: pl.BlockSpec((None, None, MOD_ROWS, d),
                                      lambda i: (l, jnp.where(i >= ctx_tiles, 1, 0), 0, 0))
    row_spec = pl.BlockSpec((tm, d), lambda i: (i, 0))
    in_specs = [pl.BlockSpec((1, 1, 2 * tm), lambda i: (i, 0, 0), memory_space=pltpu.SMEM),
                pl.BlockSpec((1, 1, 2 * tm), lambda i: (jnp.minimum(i + 1, n_tiles - 1), 0, 0),
                             memory_space=pltpu.SMEM),
                pl.BlockSpec(memory_space=pl.ANY),
                pl.BlockSpec((tm, LANES), lambda i: (i, 0)),
                row_spec,
                mod_spec(layer)]
    args = [dest3, dest3, ys, info, z, modsel]
    if emit_next:
        out_specs = [row_spec]
        out_shape = [jax.ShapeDtypeStruct((rows, d), F32)]
    else:
        out_specs = [pl.BlockSpec((tm, d), lambda i: (jnp.maximum(i - ctx_tiles, 0), 0))]
        out_shape = [jax.ShapeDtypeStruct((rows - ctx_rows, d), F32)]
    if emit_next:
        in_specs += [pl.BlockSpec((None, 1, d), lambda i: (layer + 1, 0, 0)), mod_spec(layer + 1)]
        args += [next_norm_w, modsel]
        out_specs.append(row_spec)
        out_shape.append(jax.ShapeDtypeStruct((rows, d), ACT_DTYPE))
    return pl.pallas_call(
        functools.partial(_combine_kernel, emit_next=emit_next),
        grid=(n_tiles,),
        in_specs=in_specs,
        out_specs=out_specs,
        out_shape=out_shape,
        scratch_shapes=[pltpu.VMEM((2, 2, tm, ys.shape[1]), ys.dtype), pltpu.SemaphoreType.DMA((2,))],
        compiler_params=_params(40, 1),
        name="moe_combine",
    )(*args)


def _rope_tables(n, ctx_rows):
    rows = n // GRID_W
    row = jnp.repeat(jnp.arange(rows, dtype=F32), GRID_W)
    col = jnp.tile(jnp.arange(GRID_W, dtype=F32), rows)
    nq = HEAD_DIM // 4
    inv = ROPE_BASE ** (-jnp.arange(nq, dtype=F32) / nq)
    ar, ac = row[:, None] * inv, col[:, None] * inv
    cos = jnp.concatenate([jnp.cos(ar), jnp.cos(ar), jnp.cos(ac), jnp.cos(ac)], axis=1)
    sin = jnp.concatenate([-jnp.sin(ar), jnp.sin(ar), -jnp.sin(ac), jnp.sin(ac)], axis=1)
    cos = jnp.concatenate([jnp.ones((ctx_rows, HEAD_DIM), F32), cos], axis=0)
    sin = jnp.concatenate([jnp.zeros((ctx_rows, HEAD_DIM), F32), sin], axis=0)
    return cos, sin


def _retention_tables(c_decay_fwd, c_decay_bwd):
    lg_f = jax.nn.log_sigmoid(c_decay_fwd.astype(F32))[:, :, None, None]
    lg_b = jax.nn.log_sigmoid(c_decay_bwd.astype(F32))[:, :, None, None]
    idx = jnp.arange(CHUNK, dtype=F32)
    diff = idx[:, None] - idx[None, :]
    ones = jnp.ones((CHUNK, CHUNK), F32)
    t_col = idx[:, None] * ones
    intra_f = jnp.where(diff >= 0, jnp.exp(lg_f * jnp.maximum(diff, 0.0)), 0.0)
    intra_b = jnp.where(diff <= 0, jnp.exp(lg_b * jnp.maximum(-diff, 0.0)), 0.0)
    qdec_f = jnp.exp(lg_f * (t_col + 1.0))
    qdec_b = jnp.exp(lg_b * (CHUNK - t_col))
    kdec_f = jnp.exp(lg_f * (CHUNK - 1.0 - t_col))
    kdec_b = jnp.exp(lg_b * t_col)
    cdec_f = jnp.exp(lg_f * CHUNK) * ones
    cdec_b = jnp.exp(lg_b * CHUNK) * ones
    pair = lambda a, b: jnp.stack([a, b], axis=1)
    return (pair(intra_f, intra_b), pair(qdec_f, qdec_b), pair(kdec_f, kdec_b), pair(cdec_f, cdec_b))


def _moe_plan(info, counts_row, n_slots_blocks):
    e = info[:, 0:2].astype(jnp.int32)
    rank = info[:, 4:6].astype(jnp.int32)
    counts = counts_row[:N_EXPERTS].astype(jnp.int32)
    padded = (counts + MOE_BLOCK - 1) // MOE_BLOCK * MOE_BLOCK
    pad_end = jnp.cumsum(padded)
    pad_start = pad_end - padded
    hit = e[:, :, None] == jnp.arange(N_EXPERTS, dtype=jnp.int32)
    dest = jnp.sum(jnp.where(hit, pad_start, 0), axis=-1) + rank
    n_used = pad_end[-1] // MOE_BLOCK
    blocks = jnp.arange(n_slots_blocks, dtype=jnp.int32)
    first_slot = jnp.minimum(blocks, n_used - 1) * MOE_BLOCK
    block_e = jnp.sum((pad_end[None, :] <= first_slot[:, None]).astype(jnp.int32), axis=1)
    ids = jnp.arange(N_EXPERTS, dtype=jnp.int32)
    nonempty = counts > 0
    slot_of = (jnp.cumsum(nonempty.astype(jnp.int32)) - 1) % 2
    later = nonempty[None, :] & (ids[None, :] > ids[:, None])
    next_of = jnp.min(jnp.where(later, ids[None, :], N_EXPERTS), axis=1)
    next_of = jnp.where(next_of == N_EXPERTS, -1, next_of)
    pick = lambda table: jnp.sum(jnp.where(block_e[:, None] == ids[None, :], table[None, :], 0), axis=1)
    prev_e = jnp.concatenate([jnp.full((1,), -1, jnp.int32), block_e[:-1]])
    first = ((block_e != prev_e) & (blocks < n_used)).astype(jnp.int32)
    i32 = lambda a: a.astype(jnp.int32)
    return dict(dest=dest, block_e=i32(block_e), n_used=i32(n_used.reshape(1)), pad_end=i32(pad_end),
                padded=i32(padded), first=first, slot=i32(pick(slot_of)), next_e=i32(pick(next_of)))


def kernel(x, c, ctx, c_ctx, norm_mix, norm_ffn, w_ada, b_ada, w_in, a_q_norm, a_k_norm, a_sink,
           b_norm, b_spatial, b_spatial_bias, c_decay_fwd, c_decay_bwd, c_norm, w_branch, w_out,
           w_router_group, b_router_group, w_router_expert, b_router_expert, w_expert_in, w_expert_out):
    batch, n, d = x.shape
    ctx_rows = ctx.shape[1]
    depth = w_in.shape[0]
    assert batch == 1 and ctx_rows % 256 == 0 and n % 256 == 0
    rows = ctx_rows + n

    cond = jnp.stack([c[0], c_ctx], axis=0)
    cond_b = jnp.broadcast_to(cond[:, :, None], (2, d, LANES))
    mod = _ada_call(cond_b, w_ada, b_ada)[:, :2].reshape(depth, 2, N_MOD, d)
    modsel = jnp.pad(mod[:, ::-1], ((0, 0), (0, 0), (0, MOD_ROWS - N_MOD), (0, 0)))

    cos_t, sin_t = _rope_tables(n, ctx_rows)
    ret_tables = _retention_tables(c_decay_fwd, c_decay_bwd)
    sink_col = jnp.broadcast_to(a_sink.astype(F32).reshape(depth, A_KV_HEADS, A_GROUP, 1, 1),
                                (depth, A_KV_HEADS, A_GROUP, A_BLOCK, 1)).reshape(depth, A_KV_HEADS, A_GROUP * A_BLOCK, 1)
    ws = b_spatial.astype(MXU_DTYPE)
    bias_b = jnp.broadcast_to(b_spatial_bias.astype(F32)[:, :, :, None], (depth, B_GROUPS, CHUNK, LANES))
    wb = w_branch.astype(MXU_DTYPE)
    wo = w_out
    w_r = jnp.concatenate([w_router_group, w_router_expert], axis=-1).astype(F32)
    w_r = jnp.pad(w_r, ((0, 0), (0, 0), (0, LANES - w_r.shape[-1])))
    w_r_hi = w_r.astype(MXU_DTYPE)
    w_r_lo = (w_r - w_r_hi.astype(F32)).astype(MXU_DTYPE)
    b_r = jnp.concatenate([b_router_group, b_router_expert], axis=-1).astype(F32)
    b_r = jnp.pad(b_r, ((0, 0), (0, LANES - b_r.shape[-1]))).reshape(depth, 1, LANES)
    norm_mix3 = norm_mix.reshape(depth, 1, d)
    norm_ffn3 = norm_ffn.reshape(depth, 1, d)
    lane_ids = jnp.arange(HEAD_DIM)
    partner = jnp.where((lane_ids // 32) % 2 == 0, lane_ids + 32, lane_ids - 32)
    perm = (lane_ids[:, None] == partner[None, :]).astype(ACT_DTYPE)
    a_q_norm3 = jnp.stack([a_q_norm, a_q_norm[:, partner]], axis=1).astype(F32)
    a_k_norm3 = jnp.stack([a_k_norm, a_k_norm[:, partner]], axis=1).astype(F32)
    b_norm3 = b_norm.reshape(depth, 1, BRANCH_WIDTH)
    c_norm3 = c_norm.reshape(depth, 1, BRANCH_WIDTH)

    kvw = A_KV_HEADS * HEAD_DIM
    w = BRANCH_WIDTH
    n_assign = rows * 2
    n_slot_blocks = -(-(n_assign + N_EXPERTS * (MOE_BLOCK - 1)) // MOE_BLOCK)
    tok_tile = 256
    disp_tile = _row_tile(rows, (1056, 768, 256))

    z, h = _first_norm_call(ctx[0], x[0], norm_mix3, modsel)
    for l in range(depth):
        g12 = _proj_call(h, w_in, l, 0, 2 * kvw + 2 * w)
        g3 = _proj_call(h, w_in, l, 2 * kvw + 2 * w, 5 * w)
        g4 = _proj_call(h, w_in, l, 2 * kvw + 7 * w, N_BRANCH * d)
        kn, qn, kr, qr = _prep_call(g12, g3, cos_t, sin_t, a_q_norm3, a_k_norm3, perm, l)
        attn = _attn_call(qn, kn, g12, sink_col, l, ctx_rows)
        gm = _gmlp_call(g3, b_norm3, ws, bias_b, l)
        o_f, o_b = _ret_call(qr, kr, g12, ret_tables, l, ctx_rows)
        merged = _merge_call(attn, gm, o_f, o_b, g3, g4, c_norm3, wb, l)
        z, h2, info, counts = _route_call(merged, wo, z, norm_ffn3, modsel, w_r_hi, w_r_lo, b_r, l, ctx_rows)
        plan = _moe_plan(info, counts[0], n_slot_blocks)
        dest3 = plan["dest"].reshape(rows // tok_tile, 1, 2 * tok_tile)
        dest3_disp = plan["dest"].reshape(rows // disp_tile, 1, 2 * disp_tile)
        xs = _dispatch_call(plan["pad_end"], plan["padded"], plan["n_used"], dest3_disp, h2,
                            n_slot_blocks * MOE_BLOCK)
        ys = _expert_call(plan, xs, w_expert_in, w_expert_out, l)
        if l + 1 < depth:
            z, h = _combine_call(dest3, ys, info, z, modsel, l, ctx_rows, next_norm_w=norm_mix3)
        else:
            (z_latent,) = _combine_call(dest3, ys, info, z, modsel, l, ctx_rows)
    return z_latent[None]
```
